```python
import math
import jax, jax.numpy as jnp
from jax import lax
import numpy as np

D_MODEL = 1024
BATCH = 1
SEQ = 16384
DEPTH = 1
DEC_BATCH = 16
DEC_SEQ = 32
PAST_LEN = 4096

CHUNK = 64
N_HEADS = 4
HEAD_DIM = 64
V_DIM = 2 * HEAD_DIM
Q_WIDTH = N_HEADS * 2 * HEAD_DIM
ATTN_WIDTH = N_HEADS * V_DIM
CONV_CH = D_MODEL - ATTN_WIDTH
CONV_WIDTH = 31
D_FF = 2816
IN_COLS = 2 * Q_WIDTH + ATTN_WIDTH + 2 * CONV_CH
Q_BLOCK = 128
EPS = 1e-6
NEG_INF = -1e30
ALIBI_SLOPES = tuple(2.0 ** (-8.0 * (h + 1) / N_HEADS) for h in range(N_HEADS))

kernel_name = "hymba_diffattn_conformer_conv_stream_step"


def _rms(x, g):
    x32 = x.astype(jnp.float32)
    y = x32 * lax.rsqrt(jnp.mean(x32 * x32, axis=-1, keepdims=True) + EPS)
    return (y * g.astype(jnp.float32)).astype(x.dtype)


def _layernorm(x, g, b):
    x32 = x.astype(jnp.float32)
    mu = jnp.mean(x32, axis=-1, keepdims=True)
    var = jnp.mean(jnp.square(x32 - mu), axis=-1, keepdims=True)
    y = (x32 - mu) * lax.rsqrt(var + EPS) * g.astype(jnp.float32) + b.astype(jnp.float32)
    return y.astype(x.dtype)


def _swiglu_ffn(x, g, w_gu, w_down):
    h = _rms(x, g)
    a, b = jnp.split(h @ w_gu, 2, axis=-1)
    return (jax.nn.silu(a) * b) @ w_down


def _diff_lambda(lq1, lk1, lq2, lk2, lam_init):
    f32 = jnp.float32
    return (jnp.exp(jnp.sum(lq1.astype(f32) * lk1.astype(f32)))
            - jnp.exp(jnp.sum(lq2.astype(f32) * lk2.astype(f32))) + lam_init)


def _project(h, w_in, g_q, g_k):
    B, T, _ = h.shape
    z = h @ w_in
    q = z[..., :Q_WIDTH].reshape(B, T, N_HEADS, 2, HEAD_DIM)
    k = z[..., Q_WIDTH:2 * Q_WIDTH].reshape(B, T, N_HEADS, 2, HEAD_DIM)
    v = z[..., 2 * Q_WIDTH:2 * Q_WIDTH + ATTN_WIDTH].reshape(B, T, N_HEADS, V_DIM)
    a, gate = jnp.split(z[..., 2 * Q_WIDTH + ATTN_WIDTH:], 2, axis=-1)
    u = a * jax.nn.sigmoid(gate)
    return _rms(q, g_q), _rms(k, g_k), v, u


def _diff_attend(q, k, v, q_pos, k_pos, lam):
    s = jnp.einsum('bqhpd,bkhpd->bhpqk', q.astype(jnp.float32), k.astype(jnp.float32)) * (HEAD_DIM ** -0.5)
    slopes = jnp.asarray(ALIBI_SLOPES, jnp.float32)
    dist = jnp.abs(q_pos[:, None] - k_pos[None, :]).astype(jnp.float32)
    bias = -slopes[:, None, None, None] * dist
    visible = (k_pos[None, :] // CHUNK) <= (q_pos[:, None] // CHUNK)
    p = jax.nn.softmax(jnp.where(visible, s + bias, NEG_INF), axis=-1)
    w = p[:, :, 0] - lam * p[:, :, 1]
    o = jnp.einsum('bhqk,bkhd->bqhd', w, v.astype(jnp.float32))
    return o.astype(v.dtype)


def _prompt_attend(q, k, v, lam):
    B, T = q.shape[0], q.shape[1]
    nb = T // Q_BLOCK
    pos = jnp.arange(T, dtype=jnp.int32)
    qb = q.reshape(B, nb, Q_BLOCK, N_HEADS, 2, HEAD_DIM).swapaxes(0, 1)
    pb = pos.reshape(nb, Q_BLOCK)
    ob = lax.map(lambda a: _diff_attend(a[0], k, v, a[1], pos, lam), (qb, pb))
    return ob.swapaxes(0, 1).reshape(B, T, N_HEADS, V_DIM)


def _head_out(o, g_sub, lam_init):
    B, T = o.shape[0], o.shape[1]
    return (_rms(o, g_sub) * (1.0 - lam_init)).reshape(B, T, ATTN_WIDTH)


def _conv_branch(u, buf, w_dw, b_dw, g_ln, b_ln):
    xp = jnp.concatenate([buf.astype(u.dtype), u], axis=1)
    y = lax.conv_general_dilated(xp, w_dw[:, None, :].astype(u.dtype), (1,), 'VALID',
                                 dimension_numbers=('NWC', 'WIO', 'NWC'),
                                 feature_group_count=CONV_CH) + b_dw
    y = jax.nn.silu(_layernorm(y, g_ln, b_ln))
    return y, xp[:, -(CONV_WIDTH - 1):]


def _block(x, attend, conv_buf, lam_init, g_ffn1, w_ffn1_gu, w_ffn1_down, g_mix, w_in, g_q, g_k,
           g_sub, w_dw, b_dw, g_conv_ln, b_conv_ln, w_out, g_ffn2, w_ffn2_gu, w_ffn2_down, g_final):
    x = x + 0.5 * _swiglu_ffn(x, g_ffn1, w_ffn1_gu, w_ffn1_down)
    h = _rms(x, g_mix)
    q, k, v, u = _project(h, w_in, g_q, g_k)
    attn = _head_out(attend(q, k, v), g_sub, lam_init)
    conv, new_buf = _conv_branch(u, conv_buf, w_dw, b_dw, g_conv_ln, b_conv_ln)
    x = x + jnp.concatenate([attn, conv], axis=-1) @ w_out
    x = x + 0.5 * _swiglu_ffn(x, g_ffn2, w_ffn2_gu, w_ffn2_down)
    return _rms(x, g_final), k, v, new_buf


def setup_inputs(seed: int = 0) -> dict:
    key = jax.random.key(seed)
    ks = iter(jax.random.split(key, 40))
    f32 = jnp.float32

    def nrm(shape, scale):
        return jax.random.normal(next(ks), shape, f32) * scale

    def gain(shape):
        return 1.0 + nrm(shape, 0.02)

    L = DEPTH
    return {
        "x_prompt": nrm((BATCH, SEQ, D_MODEL), 1.0),
        "x_sample": nrm((DEC_BATCH, DEC_SEQ, D_MODEL), 1.0),
        "cache_k": nrm((L, DEC_BATCH, PAST_LEN, N_HEADS, 2, HEAD_DIM), 1.0),
        "cache_v": nrm((L, DEC_BATCH, PAST_LEN, N_HEADS, V_DIM), 1.0),
        "cache_conv": nrm((L, DEC_BATCH, CONV_WIDTH - 1, CONV_CH), 0.5),
        "g_ffn1": gain((L, D_MODEL)),
        "w_ffn1_gu": nrm((L, D_MODEL, 2 * D_FF), D_MODEL ** -0.5),
        "w_ffn1_down": nrm((L, D_FF, D_MODEL), D_FF ** -0.5),
        "g_mix": gain((L, D_MODEL)),
        "w_in": nrm((L, D_MODEL, IN_COLS), D_MODEL ** -0.5),
        "g_q": gain((L, HEAD_DIM)),
        "g_k": gain((L, HEAD_DIM)),
        "lambda_q1": nrm((L, HEAD_DIM), 0.1),
        "lambda_k1": nrm((L, HEAD_DIM), 0.1),
        "lambda_q2": nrm((L, HEAD_DIM), 0.1),
        "lambda_k2": nrm((L, HEAD_DIM), 0.1),
        "g_sub": gain((L, V_DIM)),
        "w_dw": nrm((L, CONV_WIDTH, CONV_CH), CONV_WIDTH ** -0.5),
        "b_dw": nrm((L, CONV_CH), 0.02),
        "g_conv_ln": gain((L, CONV_CH)),
        "b_conv_ln": nrm((L, CONV_CH), 0.02),
        "w_out": nrm((L, D_MODEL, D_MODEL), D_MODEL ** -0.5),
        "g_ffn2": gain((L, D_MODEL)),
        "w_ffn2_gu": nrm((L, D_MODEL, 2 * D_FF), D_MODEL ** -0.5),
        "w_ffn2_down": nrm((L, D_FF, D_MODEL), D_FF ** -0.5),
        "g_final": gain((L, D_MODEL)),
    }


def reference(x_prompt, x_sample, cache_k, cache_v, cache_conv, g_ffn1, w_ffn1_gu, w_ffn1_down,
              g_mix, w_in, g_q, g_k, lambda_q1, lambda_k1, lambda_q2, lambda_k2, g_sub, w_dw, b_dw,
              g_conv_ln, b_conv_ln, w_out, g_ffn2, w_ffn2_gu, w_ffn2_down, g_final):
    n_prompt = x_prompt.shape[0]
    t_new = x_sample.shape[1]
    past = cache_k.shape[2]
    q_pos_s = past + jnp.arange(t_new, dtype=jnp.int32)
    k_pos_s = jnp.arange(past + t_new, dtype=jnp.int32)

    yp, ys = x_prompt, x_sample
    kp_l, vp_l, cp_l, ks_l, vs_l, cs_l = [], [], [], [], [], []
    for l in range(DEPTH):
        lam_init = 0.8 - 0.6 * math.exp(-0.3 * l)
        lam = _diff_lambda(lambda_q1[l], lambda_k1[l], lambda_q2[l], lambda_k2[l], lam_init)
        lw = (g_ffn1[l], w_ffn1_gu[l], w_ffn1_down[l], g_mix[l], w_in[l], g_q[l], g_k[l], g_sub[l],
              w_dw[l], b_dw[l], g_conv_ln[l], b_conv_ln[l], w_out[l], g_ffn2[l], w_ffn2_gu[l],
              w_ffn2_down[l], g_final[l])

        def attend_prompt(q, k, v, lam=lam):
            return _prompt_attend(q, k, v, lam)

        def attend_sample(q, k, v, lam=lam, l=l):
            k_all = jnp.concatenate([cache_k[l].astype(k.dtype), k], axis=1)
            v_all = jnp.concatenate([cache_v[l].astype(v.dtype), v], axis=1)
            return _diff_attend(q, k_all, v_all, q_pos_s, k_pos_s, lam)

        zero_buf = jnp.zeros((n_prompt, CONV_WIDTH - 1, CONV_CH), x_prompt.dtype)
        yp, kp, vp, cp = _block(yp, attend_prompt, zero_buf, lam_init, *lw)
        ys, ks_, vs_, cs_ = _block(ys, attend_sample, cache_conv[l], lam_init, *lw)
        kp_l.append(kp); vp_l.append(vp); cp_l.append(cp)
        ks_l.append(ks_); vs_l.append(vs_); cs_l.append(cs_)

    new_k_prompt = jnp.stack(kp_l, axis=0)
    new_v_prompt = jnp.stack(vp_l, axis=0)
    new_conv_prompt = jnp.stack(cp_l, axis=0)
    new_k_sample = jnp.stack(ks_l, axis=0)
    new_v_sample = jnp.stack(vs_l, axis=0)
    new_conv_sample = jnp.stack(cs_l, axis=0)
    return (yp, ys, new_k_prompt, new_v_prompt, new_conv_prompt, new_k_sample, new_v_sample, new_conv_sample)
```

```python
import functools
import math

import jax
import jax.numpy as jnp
from jax import lax
from jax.experimental import pallas as pl
from jax.experimental.pallas import tpu as pltpu

N_HEADS = 4
HEAD_DIM = 64
V_DIM = 2 * HEAD_DIM
Q_WIDTH = N_HEADS * 2 * HEAD_DIM
ATTN_WIDTH = N_HEADS * V_DIM
CHUNK = 64
CONV_WIDTH = 31
EPS = 1e-6
NEG_INF = -1e30
ALIBI_SLOPES = tuple(2.0 ** (-8.0 * (h + 1) / N_HEADS) for h in range(N_HEADS))
LOG2E = math.log2(math.e)

V7X_LANES = 128
V7X_VMEM_BYTES = 64 * 1024 * 1024
VMEM_LIMIT_BYTES = V7X_VMEM_BYTES - 8 * 1024 * 1024

HALO_ROWS = 32

F32 = jnp.float32
BF16 = jnp.bfloat16


def _dot(a, b):
    return jnp.dot(a, b, preferred_element_type=F32)


def _dot_nt(a, b):
    return lax.dot_general(a, b, (((1,), (1,)), ((), ())), preferred_element_type=F32)


def _rms_rows(x, g):
    return x * lax.rsqrt(jnp.mean(x * x, axis=-1, keepdims=True) + EPS) * g


def _params(n_grid_axes):
    return pltpu.CompilerParams(
        dimension_semantics=("arbitrary",) * n_grid_axes,
        vmem_limit_bytes=VMEM_LIMIT_BYTES,
    )


def _const_spec(shape):
    nd = len(shape)
    return pl.BlockSpec(shape, lambda *_: (0,) * nd, pipeline_mode=pl.Buffered(1))


def _row_tile(n_rows, want):
    t = min(want, n_rows)
    assert n_rows % t == 0, (n_rows, t)
    return t


def _ffn_kernel(*refs, has_mix, has_final):
    it = iter(refs)
    x_ref = next(it)
    if has_mix:
        attn_ref, conv_ref, woa_ref, wob_ref = next(it), next(it), next(it), next(it)
    g_ref, wg_ref, wu_ref, wd_ref = next(it), next(it), next(it), next(it)
    gf_ref = next(it) if has_final else None
    out_ref = next(it)

    x = x_ref[...]
    if has_mix:
        x = x + _dot(attn_ref[...], woa_ref[...]) + _dot(conv_ref[...], wob_ref[...])
    h = _rms_rows(x, g_ref[...]).astype(BF16)
    a = _dot(h, wg_ref[...])
    b = _dot(h, wu_ref[...])
    act = (a * jax.nn.sigmoid(a) * b).astype(BF16)
    y = x + 0.5 * _dot(act, wd_ref[...])
    if has_final:
        y = _rms_rows(y, gf_ref[...])
    out_ref[...] = y


def _ffn(x, g, wg, wu, wd, mix=None, g_final=None, name="ffn"):
    T, D = x.shape
    F = wg.shape[1]
    tm = _row_tile(T, 256)
    row = lambda w: pl.BlockSpec((tm, w), lambda i: (i, 0))
    args, specs = [x], [row(D)]
    if mix is not None:
        attn, conv, woa, wob = mix
        args += [attn, conv, woa, wob]
        specs += [row(attn.shape[1]), row(conv.shape[1]), _const_spec(woa.shape), _const_spec(wob.shape)]
    args += [g, wg, wu, wd]
    specs += [_const_spec((1, D)), _const_spec((D, F)), _const_spec((D, F)), _const_spec((F, D))]
    if g_final is not None:
        args.append(g_final)
        specs.append(_const_spec((1, D)))
    return pl.pallas_call(
        functools.partial(_ffn_kernel, has_mix=mix is not None, has_final=g_final is not None),
        grid=(T // tm,),
        in_specs=specs,
        out_specs=row(D),
        out_shape=jax.ShapeDtypeStruct((T, D), F32),
        compiler_params=_params(1),
        name=name,
    )(*args)


def _group_rms(y, g, ones_bd):
    y2 = y * y
    hi = y2.astype(BF16)
    lo = (y2 - hi.astype(F32)).astype(BF16)
    ss = _dot(hi, ones_bd) + _dot(lo, ones_bd)
    return y * lax.rsqrt(ss * (1.0 / HEAD_DIM) + EPS) * g


def _proj_kernel(x_ref, g_ref, w_ref, gq_ref, gk_ref, ones_ref,
                 qs_ref, kb_ref, vb_ref, k32_ref, v32_ref, u_ref):
    h = _rms_rows(x_ref[...], g_ref[...]).astype(BF16)
    z = _dot(h, w_ref[...])
    ones_bd = ones_ref[...]
    qn = _group_rms(z[:, :Q_WIDTH], gq_ref[...], ones_bd) * (HEAD_DIM ** -0.5 * LOG2E)
    kn = _group_rms(z[:, Q_WIDTH:2 * Q_WIDTH], gk_ref[...], ones_bd)
    v = z[:, 2 * Q_WIDTH:2 * Q_WIDTH + ATTN_WIDTH]
    conv_ch = (z.shape[1] - 2 * Q_WIDTH - ATTN_WIDTH) // 2
    a = z[:, 2 * Q_WIDTH + ATTN_WIDTH:2 * Q_WIDTH + ATTN_WIDTH + conv_ch]
    gate = z[:, 2 * Q_WIDTH + ATTN_WIDTH + conv_ch:]

    lane = lax.broadcasted_iota(jnp.int32, (x_ref.shape[0], V_DIM), 1)
    for hh in range(N_HEADS):
        sl = slice(V_DIM * hh, V_DIM * (hh + 1))
        qh = qn[:, sl]
        qs_ref[hh, 0] = jnp.where(lane < HEAD_DIM, qh, 0.0).astype(BF16)
        qs_ref[hh, 1] = jnp.where(lane >= HEAD_DIM, qh, 0.0).astype(BF16)
        kb_ref[hh] = kn[:, sl].astype(BF16)
        vb_ref[hh] = v[:, sl].astype(BF16)
    k32_ref[...] = kn
    v32_ref[...] = v
    u_ref[...] = a * jax.nn.sigmoid(gate)


def _project(x1, g_mix, w_in, gq_t, gk_t, ones_bd):
    T, D = x1.shape
    C = w_in.shape[1]
    conv_ch = (C - 2 * Q_WIDTH - ATTN_WIDTH) // 2
    tm = _row_tile(T, 256)
    row = lambda w: pl.BlockSpec((tm, w), lambda i: (i, 0))
    return pl.pallas_call(
        _proj_kernel,
        grid=(T // tm,),
        in_specs=[row(D), _const_spec((1, D)), _const_spec((D, C)), _const_spec((1, Q_WIDTH)),
                  _const_spec((1, Q_WIDTH)), _const_spec((Q_WIDTH, Q_WIDTH))],
        out_specs=[
            pl.BlockSpec((N_HEADS, 2, tm, V_DIM), lambda i: (0, 0, i, 0)),
            pl.BlockSpec((N_HEADS, tm, V_DIM), lambda i: (0, i, 0)),
            pl.BlockSpec((N_HEADS, tm, V_DIM), lambda i: (0, i, 0)),
            row(Q_WIDTH), row(ATTN_WIDTH), row(conv_ch),
        ],
        out_shape=[
            jax.ShapeDtypeStruct((N_HEADS, 2, T, V_DIM), BF16),
            jax.ShapeDtypeStruct((N_HEADS, T, V_DIM), BF16),
            jax.ShapeDtypeStruct((N_HEADS, T, V_DIM), BF16),
            jax.ShapeDtypeStruct((T, Q_WIDTH), F32),
            jax.ShapeDtypeStruct((T, ATTN_WIDTH), F32),
            jax.ShapeDtypeStruct((T, conv_ch), F32),
        ],
        compiler_params=_params(1),
        name="proj",
    )(x1, g_mix, w_in, gq_t, gk_t, ones_bd)


def _diff_lambda(lam_ref, lam_init):
    lp = lam_ref[...]
    s1 = jnp.sum(lp[0:1] * lp[1:2], axis=-1, keepdims=True)
    s2 = jnp.sum(lp[2:3] * lp[3:4], axis=-1, keepdims=True)
    return jnp.exp(s1) - jnp.exp(s2) + lam_init


def _softmax_step(s, v, m_ref, l_ref, acc_ref):
    m_prev = m_ref[...]
    m_new = jnp.maximum(m_prev, jnp.max(s, axis=1, keepdims=True))
    alpha = jnp.exp2(m_prev - m_new)
    p = jnp.exp2(s - m_new)
    l_ref[...] = alpha * l_ref[...] + jnp.sum(p, axis=1, keepdims=True)
    acc_ref[...] = alpha * acc_ref[...] + _dot(p.astype(BF16), v)
    m_ref[...] = m_new


def _head_out(o1, o2, lam, g_sub, lam_init):
    d = o1 - lam * o2
    return _rms_rows(d, g_sub) * (1.0 - lam_init)


def _prompt_attn_kernel(qs_ref, kb_ref, vb_ref, lam_ref, gsub_ref, out_ref, m_ref, l_ref, acc_ref,
                        *, tq, lam_init):
    hh = pl.program_id(0)
    qi = pl.program_id(1)
    rows = 2 * tq
    q = qs_ref[...].reshape(rows, V_DIM)
    slope2 = jnp.float32(0.0)
    for h_static, sl in enumerate(ALIBI_SLOPES):
        slope2 = jnp.where(hh == h_static, jnp.float32(sl * LOG2E), slope2)

    m_ref[...] = jnp.full(m_ref.shape, NEG_INF, F32)
    l_ref[...] = jnp.zeros(l_ref.shape, F32)
    acc_ref[...] = jnp.zeros(acc_ref.shape, F32)

    key_off = lax.broadcasted_iota(jnp.int32, (1, tq), 1).astype(F32)

    def off_diag(j, carry):
        start = pl.multiple_of(j * tq, tq)
        k = kb_ref[pl.ds(start, tq), :]
        v = vb_ref[pl.ds(start, tq), :]
        bias = slope2 * (key_off + ((j - qi) * tq).astype(F32))
        _softmax_step(_dot_nt(q, k) + bias, v, m_ref, l_ref, acc_ref)
        return carry

    lax.fori_loop(0, qi, off_diag, 0)

    start = pl.multiple_of(qi * tq, tq)
    k = kb_ref[pl.ds(start, tq), :]
    v = vb_ref[pl.ds(start, tq), :]
    r = lax.broadcasted_iota(jnp.int32, (rows, tq), 0)
    qo = jnp.where(r >= tq, r - tq, r)
    ko = lax.broadcasted_iota(jnp.int32, (rows, tq), 1)
    bias = slope2 * (qo - jnp.abs(qo - ko)).astype(F32)
    visible = (ko // CHUNK) <= (qo // CHUNK)
    s = jnp.where(visible, _dot_nt(q, k) + bias, NEG_INF)
    _softmax_step(s, v, m_ref, l_ref, acc_ref)

    o = acc_ref[...] / l_ref[...]
    lam = _diff_lambda(lam_ref, lam_init)
    out_ref[...] = _head_out(o[:tq], o[tq:], lam, gsub_ref[...], lam_init).astype(out_ref.dtype)


def _prompt_attention(qs, kb, vb, lam_rows, g_sub, lam_init):
    T = kb.shape[1]
    tq = _row_tile(T, 256)
    return pl.pallas_call(
        functools.partial(_prompt_attn_kernel, tq=tq, lam_init=lam_init),
        grid=(N_HEADS, T // tq),
        in_specs=[
            pl.BlockSpec((None, 2, tq, V_DIM), lambda h, i: (h, 0, i, 0)),
            pl.BlockSpec((None, T, V_DIM), lambda h, i: (h, 0, 0)),
            pl.BlockSpec((None, T, V_DIM), lambda h, i: (h, 0, 0)),
            pl.BlockSpec((4, HEAD_DIM), lambda h, i: (0, 0)),
            pl.BlockSpec((1, V_DIM), lambda h, i: (0, 0)),
        ],
        out_specs=pl.BlockSpec((tq, V_DIM), lambda h, i: (i, h)),
        out_shape=jax.ShapeDtypeStruct((T, ATTN_WIDTH), BF16),
        scratch_shapes=[
            pltpu.VMEM((2 * tq, 1), F32),
            pltpu.VMEM((2 * tq, 1), F32),
            pltpu.VMEM((2 * tq, V_DIM), F32),
        ],
        compiler_params=_params(2),
        name="prompt_attn",
    )(qs, kb, vb, lam_rows, g_sub)


def _sample_attn_kernel(qs_ref, kn_ref, vn_ref, ck_ref, cv_ref, lam_ref, gsub_ref, out_ref,
                        m_ref, l_ref, acc_ref, *, t_new, past, tk, lam_init):
    grp = 2 * t_new
    rows = N_HEADS * grp
    blocks = []
    for hh in range(N_HEADS):
        qh = qs_ref[hh].reshape(grp, V_DIM)
        z = jnp.zeros((grp, V_DIM), BF16)
        blocks.append(jnp.concatenate([qh if c == hh else z for c in range(N_HEADS)], axis=1))
    q = jnp.concatenate(blocks, axis=0)

    r = lax.broadcasted_iota(jnp.int32, (rows, 1), 0)
    q_pos = past + (r % t_new)
    head = r // grp
    slope2 = jnp.zeros((rows, 1), F32)
    for h_static, sl in enumerate(ALIBI_SLOPES):
        slope2 = jnp.where(head == h_static, jnp.float32(sl * LOG2E), slope2)

    m_ref[...] = jnp.full(m_ref.shape, NEG_INF, F32)
    l_ref[...] = jnp.zeros(l_ref.shape, F32)
    acc_ref[...] = jnp.zeros(acc_ref.shape, F32)

    def step(k, v, k_start, n_keys):
        k_pos = k_start + lax.broadcasted_iota(jnp.int32, (1, n_keys), 1)
        bias = -slope2 * jnp.abs(q_pos - k_pos).astype(F32)
        visible = (k_pos // CHUNK) <= (q_pos // CHUNK)
        s = jnp.where(visible, _dot_nt(q, k) + bias, NEG_INF)
        _softmax_step(s, v, m_ref, l_ref, acc_ref)

    for c in range(past // tk):
        step(ck_ref[c * tk:(c + 1) * tk, :].astype(BF16), cv_ref[c * tk:(c + 1) * tk, :].astype(BF16),
             c * tk, tk)
    step(kn_ref[...].astype(BF16), vn_ref[...].astype(BF16), past, t_new)

    o = acc_ref[...] / l_ref[...]
    lam = _diff_lambda(lam_ref, lam_init)
    outs = []
    for hh in range(N_HEADS):
        sl = slice(V_DIM * hh, V_DIM * (hh + 1))
        o1 = o[hh * grp:hh * grp + t_new, sl]
        o2 = o[hh * grp + t_new:(hh + 1) * grp, sl]
        outs.append(_head_out(o1, o2, lam, gsub_ref[...], lam_init))
    out_ref[...] = jnp.concatenate(outs, axis=1).astype(out_ref.dtype)


def _sample_attention(qs, k_new, v_new, cache_k, cache_v, lam_rows, g_sub, lam_init, n_streams):
    t_new = k_new.shape[0] // n_streams
    past = cache_k.shape[1]
    tk = _row_tile(past, 1024)
    rows = N_HEADS * 2 * t_new
    return pl.pallas_call(
        functools.partial(_sample_attn_kernel, t_new=t_new, past=past, tk=tk, lam_init=lam_init),
        grid=(n_streams,),
        in_specs=[
            pl.BlockSpec((N_HEADS, 2, t_new, V_DIM), lambda b: (0, 0, b, 0)),
            pl.BlockSpec((t_new, Q_WIDTH), lambda b: (b, 0)),
            pl.BlockSpec((t_new, ATTN_WIDTH), lambda b: (b, 0)),
            pl.BlockSpec((None, past, Q_WIDTH), lambda b: (b, 0, 0)),
            pl.BlockSpec((None, past, ATTN_WIDTH), lambda b: (b, 0, 0)),
            pl.BlockSpec((4, HEAD_DIM), lambda b: (0, 0)),
            pl.BlockSpec((1, V_DIM), lambda b: (0, 0)),
        ],
        out_specs=pl.BlockSpec((t_new, ATTN_WIDTH), lambda b: (b, 0)),
        out_shape=jax.ShapeDtypeStruct((n_streams * t_new, ATTN_WIDTH), BF16),
        scratch_shapes=[
            pltpu.VMEM((rows, 1), F32),
            pltpu.VMEM((rows, 1), F32),
            pltpu.VMEM((rows, ATTN_WIDTH), F32),
        ],
        compiler_params=_params(1),
        name="sample_attn",
    )(qs, k_new, v_new, cache_k, cache_v, lam_rows, g_sub)


def _conv_kernel(u_ref, halo_ref, init_ref, w_ref, b_ref, g_ref, beta_ref, out_ref, xp_ref, *, tm):
    i = pl.program_id(1)
    hist = jnp.where(i == 0, init_ref[...], halo_ref[...])
    xp_ref[0:HALO_ROWS, :] = hist
    xp_ref[HALO_ROWS:HALO_ROWS + tm, :] = u_ref[...]
    first = HALO_ROWS - (CONV_WIDTH - 1)
    y = jnp.zeros((tm, u_ref.shape[-1]), F32) + b_ref[...]
    for w in range(CONV_WIDTH):
        y = y + xp_ref[first + w:first + w + tm, :] * w_ref[w:w + 1, :]
    mu = jnp.mean(y, axis=-1, keepdims=True)
    d = y - mu
    var = jnp.mean(d * d, axis=-1, keepdims=True)
    yn = d * lax.rsqrt(var + EPS) * g_ref[...] + beta_ref[...]
    out_ref[...] = (yn * jax.nn.sigmoid(yn)).astype(out_ref.dtype)


def _conv_branch(u, init_hist, w_dw, b_dw, g_ln, b_ln):
    B, T, C = u.shape
    tm = _row_tile(T, 256)
    assert tm % HALO_ROWS == 0
    per = tm // HALO_ROWS
    vec = lambda: pl.BlockSpec((1, C), lambda b, i: (0, 0))
    return pl.pallas_call(
        functools.partial(_conv_kernel, tm=tm),
        grid=(B, T // tm),
        in_specs=[
            pl.BlockSpec((None, tm, C), lambda b, i: (b, i, 0)),
            pl.BlockSpec((None, HALO_ROWS, C), lambda b, i: (b, jnp.maximum(i * per - 1, 0), 0)),
            pl.BlockSpec((None, HALO_ROWS, C), lambda b, i: (b, 0, 0)),
            pl.BlockSpec((CONV_WIDTH, C), lambda b, i: (0, 0)),
            vec(), vec(), vec(),
        ],
        out_specs=pl.BlockSpec((None, tm, C), lambda b, i: (b, i, 0)),
        out_shape=jax.ShapeDtypeStruct((B, T, C), BF16),
        scratch_shapes=[pltpu.VMEM((HALO_ROWS + tm, C), F32)],
        compiler_params=_params(2),
        name="conv",
    )(u, u, init_hist, w_dw, b_dw, g_ln, b_ln)


def _layer(x, n_streams, attend, conv_hist, lam_init, p):
    T = x.shape[0] // n_streams
    x1 = _ffn(x, p["g_ffn1"], p["w1g"], p["w1u"], p["w1d"], name="ffn1")
    qs, kb, vb, k32, v32, u = _project(x1, p["g_mix"], p["w_in"], p["gq_t"], p["gk_t"], p["ones_bd"])
    attn = attend(qs, kb, vb, k32, v32)
    conv_ch = u.shape[1]
    hist = jnp.pad(conv_hist, ((0, 0), (HALO_ROWS - conv_hist.shape[1], 0), (0, 0)))
    conv = _conv_branch(u.reshape(n_streams, T, conv_ch), hist, p["w_dw"], p["b_dw"], p["g_ln"], p["b_ln"])
    y = _ffn(x1, p["g_ffn2"], p["w2g"], p["w2u"], p["w2d"],
             mix=(attn, conv.reshape(n_streams * T, conv_ch), p["wo_attn"], p["wo_conv"]),
             g_final=p["g_final"], name="ffn2")
    return y, k32, v32, u


def kernel(x_prompt, x_sample, cache_k, cache_v, cache_conv, g_ffn1, w_ffn1_gu, w_ffn1_down, g_mix, w_in, g_q, g_k, lambda_q1, lambda_k1, lambda_q2, lambda_k2, g_sub, w_dw, b_dw, g_conv_ln, b_conv_ln, w_out, g_ffn2, w_ffn2_gu, w_ffn2_down, g_final):
    depth = cache_k.shape[0]
    n_p, t_p, d_model = x_prompt.shape
    n_s, t_s, _ = x_sample.shape
    past = cache_k.shape[2]
    d_ff = w_ffn1_down.shape[1]
    conv_ch = cache_conv.shape[-1]
    hist_rows = CONV_WIDTH - 1
    assert n_p == 1, "prompt attention kernel handles one prompt stream"
    assert t_p >= hist_rows and t_s >= hist_rows

    group = jnp.arange(Q_WIDTH) // HEAD_DIM
    ones_bd = (group[:, None] == group[None, :]).astype(BF16)
    row = lambda a: a.reshape(1, -1).astype(F32)

    yp = x_prompt.reshape(n_p * t_p, d_model)
    ys = x_sample.reshape(n_s * t_s, d_model)
    outs = [[] for _ in range(6)]
    for l in range(depth):
        lam_init = 0.8 - 0.6 * math.exp(-0.3 * l)
        lam_rows = jnp.stack([lambda_q1[l], lambda_k1[l], lambda_q2[l], lambda_k2[l]]).astype(F32)
        p = dict(
            g_ffn1=row(g_ffn1[l]), w1g=w_ffn1_gu[l][:, :d_ff].astype(BF16), w1u=w_ffn1_gu[l][:, d_ff:].astype(BF16),
            w1d=w_ffn1_down[l].astype(BF16), g_mix=row(g_mix[l]), w_in=w_in[l].astype(BF16),
            gq_t=row(jnp.tile(g_q[l], 2 * N_HEADS)), gk_t=row(jnp.tile(g_k[l], 2 * N_HEADS)), ones_bd=ones_bd,
            w_dw=w_dw[l].astype(F32), b_dw=row(b_dw[l]), g_ln=row(g_conv_ln[l]), b_ln=row(b_conv_ln[l]),
            wo_attn=w_out[l][:ATTN_WIDTH].astype(BF16), wo_conv=w_out[l][ATTN_WIDTH:].astype(BF16),
            g_ffn2=row(g_ffn2[l]), w2g=w_ffn2_gu[l][:, :d_ff].astype(BF16), w2u=w_ffn2_gu[l][:, d_ff:].astype(BF16),
            w2d=w_ffn2_down[l].astype(BF16), g_final=row(g_final[l]),
        )
        g_sub_row = row(g_sub[l])

        def attend_prompt(qs, kb, vb, k32, v32):
            return _prompt_attention(qs, kb, vb, lam_rows, g_sub_row, lam_init)

        def attend_sample(qs, kb, vb, k32, v32, l=l):
            ck = cache_k[l].reshape(n_s, past, Q_WIDTH)
            cv = cache_v[l].reshape(n_s, past, ATTN_WIDTH)
            return _sample_attention(qs, k32, v32, ck, cv, lam_rows, g_sub_row, lam_init, n_s)

        zero_hist = jnp.zeros((n_p, hist_rows, conv_ch), F32)
        yp, kp, vp, up = _layer(yp, n_p, attend_prompt, zero_hist, lam_init, p)
        ys, ks, vs, us = _layer(ys, n_s, attend_sample, cache_conv[l], lam_init, p)
        outs[0].append(kp.reshape(n_p, t_p, N_HEADS, 2, HEAD_DIM))
        outs[1].append(vp.reshape(n_p, t_p, N_HEADS, V_DIM))
        outs[2].append(up.reshape(n_p, t_p, conv_ch)[:, t_p - hist_rows:])
        outs[3].append(ks.reshape(n_s, t_s, N_HEADS, 2, HEAD_DIM))
        outs[4].append(vs.reshape(n_s, t_s, N_HEADS, V_DIM))
        outs[5].append(us.reshape(n_s, t_s, conv_ch)[:, t_s - hist_rows:])

    return (yp.reshape(n_p, t_p, d_model), ys.reshape(n_s, t_s, d_model),
            *[jnp.stack(o, axis=0) for o in outs])
```

```python
import functools
import math

import jax
import jax.numpy as jnp
from jax import lax
from jax.experimental import pallas as pl
from jax.experimental.pallas import tpu as pltpu

N_HEADS = 4
HEAD_DIM = 64
V_DIM = 2 * HEAD_DIM
Q_WIDTH = N_HEADS * 2 * HEAD_DIM
ATTN_WIDTH = N_HEADS * V_DIM
CHUNK = 64
CONV_WIDTH = 31
EPS = 1e-6
NEG_INF = -1e30
ALIBI_SLOPES = tuple(2.0 ** (-8.0 * (h + 1) / N_HEADS) for h in range(N_HEADS))
LOG2E = math.log2(math.e)

V7X_LANES = 128
V7X_VMEM_BYTES = 64 * 1024 * 1024
VMEM_LIMIT_BYTES = V7X_VMEM_BYTES - 8 * 1024 * 1024

HALO_ROWS = 32

F32 = jnp.float32
BF16 = jnp.bfloat16


def _dot(a, b):
    return jnp.dot(a, b, preferred_element_type=F32)


def _dot_nt(a, b):
    return lax.dot_general(a, b, (((1,), (1,)), ((), ())), preferred_element_type=F32)


def _rms_rows(x, g):
    return x * lax.rsqrt(jnp.mean(x * x, axis=-1, keepdims=True) + EPS) * g


def _params(n_grid_axes):
    return pltpu.CompilerParams(
        dimension_semantics=("arbitrary",) * n_grid_axes,
        vmem_limit_bytes=VMEM_LIMIT_BYTES,
    )


def _const_spec(shape):
    nd = len(shape)
    return pl.BlockSpec(shape, lambda *_: (0,) * nd, pipeline_mode=pl.Buffered(1))


def _row_tile(n_rows, want):
    t = min(want, n_rows)
    assert n_rows % t == 0, (n_rows, t)
    return t


def _ffn_kernel(*refs, has_mix, has_final):
    it = iter(refs)
    x_ref = next(it)
    if has_mix:
        attn_ref, conv_ref, woa_ref, wob_ref = next(it), next(it), next(it), next(it)
    g_ref, wg_ref, wu_ref, wd_ref = next(it), next(it), next(it), next(it)
    gf_ref = next(it) if has_final else None
    out_ref = next(it)

    x = x_ref[...]
    if has_mix:
        x = x + _dot(attn_ref[...], woa_ref[...]) + _dot(conv_ref[...], wob_ref[...])
    h = _rms_rows(x, g_ref[...]).astype(BF16)
    a = _dot(h, wg_ref[...])
    b = _dot(h, wu_ref[...])
    act = (a * jax.nn.sigmoid(a) * b).astype(BF16)
    y = x + 0.5 * _dot(act, wd_ref[...])
    if has_final:
        y = _rms_rows(y, gf_ref[...])
    out_ref[...] = y


def _ffn(x, g, wg, wu, wd, mix=None, g_final=None, name="ffn"):
    T, D = x.shape
    F = wg.shape[1]
    tm = _row_tile(T, 256)
    row = lambda w: pl.BlockSpec((tm, w), lambda i: (i, 0))
    args, specs = [x], [row(D)]
    if mix is not None:
        attn, conv, woa, wob = mix
        args += [attn, conv, woa, wob]
        specs += [row(attn.shape[1]), row(conv.shape[1]), _const_spec(woa.shape), _const_spec(wob.shape)]
    args += [g, wg, wu, wd]
    specs += [_const_spec((1, D)), _const_spec((D, F)), _const_spec((D, F)), _const_spec((F, D))]
    if g_final is not None:
        args.append(g_final)
        specs.append(_const_spec((1, D)))
    return pl.pallas_call(
        functools.partial(_ffn_kernel, has_mix=mix is not None, has_final=g_final is not None),
        grid=(T // tm,),
        in_specs=specs,
        out_specs=row(D),
        out_shape=jax.ShapeDtypeStruct((T, D), F32),
        compiler_params=_params(1),
        name=name,
    )(*args)


def _group_rms(y, g, ones_bd):
    y2 = y * y
    hi = y2.astype(BF16)
    lo = (y2 - hi.astype(F32)).astype(BF16)
    ss = _dot(hi, ones_bd) + _dot(lo, ones_bd)
    return y * lax.rsqrt(ss * (1.0 / HEAD_DIM) + EPS) * g


def _proj_kernel(x_ref, g_ref, w_ref, gq_ref, gk_ref, ones_ref,
                 qs_ref, kb_ref, vt_ref, k32_ref, v32_ref, u_ref):
    h = _rms_rows(x_ref[...], g_ref[...]).astype(BF16)
    z = _dot(h, w_ref[...])
    ones_bd = ones_ref[...]
    qn = _group_rms(z[:, :Q_WIDTH], gq_ref[...], ones_bd) * (HEAD_DIM ** -0.5 * LOG2E)
    kn = _group_rms(z[:, Q_WIDTH:2 * Q_WIDTH], gk_ref[...], ones_bd)
    v = z[:, 2 * Q_WIDTH:2 * Q_WIDTH + ATTN_WIDTH]
    conv_ch = (z.shape[1] - 2 * Q_WIDTH - ATTN_WIDTH) // 2
    a = z[:, 2 * Q_WIDTH + ATTN_WIDTH:2 * Q_WIDTH + ATTN_WIDTH + conv_ch]
    gate = z[:, 2 * Q_WIDTH + ATTN_WIDTH + conv_ch:]

    lane = lax.broadcasted_iota(jnp.int32, (x_ref.shape[0], V_DIM), 1)
    for hh in range(N_HEADS):
        sl = slice(V_DIM * hh, V_DIM * (hh + 1))
        qh = qn[:, sl]
        qs_ref[hh, 0] = jnp.where(lane < HEAD_DIM, qh, 0.0).astype(BF16)
        qs_ref[hh, 1] = jnp.where(lane >= HEAD_DIM, qh, 0.0).astype(BF16)
        kb_ref[hh] = kn[:, sl].astype(BF16)
        vt_ref[hh] = v[:, sl].T.astype(BF16)
    k32_ref[...] = kn
    v32_ref[...] = v
    u_ref[...] = a * jax.nn.sigmoid(gate)


def _project(x1, g_mix, w_in, gq_t, gk_t, ones_bd):
    T, D = x1.shape
    C = w_in.shape[1]
    conv_ch = (C - 2 * Q_WIDTH - ATTN_WIDTH) // 2
    tm = _row_tile(T, 256)
    row = lambda w: pl.BlockSpec((tm, w), lambda i: (i, 0))
    return pl.pallas_call(
        _proj_kernel,
        grid=(T // tm,),
        in_specs=[row(D), _const_spec((1, D)), _const_spec((D, C)), _const_spec((1, Q_WIDTH)),
                  _const_spec((1, Q_WIDTH)), _const_spec((Q_WIDTH, Q_WIDTH))],
        out_specs=[
            pl.BlockSpec((N_HEADS, 2, tm, V_DIM), lambda i: (0, 0, i, 0)),
            pl.BlockSpec((N_HEADS, tm, V_DIM), lambda i: (0, i, 0)),
            pl.BlockSpec((N_HEADS, V_DIM, tm), lambda i: (0, 0, i)),
            row(Q_WIDTH), row(ATTN_WIDTH), row(conv_ch),
        ],
        out_shape=[
            jax.ShapeDtypeStruct((N_HEADS, 2, T, V_DIM), BF16),
            jax.ShapeDtypeStruct((N_HEADS, T, V_DIM), BF16),
            jax.ShapeDtypeStruct((N_HEADS, V_DIM, T), BF16),
            jax.ShapeDtypeStruct((T, Q_WIDTH), F32),
            jax.ShapeDtypeStruct((T, ATTN_WIDTH), F32),
            jax.ShapeDtypeStruct((T, conv_ch), F32),
        ],
        compiler_params=_params(1),
        name="proj",
    )(x1, g_mix, w_in, gq_t, gk_t, ones_bd)


def _diff_lambda(lam_ref, lam_init):
    lp = lam_ref[...]
    s1 = jnp.sum(lp[0:1] * lp[1:2], axis=-1, keepdims=True)
    s2 = jnp.sum(lp[2:3] * lp[3:4], axis=-1, keepdims=True)
    return jnp.exp(s1) - jnp.exp(s2) + lam_init


def _softmax_step(s, v, m_ref, l_ref, acc_ref):
    m_prev = m_ref[...]
    m_new = jnp.maximum(m_prev, jnp.max(s, axis=1, keepdims=True))
    alpha = jnp.exp2(m_prev - m_new)
    p = jnp.exp2(s - m_new)
    l_ref[...] = alpha * l_ref[...] + jnp.sum(p, axis=1, keepdims=True)
    acc_ref[...] = alpha * acc_ref[...] + _dot(p.astype(BF16), v)
    m_ref[...] = m_new


def _head_out(o1, o2, lam, g_sub, lam_init):
    d = o1 - lam * o2
    return _rms_rows(d, g_sub) * (1.0 - lam_init)


def _prompt_attn_kernel(qs_ref, kb_ref, vt_ref, lam_ref, gsub_ref, out_ref,
                        acc_ref, m_ref, l_ref, bias_ref, sa_ref, sb_ref, *, tq, lam_init):
    hh = pl.program_id(0)
    qi = pl.program_id(1)
    cols = 2 * tq
    slope2 = jnp.float32(0.0)
    for h_static, sl in enumerate(ALIBI_SLOPES):
        slope2 = jnp.where(hh == h_static, jnp.float32(sl * LOG2E), slope2)

    acc_ref[...] = jnp.zeros(acc_ref.shape, F32)
    m_ref[...] = jnp.full(m_ref.shape, NEG_INF, F32)
    l_ref[...] = jnp.zeros(l_ref.shape, F32)

    ko = lax.broadcasted_iota(jnp.int32, (tq, cols), 0)
    bias_ref[...] = slope2 * ko.astype(F32)

    def tile_start(j):
        return pl.multiple_of(j * tq, tq)

    def scores(j, dst_ref):
        q = qs_ref[...].reshape(cols, V_DIM)
        dst_ref[...] = _dot_nt(kb_ref[pl.ds(tile_start(j), tq), :], q)

    def update(r, t_off, j):
        vt = vt_ref[:, pl.ds(tile_start(j), tq)]
        m = m_ref[...]
        m_new = jnp.maximum(m, jnp.max(r, axis=0, keepdims=True) + t_off)
        alpha = jnp.exp2(m - m_new)
        p = jnp.exp2(r - (m_new - t_off))
        l_ref[...] = alpha * l_ref[...] + jnp.sum(p, axis=0, keepdims=True)
        acc_ref[...] = alpha * acc_ref[...] + _dot(vt, p.astype(BF16))
        m_ref[...] = m_new

    def off_diag(j, src_ref):
        update(src_ref[...] + bias_ref[...], slope2 * ((j - qi) * tq).astype(F32), j)

    def diag(src_ref):
        c = lax.broadcasted_iota(jnp.int32, (tq, cols), 1)
        qo = jnp.where(c >= tq, c - tq, c)
        bias = slope2 * (qo - jnp.abs(qo - ko)).astype(F32)
        visible = (ko // CHUNK) <= (qo // CHUNK)
        update(jnp.where(visible, src_ref[...] + bias, NEG_INF), jnp.float32(0.0), qi)

    scores(0, sa_ref)

    def pair(jj, carry):
        t0 = 2 * jj
        scores(t0 + 1, sb_ref)
        off_diag(t0, sa_ref)
        scores(t0 + 2, sa_ref)
        off_diag(t0 + 1, sb_ref)
        return carry

    lax.fori_loop(0, qi // 2, pair, 0)

    @pl.when(qi % 2 == 0)
    def _():
        diag(sa_ref)

    @pl.when(qi % 2 == 1)
    def _():
        scores(qi, sb_ref)
        off_diag(qi - 1, sa_ref)
        diag(sb_ref)

    ot = acc_ref[...] * (1.0 / l_ref[...])
    lam = _diff_lambda(lam_ref, lam_init)
    dt = ot[:, :tq] - lam * ot[:, tq:]
    yt = dt * lax.rsqrt(jnp.mean(dt * dt, axis=0, keepdims=True) + EPS)
    out_ref[...] = (yt.T * gsub_ref[...] * (1.0 - lam_init)).astype(out_ref.dtype)


def _prompt_attention(qs, kb, vt, lam_rows, g_sub, lam_init):
    T = kb.shape[1]
    tq = _row_tile(T, 256)
    return pl.pallas_call(
        functools.partial(_prompt_attn_kernel, tq=tq, lam_init=lam_init),
        grid=(N_HEADS, T // tq),
        in_specs=[
            pl.BlockSpec((None, 2, tq, V_DIM), lambda h, i: (h, 0, i, 0)),
            pl.BlockSpec((None, T, V_DIM), lambda h, i: (h, 0, 0)),
            pl.BlockSpec((None, V_DIM, T), lambda h, i: (h, 0, 0)),
            pl.BlockSpec((4, HEAD_DIM), lambda h, i: (0, 0)),
            pl.BlockSpec((1, V_DIM), lambda h, i: (0, 0)),
        ],
        out_specs=pl.BlockSpec((tq, V_DIM), lambda h, i: (i, h)),
        out_shape=jax.ShapeDtypeStruct((T, ATTN_WIDTH), BF16),
        scratch_shapes=[
            pltpu.VMEM((V_DIM, 2 * tq), F32),
            pltpu.VMEM((1, 2 * tq), F32),
            pltpu.VMEM((1, 2 * tq), F32),
            pltpu.VMEM((tq, 2 * tq), F32),
            pltpu.VMEM((tq, 2 * tq), F32),
            pltpu.VMEM((tq, 2 * tq), F32),
        ],
        compiler_params=_params(2),
        name="prompt_attn",
    )(qs, kb, vt, lam_rows, g_sub)


def _sample_attn_kernel(qs_ref, kn_ref, vn_ref, ck_ref, cv_ref, lam_ref, gsub_ref, out_ref,
                        m_ref, l_ref, acc_ref, *, t_new, past, tk, lam_init):
    grp = 2 * t_new
    rows = N_HEADS * grp
    blocks = []
    for hh in range(N_HEADS):
        qh = qs_ref[hh].reshape(grp, V_DIM)
        z = jnp.zeros((grp, V_DIM), BF16)
        blocks.append(jnp.concatenate([qh if c == hh else z for c in range(N_HEADS)], axis=1))
    q = jnp.concatenate(blocks, axis=0)

    r = lax.broadcasted_iota(jnp.int32, (rows, 1), 0)
    q_pos = past + (r % t_new)
    head = r // grp
    slope2 = jnp.zeros((rows, 1), F32)
    for h_static, sl in enumerate(ALIBI_SLOPES):
        slope2 = jnp.where(head == h_static, jnp.float32(sl * LOG2E), slope2)

    m_ref[...] = jnp.full(m_ref.shape, NEG_INF, F32)
    l_ref[...] = jnp.zeros(l_ref.shape, F32)
    acc_ref[...] = jnp.zeros(acc_ref.shape, F32)

    def step(k, v, k_start, n_keys):
        k_pos = k_start + lax.broadcasted_iota(jnp.int32, (1, n_keys), 1)
        bias = -slope2 * jnp.abs(q_pos - k_pos).astype(F32)
        visible = (k_pos // CHUNK) <= (q_pos // CHUNK)
        s = jnp.where(visible, _dot_nt(q, k) + bias, NEG_INF)
        _softmax_step(s, v, m_ref, l_ref, acc_ref)

    for c in range(past // tk):
        step(ck_ref[c * tk:(c + 1) * tk, :].astype(BF16), cv_ref[c * tk:(c + 1) * tk, :].astype(BF16),
             c * tk, tk)
    step(kn_ref[...].astype(BF16), vn_ref[...].astype(BF16), past, t_new)

    o = acc_ref[...] / l_ref[...]
    lam = _diff_lambda(lam_ref, lam_init)
    outs = []
    for hh in range(N_HEADS):
        sl = slice(V_DIM * hh, V_DIM * (hh + 1))
        o1 = o[hh * grp:hh * grp + t_new, sl]
        o2 = o[hh * grp + t_new:(hh + 1) * grp, sl]
        outs.append(_head_out(o1, o2, lam, gsub_ref[...], lam_init))
    out_ref[...] = jnp.concatenate(outs, axis=1).astype(out_ref.dtype)


def _sample_attention(qs, k_new, v_new, cache_k, cache_v, lam_rows, g_sub, lam_init, n_streams):
    t_new = k_new.shape[0] // n_streams
    past = cache_k.shape[1]
    tk = _row_tile(past, 1024)
    rows = N_HEADS * 2 * t_new
    return pl.pallas_call(
        functools.partial(_sample_attn_kernel, t_new=t_new, past=past, tk=tk, lam_init=lam_init),
        grid=(n_streams,),
        in_specs=[
            pl.BlockSpec((N_HEADS, 2, t_new, V_DIM), lambda b: (0, 0, b, 0)),
            pl.BlockSpec((t_new, Q_WIDTH), lambda b: (b, 0)),
            pl.BlockSpec((t_new, ATTN_WIDTH), lambda b: (b, 0)),
            pl.BlockSpec((None, past, Q_WIDTH), lambda b: (b, 0, 0)),
            pl.BlockSpec((None, past, ATTN_WIDTH), lambda b: (b, 0, 0)),
            pl.BlockSpec((4, HEAD_DIM), lambda b: (0, 0)),
            pl.BlockSpec((1, V_DIM), lambda b: (0, 0)),
        ],
        out_specs=pl.BlockSpec((t_new, ATTN_WIDTH), lambda b: (b, 0)),
        out_shape=jax.ShapeDtypeStruct((n_streams * t_new, ATTN_WIDTH), BF16),
        scratch_shapes=[
            pltpu.VMEM((rows, 1), F32),
            pltpu.VMEM((rows, 1), F32),
            pltpu.VMEM((rows, ATTN_WIDTH), F32),
        ],
        compiler_params=_params(1),
        name="sample_attn",
    )(qs, k_new, v_new, cache_k, cache_v, lam_rows, g_sub)


def _conv_kernel(u_ref, halo_ref, init_ref, w_ref, b_ref, g_ref, beta_ref, out_ref, xp_ref, *, tm):
    i = pl.program_id(1)
    hist = jnp.where(i == 0, init_ref[...], halo_ref[...])
    xp_ref[0:HALO_ROWS, :] = hist
    xp_ref[HALO_ROWS:HALO_ROWS + tm, :] = u_ref[...]
    first = HALO_ROWS - (CONV_WIDTH - 1)
    y = jnp.zeros((tm, u_ref.shape[-1]), F32) + b_ref[...]
    for w in range(CONV_WIDTH):
        y = y + xp_ref[first + w:first + w + tm, :] * w_ref[w:w + 1, :]
    mu = jnp.mean(y, axis=-1, keepdims=True)
    d = y - mu
    var = jnp.mean(d * d, axis=-1, keepdims=True)
    yn = d * lax.rsqrt(var + EPS) * g_ref[...] + beta_ref[...]
    out_ref[...] = (yn * jax.nn.sigmoid(yn)).astype(out_ref.dtype)


def _conv_branch(u, init_hist, w_dw, b_dw, g_ln, b_ln):
    B, T, C = u.shape
    tm = _row_tile(T, 256)
    assert tm % HALO_ROWS == 0
    per = tm // HALO_ROWS
    vec = lambda: pl.BlockSpec((1, C), lambda b, i: (0, 0))
    return pl.pallas_call(
        functools.partial(_conv_kernel, tm=tm),
        grid=(B, T // tm),
        in_specs=[
            pl.BlockSpec((None, tm, C), lambda b, i: (b, i, 0)),
            pl.BlockSpec((None, HALO_ROWS, C), lambda b, i: (b, jnp.maximum(i * per - 1, 0), 0)),
            pl.BlockSpec((None, HALO_ROWS, C), lambda b, i: (b, 0, 0)),
            pl.BlockSpec((CONV_WIDTH, C), lambda b, i: (0, 0)),
            vec(), vec(), vec(),
        ],
        out_specs=pl.BlockSpec((None, tm, C), lambda b, i: (b, i, 0)),
        out_shape=jax.ShapeDtypeStruct((B, T, C), BF16),
        scratch_shapes=[pltpu.VMEM((HALO_ROWS + tm, C), F32)],
        compiler_params=_params(2),
        name="conv",
    )(u, u, init_hist, w_dw, b_dw, g_ln, b_ln)


def _layer(x, n_streams, attend, conv_hist, lam_init, p):
    T = x.shape[0] // n_streams
    x1 = _ffn(x, p["g_ffn1"], p["w1g"], p["w1u"], p["w1d"], name="ffn1")
    qs, kb, vt, k32, v32, u = _project(x1, p["g_mix"], p["w_in"], p["gq_t"], p["gk_t"], p["ones_bd"])
    attn = attend(qs, kb, vt, k32, v32)
    conv_ch = u.shape[1]
    hist = jnp.pad(conv_hist, ((0, 0), (HALO_ROWS - conv_hist.shape[1], 0), (0, 0)))
    conv = _conv_branch(u.reshape(n_streams, T, conv_ch), hist, p["w_dw"], p["b_dw"], p["g_ln"], p["b_ln"])
    y = _ffn(x1, p["g_ffn2"], p["w2g"], p["w2u"], p["w2d"],
             mix=(attn, conv.reshape(n_streams * T, conv_ch), p["wo_attn"], p["wo_conv"]),
             g_final=p["g_final"], name="ffn2")
    return y, k32, v32, u


def kernel(x_prompt, x_sample, cache_k, cache_v, cache_conv, g_ffn1, w_ffn1_gu, w_ffn1_down, g_mix, w_in, g_q, g_k, lambda_q1, lambda_k1, lambda_q2, lambda_k2, g_sub, w_dw, b_dw, g_conv_ln, b_conv_ln, w_out, g_ffn2, w_ffn2_gu, w_ffn2_down, g_final):
    depth = cache_k.shape[0]
    n_p, t_p, d_model = x_prompt.shape
    n_s, t_s, _ = x_sample.shape
    past = cache_k.shape[2]
    d_ff = w_ffn1_down.shape[1]
    conv_ch = cache_conv.shape[-1]
    hist_rows = CONV_WIDTH - 1
    assert n_p == 1, "prompt attention kernel handles one prompt stream"
    assert t_p >= hist_rows and t_s >= hist_rows

    group = jnp.arange(Q_WIDTH) // HEAD_DIM
    ones_bd = (group[:, None] == group[None, :]).astype(BF16)
    row = lambda a: a.reshape(1, -1).astype(F32)

    yp = x_prompt.reshape(n_p * t_p, d_model)
    ys = x_sample.reshape(n_s * t_s, d_model)
    outs = [[] for _ in range(6)]
    for l in range(depth):
        lam_init = 0.8 - 0.6 * math.exp(-0.3 * l)
        lam_rows = jnp.stack([lambda_q1[l], lambda_k1[l], lambda_q2[l], lambda_k2[l]]).astype(F32)
        p = dict(
            g_ffn1=row(g_ffn1[l]), w1g=w_ffn1_gu[l][:, :d_ff].astype(BF16), w1u=w_ffn1_gu[l][:, d_ff:].astype(BF16),
            w1d=w_ffn1_down[l].astype(BF16), g_mix=row(g_mix[l]), w_in=w_in[l].astype(BF16),
            gq_t=row(jnp.tile(g_q[l], 2 * N_HEADS)), gk_t=row(jnp.tile(g_k[l], 2 * N_HEADS)), ones_bd=ones_bd,
            w_dw=w_dw[l].astype(F32), b_dw=row(b_dw[l]), g_ln=row(g_conv_ln[l]), b_ln=row(b_conv_ln[l]),
            wo_attn=w_out[l][:ATTN_WIDTH].astype(BF16), wo_conv=w_out[l][ATTN_WIDTH:].astype(BF16),
            g_ffn2=row(g_ffn2[l]), w2g=w_ffn2_gu[l][:, :d_ff].astype(BF16), w2u=w_ffn2_gu[l][:, d_ff:].astype(BF16),
            w2d=w_ffn2_down[l].astype(BF16), g_final=row(g_final[l]),
        )
        g_sub_row = row(g_sub[l])

        def attend_prompt(qs, kb, vt, k32, v32):
            return _prompt_attention(qs, kb, vt, lam_rows, g_sub_row, lam_init)

        def attend_sample(qs, kb, vt, k32, v32, l=l):
            ck = cache_k[l].reshape(n_s, past, Q_WIDTH)
            cv = cache_v[l].reshape(n_s, past, ATTN_WIDTH)
            return _sample_attention(qs, k32, v32, ck, cv, lam_rows, g_sub_row, lam_init, n_s)

        zero_hist = jnp.zeros((n_p, hist_rows, conv_ch), F32)
        yp, kp, vp, up = _layer(yp, n_p, attend_prompt, zero_hist, lam_init, p)
        ys, ks, vs, us = _layer(ys, n_s, attend_sample, cache_conv[l], lam_init, p)
        outs[0].append(kp.reshape(n_p, t_p, N_HEADS, 2, HEAD_DIM))
        outs[1].append(vp.reshape(n_p, t_p, N_HEADS, V_DIM))
        outs[2].append(up.reshape(n_p, t_p, conv_ch)[:, t_p - hist_rows:])
        outs[3].append(ks.reshape(n_s, t_s, N_HEADS, 2, HEAD_DIM))
        outs[4].append(vs.reshape(n_s, t_s, N_HEADS, V_DIM))
        outs[5].append(us.reshape(n_s, t_s, conv_ch)[:, t_s - hist_rows:])

    return (yp.reshape(n_p, t_p, d_model), ys.reshape(n_s, t_s, d_model),
            *[jnp.stack(o, axis=0) for o in outs])
```

```python
import functools
import math

import jax
import jax.numpy as jnp
from jax import lax
from jax.experimental import pallas as pl
from jax.experimental.pallas import tpu as pltpu

N_HEADS = 4
HEAD_DIM = 64
V_DIM = 2 * HEAD_DIM
Q_WIDTH = N_HEADS * 2 * HEAD_DIM
ATTN_WIDTH = N_HEADS * V_DIM
CHUNK = 64
CONV_WIDTH = 31
EPS = 1e-6
NEG_INF = -1e30
ALIBI_SLOPES = tuple(2.0 ** (-8.0 * (h + 1) / N_HEADS) for h in range(N_HEADS))
LOG2E = math.log2(math.e)

V7X_LANES = 128
V7X_VMEM_BYTES = 64 * 1024 * 1024
VMEM_LIMIT_BYTES = V7X_VMEM_BYTES - 8 * 1024 * 1024

QUERY_BLOCK = 256
HALO_ROWS = 32

F32 = jnp.float32
BF16 = jnp.bfloat16


def _dot(a, b):
    return jnp.dot(a, b, preferred_element_type=F32)


def _dot_nt(a, b):
    return lax.dot_general(a, b, (((1,), (1,)), ((), ())), preferred_element_type=F32)


def _rms_rows(x, g):
    return x * lax.rsqrt(jnp.mean(x * x, axis=-1, keepdims=True) + EPS) * g


def _params(n_grid_axes, flags=None):
    return pltpu.CompilerParams(
        dimension_semantics=("arbitrary",) * n_grid_axes,
        vmem_limit_bytes=VMEM_LIMIT_BYTES,
        flags=flags,
    )


def _const_spec(shape):
    nd = len(shape)
    return pl.BlockSpec(shape, lambda *_: (0,) * nd, pipeline_mode=pl.Buffered(1))


def _row_tile(n_rows, want):
    t = min(want, n_rows)
    assert n_rows % t == 0, (n_rows, t)
    return t


def _ffn_kernel(*refs, has_mix, has_final):
    it = iter(refs)
    x_ref = next(it)
    if has_mix:
        attn_ref, conv_ref, woa_ref, wob_ref = next(it), next(it), next(it), next(it)
    g_ref, wg_ref, wu_ref, wd_ref = next(it), next(it), next(it), next(it)
    gf_ref = next(it) if has_final else None
    out_ref = next(it)

    x = x_ref[...]
    if has_mix:
        x = x + _dot(attn_ref[...], woa_ref[...]) + _dot(conv_ref[...], wob_ref[...])
    h = _rms_rows(x, g_ref[...]).astype(BF16)
    a = _dot(h, wg_ref[...])
    b = _dot(h, wu_ref[...])
    act = (a * jax.nn.sigmoid(a) * b).astype(BF16)
    y = x + 0.5 * _dot(act, wd_ref[...])
    if has_final:
        y = _rms_rows(y, gf_ref[...])
    out_ref[...] = y


def _ffn(x, g, wg, wu, wd, mix=None, g_final=None, name="ffn"):
    T, D = x.shape
    F = wg.shape[1]
    tm = _row_tile(T, 256)
    row = lambda w: pl.BlockSpec((tm, w), lambda i: (i, 0))
    args, specs = [x], [row(D)]
    if mix is not None:
        attn, conv, woa, wob = mix
        args += [attn, conv, woa, wob]
        specs += [row(attn.shape[1]), row(conv.shape[1]), _const_spec(woa.shape), _const_spec(wob.shape)]
    args += [g, wg, wu, wd]
    specs += [_const_spec((1, D)), _const_spec((D, F)), _const_spec((D, F)), _const_spec((F, D))]
    if g_final is not None:
        args.append(g_final)
        specs.append(_const_spec((1, D)))
    return pl.pallas_call(
        functools.partial(_ffn_kernel, has_mix=mix is not None, has_final=g_final is not None),
        grid=(T // tm,),
        in_specs=specs,
        out_specs=row(D),
        out_shape=jax.ShapeDtypeStruct((T, D), F32),
        compiler_params=_params(1),
        name=name,
    )(*args)


def _group_rms(y, g, ones_bd):
    y2 = y * y
    hi = y2.astype(BF16)
    lo = (y2 - hi.astype(F32)).astype(BF16)
    ss = _dot(hi, ones_bd) + _dot(lo, ones_bd)
    return y * lax.rsqrt(ss * (1.0 / HEAD_DIM) + EPS) * g


def _alibi_query_lanes():
    c = jnp.asarray([s * LOG2E for s in ALIBI_SLOPES], F32)
    c1 = c.astype(BF16).astype(F32)
    c2 = (c - c1).astype(BF16).astype(F32)
    c3 = (c - c1 - c2).astype(BF16).astype(F32)
    pieces = jnp.stack([c1, c2, c3, c1, c2, c3], axis=1)
    return jnp.zeros((N_HEADS, V_DIM), F32).at[:, HEAD_DIM:HEAD_DIM + 6].set(pieces)


def _proj_kernel(*refs, key_tile):
    if key_tile is None:
        x_ref, g_ref, w_ref, gq_ref, gk_ref, ones_ref, qs_ref, k32_ref, v32_ref, u_ref = refs
    else:
        (x_ref, g_ref, w_ref, gq_ref, gk_ref, ones_ref, qext_ref,
         qs_ref, ks_ref, vt_ref, k32_ref, v32_ref, u_ref) = refs
    tm = x_ref.shape[0]
    h = _rms_rows(x_ref[...], g_ref[...]).astype(BF16)
    z = _dot(h, w_ref[...])
    ones_bd = ones_ref[...]
    qn = _group_rms(z[:, :Q_WIDTH], gq_ref[...], ones_bd) * (HEAD_DIM ** -0.5 * LOG2E)
    kn = _group_rms(z[:, Q_WIDTH:2 * Q_WIDTH], gk_ref[...], ones_bd)
    v = z[:, 2 * Q_WIDTH:2 * Q_WIDTH + ATTN_WIDTH]
    conv_ch = (z.shape[1] - 2 * Q_WIDTH - ATTN_WIDTH) // 2
    a = z[:, 2 * Q_WIDTH + ATTN_WIDTH:2 * Q_WIDTH + ATTN_WIDTH + conv_ch]
    gate = z[:, 2 * Q_WIDTH + ATTN_WIDTH + conv_ch:]
    k32_ref[...] = kn
    v32_ref[...] = v
    u_ref[...] = a * jax.nn.sigmoid(gate)

    lane = lax.broadcasted_iota(jnp.int32, (tm, V_DIM), 1)
    low = lane < HEAD_DIM
    if key_tile is None:
        for hh in range(N_HEADS):
            qh = qn[:, V_DIM * hh:V_DIM * (hh + 1)]
            qs_ref[hh, 0] = jnp.where(low, qh, 0.0).astype(BF16)
            qs_ref[hh, 1] = jnp.where(low, 0.0, qh).astype(BF16)
        return

    row = pl.program_id(0) * tm + lax.broadcasted_iota(jnp.int32, (tm, V_DIM), 0)
    off = row % key_tile
    lo = off % 32
    in_hi = (lane >= HEAD_DIM) & (lane < HEAD_DIM + 3)
    in_lo = (lane >= HEAD_DIM + 3) & (lane < HEAD_DIM + 6)
    k_ext = jnp.where(in_hi, off - lo, jnp.where(in_lo, lo, 0)).astype(F32)
    for hh in range(N_HEADS):
        sl = slice(V_DIM * hh, V_DIM * (hh + 1))
        q_ext = qext_ref[hh:hh + 1, :]
        for x, ext, dst in ((qn[:, sl], q_ext, qs_ref), (kn[:, sl], k_ext, ks_ref)):
            dst[hh, 0] = jnp.where(low, x, ext).astype(BF16)
            dst[hh, 1] = jnp.where(low, pltpu.roll(x, HEAD_DIM, axis=1), ext).astype(BF16)
        vt_ref[hh] = v[:, sl].T.astype(BF16)


def _project(x1, g_mix, w_in, gq_t, gk_t, ones_bd, key_tile=None):
    T, D = x1.shape
    C = w_in.shape[1]
    conv_ch = (C - 2 * Q_WIDTH - ATTN_WIDTH) // 2
    tm = _row_tile(T, 256)
    row = lambda w: pl.BlockSpec((tm, w), lambda i: (i, 0))
    per_map = pl.BlockSpec((N_HEADS, 2, tm, V_DIM), lambda i: (0, 0, i, 0))
    per_map_shape = jax.ShapeDtypeStruct((N_HEADS, 2, T, V_DIM), BF16)
    args = [x1, g_mix, w_in, gq_t, gk_t, ones_bd]
    in_specs = [row(D), _const_spec((1, D)), _const_spec((D, C)), _const_spec((1, Q_WIDTH)),
                _const_spec((1, Q_WIDTH)), _const_spec((Q_WIDTH, Q_WIDTH))]
    out_specs, out_shape = [per_map], [per_map_shape]
    if key_tile is not None:
        args.append(_alibi_query_lanes())
        in_specs.append(_const_spec((N_HEADS, V_DIM)))
        out_specs += [per_map, pl.BlockSpec((N_HEADS, V_DIM, tm), lambda i: (0, 0, i))]
        out_shape += [per_map_shape, jax.ShapeDtypeStruct((N_HEADS, V_DIM, T), BF16)]
    out_specs += [row(Q_WIDTH), row(ATTN_WIDTH), row(conv_ch)]
    out_shape += [jax.ShapeDtypeStruct((T, w), F32) for w in (Q_WIDTH, ATTN_WIDTH, conv_ch)]
    return pl.pallas_call(
        functools.partial(_proj_kernel, key_tile=key_tile),
        grid=(T // tm,),
        in_specs=in_specs,
        out_specs=out_specs,
        out_shape=out_shape,
        compiler_params=_params(1),
        name="proj",
    )(*args)


def _diff_lambda(lam_ref, lam_init):
    lp = lam_ref[...]
    s1 = jnp.sum(lp[0:1] * lp[1:2], axis=-1, keepdims=True)
    s2 = jnp.sum(lp[2:3] * lp[3:4], axis=-1, keepdims=True)
    return jnp.exp(s1) - jnp.exp(s2) + lam_init


def _softmax_step(s, v, m_ref, l_ref, acc_ref):
    m_prev = m_ref[...]
    m_new = jnp.maximum(m_prev, jnp.max(s, axis=1, keepdims=True))
    alpha = jnp.exp2(m_prev - m_new)
    p = jnp.exp2(s - m_new)
    l_ref[...] = alpha * l_ref[...] + jnp.sum(p, axis=1, keepdims=True)
    acc_ref[...] = alpha * acc_ref[...] + _dot(p.astype(BF16), v)
    m_ref[...] = m_new


def _head_out(o1, o2, lam, g_sub, lam_init):
    d = o1 - lam * o2
    return _rms_rows(d, g_sub) * (1.0 - lam_init)


def _prompt_attn_kernel(qs_ref, ks_ref, vt_ref, lam_ref, gsub_ref, out_ref,
                        acc_ref, m_ref, l_ref, sa_ref, sb_ref, pa_ref, pb_ref, aa_ref, ab_ref,
                        *, tq, lam_init):
    hh = pl.program_id(0)
    qi = pl.program_id(1)
    cols = 2 * tq
    slope2 = jnp.float32(0.0)
    for h_static, sl in enumerate(ALIBI_SLOPES):
        slope2 = jnp.where(hh == h_static, jnp.float32(sl * LOG2E), slope2)

    acc_ref[...] = jnp.zeros(acc_ref.shape, F32)
    m_ref[...] = jnp.full(m_ref.shape, NEG_INF, F32)
    l_ref[...] = jnp.zeros(l_ref.shape, F32)

    def tile_start(j):
        return pl.multiple_of(j * tq, tq)

    n_blocks = cols // QUERY_BLOCK

    def lanes(c):
        return slice(c * QUERY_BLOCK, (c + 1) * QUERY_BLOCK)

    def scores(j, s_ref, c):
        which, first = divmod(c * QUERY_BLOCK, tq)
        q = qs_ref[which, first:first + QUERY_BLOCK, :]
        s_ref[:, lanes(c)] = _dot_nt(ks_ref[which, pl.ds(tile_start(j), tq), :], q)

    def softmax(s_ref, t_off, p_ref, a_ref, c):
        m = m_ref[:, lanes(c)]
        m_new = jnp.maximum(m, jnp.max(s_ref[:, lanes(c)], axis=0, keepdims=True) + t_off)
        alpha = jnp.exp2(m - m_new)
        p = jnp.exp2(s_ref[:, lanes(c)] - (m_new - t_off))
        l_ref[:, lanes(c)] = alpha * l_ref[:, lanes(c)] + jnp.sum(p, axis=0, keepdims=True)
        m_ref[:, lanes(c)] = m_new
        p_ref[:, lanes(c)] = p.astype(BF16)
        a_ref[:, lanes(c)] = alpha

    def softmax_off(j, s_ref, p_ref, a_ref, c):
        softmax(s_ref, slope2 * ((j - qi) * tq).astype(F32), p_ref, a_ref, c)

    def softmax_diag(s_ref, p_ref, a_ref, c):
        first = (c * QUERY_BLOCK) % tq
        ko = lax.broadcasted_iota(jnp.int32, (tq, QUERY_BLOCK), 0)
        qo = first + lax.broadcasted_iota(jnp.int32, (tq, QUERY_BLOCK), 1)
        after = (-2.0 * slope2) * jnp.maximum(ko - qo, 0).astype(F32)
        visible = (ko // CHUNK) <= (qo // CHUNK)
        s_ref[:, lanes(c)] = jnp.where(visible, s_ref[:, lanes(c)] + after, NEG_INF)
        softmax(s_ref, jnp.float32(0.0), p_ref, a_ref, c)

    def pv(j, p_ref, a_ref, c):
        vt = vt_ref[:, pl.ds(tile_start(j), tq)]
        acc_ref[:, lanes(c)] = a_ref[:, lanes(c)] * acc_ref[:, lanes(c)] + _dot(vt, p_ref[:, lanes(c)])

    for c in range(n_blocks):
        scores(0, sa_ref, c)

    even = (sa_ref, pa_ref, aa_ref)
    odd = (sb_ref, pb_ref, ab_ref)

    def step(j, cur, other):
        for c in range(n_blocks):
            scores(j + 1, other[0], c)
            softmax_off(j, *cur, c)
            pv(j, cur[1], cur[2], c)

    def body(j, carry):
        @pl.when(j % 2 == 0)
        def _():
            step(j, even, odd)

        @pl.when(j % 2 == 1)
        def _():
            step(j, odd, even)

        return carry

    lax.fori_loop(0, qi, body, 0)

    def tail(cur):
        for c in range(n_blocks):
            softmax_diag(*cur, c)
            pv(qi, cur[1], cur[2], c)

    @pl.when(qi % 2 == 0)
    def _():
        tail(even)

    @pl.when(qi % 2 == 1)
    def _():
        tail(odd)

    ot = acc_ref[...] * (1.0 / l_ref[...])
    lam = _diff_lambda(lam_ref, lam_init)
    dt = ot[:, :tq] - lam * ot[:, tq:]
    yt = dt * lax.rsqrt(jnp.mean(dt * dt, axis=0, keepdims=True) + EPS)
    out_ref[...] = (yt.T * gsub_ref[...] * (1.0 - lam_init)).astype(out_ref.dtype)


def _prompt_key_tile(T):
    return _row_tile(T, 512)


def _prompt_attention(qs, ks, vt, lam_rows, g_sub, lam_init, tq):
    T = ks.shape[2]
    assert (2 * tq) % QUERY_BLOCK == 0 and tq % QUERY_BLOCK == 0
    return pl.pallas_call(
        functools.partial(_prompt_attn_kernel, tq=tq, lam_init=lam_init),
        grid=(N_HEADS, T // tq),
        in_specs=[
            pl.BlockSpec((None, 2, tq, V_DIM), lambda h, i: (h, 0, i, 0)),
            pl.BlockSpec((None, 2, T, V_DIM), lambda h, i: (h, 0, 0, 0)),
            pl.BlockSpec((None, V_DIM, T), lambda h, i: (h, 0, 0)),
            pl.BlockSpec((4, HEAD_DIM), lambda h, i: (0, 0)),
            pl.BlockSpec((1, V_DIM), lambda h, i: (0, 0)),
        ],
        out_specs=pl.BlockSpec((tq, V_DIM), lambda h, i: (i, h)),
        out_shape=jax.ShapeDtypeStruct((T, ATTN_WIDTH), BF16),
        scratch_shapes=[
            pltpu.VMEM((V_DIM, 2 * tq), F32),
            pltpu.VMEM((1, 2 * tq), F32),
            pltpu.VMEM((1, 2 * tq), F32),
            pltpu.VMEM((tq, 2 * tq), F32),
            pltpu.VMEM((tq, 2 * tq), F32),
            pltpu.VMEM((tq, 2 * tq), BF16),
            pltpu.VMEM((tq, 2 * tq), BF16),
            pltpu.VMEM((1, 2 * tq), F32),
            pltpu.VMEM((1, 2 * tq), F32),
        ],
        compiler_params=_params(2),
        name="prompt_attn",
    )(qs, ks, vt, lam_rows, g_sub)


def _sample_attn_kernel(qs_ref, kn_ref, vn_ref, ck_ref, cv_ref, lam_ref, gsub_ref, out_ref,
                        m_ref, l_ref, acc_ref, *, t_new, past, tk, lam_init):
    grp = 2 * t_new
    rows = N_HEADS * grp
    blocks = []
    for hh in range(N_HEADS):
        qh = qs_ref[hh].reshape(grp, V_DIM)
        z = jnp.zeros((grp, V_DIM), BF16)
        blocks.append(jnp.concatenate([qh if c == hh else z for c in range(N_HEADS)], axis=1))
    q = jnp.concatenate(blocks, axis=0)

    r = lax.broadcasted_iota(jnp.int32, (rows, 1), 0)
    q_pos = past + (r % t_new)
    head = r // grp
    slope2 = jnp.zeros((rows, 1), F32)
    for h_static, sl in enumerate(ALIBI_SLOPES):
        slope2 = jnp.where(head == h_static, jnp.float32(sl * LOG2E), slope2)

    m_ref[...] = jnp.full(m_ref.shape, NEG_INF, F32)
    l_ref[...] = jnp.zeros(l_ref.shape, F32)
    acc_ref[...] = jnp.zeros(acc_ref.shape, F32)

    def step(k, v, k_start, n_keys):
        k_pos = k_start + lax.broadcasted_iota(jnp.int32, (1, n_keys), 1)
        bias = -slope2 * jnp.abs(q_pos - k_pos).astype(F32)
        visible = (k_pos // CHUNK) <= (q_pos // CHUNK)
        s = jnp.where(visible, _dot_nt(q, k) + bias, NEG_INF)
        _softmax_step(s, v, m_ref, l_ref, acc_ref)

    for c in range(past // tk):
        step(ck_ref[c * tk:(c + 1) * tk, :].astype(BF16), cv_ref[c * tk:(c + 1) * tk, :].astype(BF16),
             c * tk, tk)
    step(kn_ref[...].astype(BF16), vn_ref[...].astype(BF16), past, t_new)

    o = acc_ref[...] / l_ref[...]
    lam = _diff_lambda(lam_ref, lam_init)
    outs = []
    for hh in range(N_HEADS):
        sl = slice(V_DIM * hh, V_DIM * (hh + 1))
        o1 = o[hh * grp:hh * grp + t_new, sl]
        o2 = o[hh * grp + t_new:(hh + 1) * grp, sl]
        outs.append(_head_out(o1, o2, lam, gsub_ref[...], lam_init))
    out_ref[...] = jnp.concatenate(outs, axis=1).astype(out_ref.dtype)


def _sample_attention(qs, k_new, v_new, cache_k, cache_v, lam_rows, g_sub, lam_init, n_streams):
    t_new = k_new.shape[0] // n_streams
    past = cache_k.shape[1]
    tk = _row_tile(past, 1024)
    rows = N_HEADS * 2 * t_new
    return pl.pallas_call(
        functools.partial(_sample_attn_kernel, t_new=t_new, past=past, tk=tk, lam_init=lam_init),
        grid=(n_streams,),
        in_specs=[
            pl.BlockSpec((N_HEADS, 2, t_new, V_DIM), lambda b: (0, 0, b, 0)),
            pl.BlockSpec((t_new, Q_WIDTH), lambda b: (b, 0)),
            pl.BlockSpec((t_new, ATTN_WIDTH), lambda b: (b, 0)),
            pl.BlockSpec((None, past, Q_WIDTH), lambda b: (b, 0, 0)),
            pl.BlockSpec((None, past, ATTN_WIDTH), lambda b: (b, 0, 0)),
            pl.BlockSpec((4, HEAD_DIM), lambda b: (0, 0)),
            pl.BlockSpec((1, V_DIM), lambda b: (0, 0)),
        ],
        out_specs=pl.BlockSpec((t_new, ATTN_WIDTH), lambda b: (b, 0)),
        out_shape=jax.ShapeDtypeStruct((n_streams * t_new, ATTN_WIDTH), BF16),
        scratch_shapes=[
            pltpu.VMEM((rows, 1), F32),
            pltpu.VMEM((rows, 1), F32),
            pltpu.VMEM((rows, ATTN_WIDTH), F32),
        ],
        compiler_params=_params(1),
        name="sample_attn",
    )(qs, k_new, v_new, cache_k, cache_v, lam_rows, g_sub)


def _conv_kernel(u_ref, halo_ref, init_ref, w_ref, b_ref, g_ref, beta_ref, out_ref, xp_ref, *, tm):
    i = pl.program_id(1)
    hist = jnp.where(i == 0, init_ref[...], halo_ref[...])
    xp_ref[0:HALO_ROWS, :] = hist
    xp_ref[HALO_ROWS:HALO_ROWS + tm, :] = u_ref[...]
    first = HALO_ROWS - (CONV_WIDTH - 1)
    y = jnp.zeros((tm, u_ref.shape[-1]), F32) + b_ref[...]
    for w in range(CONV_WIDTH):
        y = y + xp_ref[first + w:first + w + tm, :] * w_ref[w:w + 1, :]
    mu = jnp.mean(y, axis=-1, keepdims=True)
    d = y - mu
    var = jnp.mean(d * d, axis=-1, keepdims=True)
    yn = d * lax.rsqrt(var + EPS) * g_ref[...] + beta_ref[...]
    out_ref[...] = (yn * jax.nn.sigmoid(yn)).astype(out_ref.dtype)


def _conv_branch(u, init_hist, w_dw, b_dw, g_ln, b_ln):
    B, T, C = u.shape
    tm = _row_tile(T, 256)
    assert tm % HALO_ROWS == 0
    per = tm // HALO_ROWS
    vec = lambda: pl.BlockSpec((1, C), lambda b, i: (0, 0))
    return pl.pallas_call(
        functools.partial(_conv_kernel, tm=tm),
        grid=(B, T // tm),
        in_specs=[
            pl.BlockSpec((None, tm, C), lambda b, i: (b, i, 0)),
            pl.BlockSpec((None, HALO_ROWS, C), lambda b, i: (b, jnp.maximum(i * per - 1, 0), 0)),
            pl.BlockSpec((None, HALO_ROWS, C), lambda b, i: (b, 0, 0)),
            pl.BlockSpec((CONV_WIDTH, C), lambda b, i: (0, 0)),
            vec(), vec(), vec(),
        ],
        out_specs=pl.BlockSpec((None, tm, C), lambda b, i: (b, i, 0)),
        out_shape=jax.ShapeDtypeStruct((B, T, C), BF16),
        scratch_shapes=[pltpu.VMEM((HALO_ROWS + tm, C), F32)],
        compiler_params=_params(2),
        name="conv",
    )(u, u, init_hist, w_dw, b_dw, g_ln, b_ln)


def _layer(x, n_streams, attend, key_tile, conv_hist, lam_init, p):
    T = x.shape[0] // n_streams
    x1 = _ffn(x, p["g_ffn1"], p["w1g"], p["w1u"], p["w1d"], name="ffn1")
    proj = _project(x1, p["g_mix"], p["w_in"], p["gq_t"], p["gk_t"], p["ones_bd"], key_tile)
    k32, v32, u = proj[-3:]
    attn = attend(*proj)
    conv_ch = u.shape[1]
    hist = jnp.pad(conv_hist, ((0, 0), (HALO_ROWS - conv_hist.shape[1], 0), (0, 0)))
    conv = _conv_branch(u.reshape(n_streams, T, conv_ch), hist, p["w_dw"], p["b_dw"], p["g_ln"], p["b_ln"])
    y = _ffn(x1, p["g_ffn2"], p["w2g"], p["w2u"], p["w2d"],
             mix=(attn, conv.reshape(n_streams * T, conv_ch), p["wo_attn"], p["wo_conv"]),
             g_final=p["g_final"], name="ffn2")
    return y, k32, v32, u


def kernel(x_prompt, x_sample, cache_k, cache_v, cache_conv, g_ffn1, w_ffn1_gu, w_ffn1_down, g_mix, w_in, g_q, g_k, lambda_q1, lambda_k1, lambda_q2, lambda_k2, g_sub, w_dw, b_dw, g_conv_ln, b_conv_ln, w_out, g_ffn2, w_ffn2_gu, w_ffn2_down, g_final):
    depth = cache_k.shape[0]
    n_p, t_p, d_model = x_prompt.shape
    n_s, t_s, _ = x_sample.shape
    past = cache_k.shape[2]
    d_ff = w_ffn1_down.shape[1]
    conv_ch = cache_conv.shape[-1]
    hist_rows = CONV_WIDTH - 1
    assert n_p == 1, "prompt attention kernel handles one prompt stream"
    assert t_p >= hist_rows and t_s >= hist_rows

    group = jnp.arange(Q_WIDTH) // HEAD_DIM
    ones_bd = (group[:, None] == group[None, :]).astype(BF16)
    row = lambda a: a.reshape(1, -1).astype(F32)

    yp = x_prompt.reshape(n_p * t_p, d_model)
    ys = x_sample.reshape(n_s * t_s, d_model)
    outs = [[] for _ in range(6)]
    for l in range(depth):
        lam_init = 0.8 - 0.6 * math.exp(-0.3 * l)
        lam_rows = jnp.stack([lambda_q1[l], lambda_k1[l], lambda_q2[l], lambda_k2[l]]).astype(F32)
        p = dict(
            g_ffn1=row(g_ffn1[l]), w1g=w_ffn1_gu[l][:, :d_ff].astype(BF16), w1u=w_ffn1_gu[l][:, d_ff:].astype(BF16),
            w1d=w_ffn1_down[l].astype(BF16), g_mix=row(g_mix[l]), w_in=w_in[l].astype(BF16),
            gq_t=row(jnp.tile(g_q[l], 2 * N_HEADS)), gk_t=row(jnp.tile(g_k[l], 2 * N_HEADS)), ones_bd=ones_bd,
            w_dw=w_dw[l].astype(F32), b_dw=row(b_dw[l]), g_ln=row(g_conv_ln[l]), b_ln=row(b_conv_ln[l]),
            wo_attn=w_out[l][:ATTN_WIDTH].astype(BF16), wo_conv=w_out[l][ATTN_WIDTH:].astype(BF16),
            g_ffn2=row(g_ffn2[l]), w2g=w_ffn2_gu[l][:, :d_ff].astype(BF16), w2u=w_ffn2_gu[l][:, d_ff:].astype(BF16),
            w2d=w_ffn2_down[l].astype(BF16), g_final=row(g_final[l]),
        )
        g_sub_row = row(g_sub[l])

        key_tile = _prompt_key_tile(t_p)

        def attend_prompt(qs, ks, vt, k32, v32, u):
            return _prompt_attention(qs, ks, vt, lam_rows, g_sub_row, lam_init, key_tile)

        def attend_sample(qs, k32, v32, u, l=l):
            ck = cache_k[l].reshape(n_s, past, Q_WIDTH)
            cv = cache_v[l].reshape(n_s, past, ATTN_WIDTH)
            return _sample_attention(qs, k32, v32, ck, cv, lam_rows, g_sub_row, lam_init, n_s)

        zero_hist = jnp.zeros((n_p, hist_rows, conv_ch), F32)
        yp, kp, vp, up = _layer(yp, n_p, attend_prompt, key_tile, zero_hist, lam_init, p)
        ys, ks, vs, us = _layer(ys, n_s, attend_sample, None, cache_conv[l], lam_init, p)
        outs[0].append(kp.reshape(n_p, t_p, N_HEADS, 2, HEAD_DIM))
        outs[1].append(vp.reshape(n_p, t_p, N_HEADS, V_DIM))
        outs[2].append(up.reshape(n_p, t_p, conv_ch)[:, t_p - hist_rows:])
        outs[3].append(ks.reshape(n_s, t_s, N_HEADS, 2, HEAD_DIM))
        outs[4].append(vs.reshape(n_s, t_s, N_HEADS, V_DIM))
        outs[5].append(us.reshape(n_s, t_s, conv_ch)[:, t_s - hist_rows:])

    return (yp.reshape(n_p, t_p, d_model), ys.reshape(n_s, t_s, d_model),
            *[jnp.stack(o, axis=0) for o in outs])
```

```python
import functools
import math

import jax
import jax.numpy as jnp
from jax import lax
from jax.experimental import pallas as pl
from jax.experimental.pallas import tpu as pltpu

N_HEADS = 4
HEAD_DIM = 64
V_DIM = 2 * HEAD_DIM
Q_WIDTH = N_HEADS * 2 * HEAD_DIM
ATTN_WIDTH = N_HEADS * V_DIM
CHUNK = 64
CONV_WIDTH = 31
EPS = 1e-6
NEG_INF = -1e30
ALIBI_SLOPES = tuple(2.0 ** (-8.0 * (h + 1) / N_HEADS) for h in range(N_HEADS))
LOG2E = math.log2(math.e)

V7X_LANES = 128
V7X_SUBLANES = 8
V7X_VMEM_BYTES = 64 * 1024 * 1024
VMEM_LIMIT_BYTES = V7X_VMEM_BYTES - 8 * 1024 * 1024

QUERY_BLOCK = 256
CONV_ROW_BLOCK = 128
HALO_ROWS = 32

F32 = jnp.float32
BF16 = jnp.bfloat16


def _dot(a, b):
    return jnp.dot(a, b, preferred_element_type=F32)


def _dot_nt(a, b):
    return lax.dot_general(a, b, (((1,), (1,)), ((), ())), preferred_element_type=F32)


def _rms_rows(x, g):
    return x * lax.rsqrt(jnp.mean(x * x, axis=-1, keepdims=True) + EPS) * g


def _params(n_grid_axes, flags=None):
    return pltpu.CompilerParams(
        dimension_semantics=("arbitrary",) * n_grid_axes,
        vmem_limit_bytes=VMEM_LIMIT_BYTES,
        flags=flags,
    )


def _const_spec(shape):
    nd = len(shape)
    return pl.BlockSpec(shape, lambda *_: (0,) * nd, pipeline_mode=pl.Buffered(1))


def _row_tile(n_rows, want):
    t = min(want, n_rows)
    assert n_rows % t == 0, (n_rows, t)
    return t


def _ffn_kernel(*refs, has_mix, has_final):
    it = iter(refs)
    x_ref = next(it)
    if has_mix:
        attn_ref, conv_ref, woa_ref, wob_ref = next(it), next(it), next(it), next(it)
    g_ref, wg_ref, wu_ref, wd_ref = next(it), next(it), next(it), next(it)
    gf_ref = next(it) if has_final else None
    out_ref = next(it)

    x = x_ref[...]
    if has_mix:
        x = x + _dot(attn_ref[...], woa_ref[...]) + _dot(conv_ref[...], wob_ref[...])
    h = _rms_rows(x, g_ref[...]).astype(BF16)
    a = _dot(h, wg_ref[...])
    b = _dot(h, wu_ref[...])
    act = (a * jax.nn.sigmoid(a) * b).astype(BF16)
    y = x + 0.5 * _dot(act, wd_ref[...])
    if has_final:
        y = _rms_rows(y, gf_ref[...])
    out_ref[...] = y


def _ffn(x, g, wg, wu, wd, mix=None, g_final=None, name="ffn"):
    T, D = x.shape
    F = wg.shape[1]
    tm = _row_tile(T, 256)
    row = lambda w: pl.BlockSpec((tm, w), lambda i: (i, 0))
    args, specs = [x], [row(D)]
    if mix is not None:
        attn, conv, woa, wob = mix
        args += [attn, conv, woa, wob]
        specs += [row(attn.shape[1]), row(conv.shape[1]), _const_spec(woa.shape), _const_spec(wob.shape)]
    args += [g, wg, wu, wd]
    specs += [_const_spec((1, D)), _const_spec((D, F)), _const_spec((D, F)), _const_spec((F, D))]
    if g_final is not None:
        args.append(g_final)
        specs.append(_const_spec((1, D)))
    return pl.pallas_call(
        functools.partial(_ffn_kernel, has_mix=mix is not None, has_final=g_final is not None),
        grid=(T // tm,),
        in_specs=specs,
        out_specs=row(D),
        out_shape=jax.ShapeDtypeStruct((T, D), F32),
        compiler_params=_params(1),
        name=name,
    )(*args)


def _group_rms(y, g, ones_bd):
    y2 = y * y
    hi = y2.astype(BF16)
    lo = (y2 - hi.astype(F32)).astype(BF16)
    ss = _dot(hi, ones_bd) + _dot(lo, ones_bd)
    return y * lax.rsqrt(ss * (1.0 / HEAD_DIM) + EPS) * g


def _alibi_query_lanes():
    c = jnp.asarray([s * LOG2E for s in ALIBI_SLOPES], F32)
    c1 = c.astype(BF16).astype(F32)
    c2 = (c - c1).astype(BF16).astype(F32)
    c3 = (c - c1 - c2).astype(BF16).astype(F32)
    pieces = jnp.stack([c1, c2, c3, c1, c2, c3], axis=1)
    return jnp.zeros((N_HEADS, V_DIM), F32).at[:, HEAD_DIM:HEAD_DIM + 6].set(pieces)


def _proj_kernel(*refs, key_tile):
    if key_tile is None:
        x_ref, g_ref, w_ref, gq_ref, gk_ref, ones_ref, qs_ref, k32_ref, v32_ref, u_ref = refs
    else:
        (x_ref, g_ref, w_ref, gq_ref, gk_ref, ones_ref, qext_ref,
         qs_ref, ks_ref, vt_ref, k32_ref, v32_ref, u_ref) = refs
    tm = x_ref.shape[0]
    h = _rms_rows(x_ref[...], g_ref[...]).astype(BF16)
    z = _dot(h, w_ref[...])
    ones_bd = ones_ref[...]
    qn = _group_rms(z[:, :Q_WIDTH], gq_ref[...], ones_bd) * (HEAD_DIM ** -0.5 * LOG2E)
    kn = _group_rms(z[:, Q_WIDTH:2 * Q_WIDTH], gk_ref[...], ones_bd)
    v = z[:, 2 * Q_WIDTH:2 * Q_WIDTH + ATTN_WIDTH]
    conv_ch = (z.shape[1] - 2 * Q_WIDTH - ATTN_WIDTH) // 2
    a = z[:, 2 * Q_WIDTH + ATTN_WIDTH:2 * Q_WIDTH + ATTN_WIDTH + conv_ch]
    gate = z[:, 2 * Q_WIDTH + ATTN_WIDTH + conv_ch:]
    k32_ref[...] = kn
    for hh in range(N_HEADS):
        v32_ref[pl.ds(hh, tm, stride=N_HEADS), :] = v[:, V_DIM * hh:V_DIM * (hh + 1)]
    u_ref[...] = a * jax.nn.sigmoid(gate)

    lane = lax.broadcasted_iota(jnp.int32, (tm, V_DIM), 1)
    low = lane < HEAD_DIM
    if key_tile is None:
        for hh in range(N_HEADS):
            qh = qn[:, V_DIM * hh:V_DIM * (hh + 1)]
            qs_ref[hh, 0] = jnp.where(low, qh, 0.0).astype(BF16)
            qs_ref[hh, 1] = jnp.where(low, 0.0, qh).astype(BF16)
        return

    row = pl.program_id(0) * tm + lax.broadcasted_iota(jnp.int32, (tm, V_DIM), 0)
    off = row % key_tile
    lo = off % 32
    in_hi = (lane >= HEAD_DIM) & (lane < HEAD_DIM + 3)
    in_lo = (lane >= HEAD_DIM + 3) & (lane < HEAD_DIM + 6)
    k_ext = jnp.where(in_hi, off - lo, jnp.where(in_lo, lo, 0)).astype(F32)
    for hh in range(N_HEADS):
        sl = slice(V_DIM * hh, V_DIM * (hh + 1))
        q_ext = qext_ref[hh:hh + 1, :]
        for x, ext, dst in ((qn[:, sl], q_ext, qs_ref), (kn[:, sl], k_ext, ks_ref)):
            dst[hh, 0] = jnp.where(low, x, ext).astype(BF16)
            dst[hh, 1] = jnp.where(low, pltpu.roll(x, HEAD_DIM, axis=1), ext).astype(BF16)
        vt_ref[hh] = v[:, sl].T.astype(BF16)


def _project(x1, g_mix, w_in, gq_t, gk_t, ones_bd, key_tile=None):
    T, D = x1.shape
    C = w_in.shape[1]
    conv_ch = (C - 2 * Q_WIDTH - ATTN_WIDTH) // 2
    tm = _row_tile(T, 256)
    row = lambda w: pl.BlockSpec((tm, w), lambda i: (i, 0))
    per_map = pl.BlockSpec((N_HEADS, 2, tm, V_DIM), lambda i: (0, 0, i, 0))
    per_map_shape = jax.ShapeDtypeStruct((N_HEADS, 2, T, V_DIM), BF16)
    args = [x1, g_mix, w_in, gq_t, gk_t, ones_bd]
    in_specs = [row(D), _const_spec((1, D)), _const_spec((D, C)), _const_spec((1, Q_WIDTH)),
                _const_spec((1, Q_WIDTH)), _const_spec((Q_WIDTH, Q_WIDTH))]
    out_specs, out_shape = [per_map], [per_map_shape]
    if key_tile is not None:
        args.append(_alibi_query_lanes())
        in_specs.append(_const_spec((N_HEADS, V_DIM)))
        out_specs += [per_map, pl.BlockSpec((N_HEADS, V_DIM, tm), lambda i: (0, 0, i))]
        out_shape += [per_map_shape, jax.ShapeDtypeStruct((N_HEADS, V_DIM, T), BF16)]
    out_specs += [row(Q_WIDTH), pl.BlockSpec((tm * N_HEADS, V_DIM), lambda i: (i, 0)), row(conv_ch)]
    out_shape += [jax.ShapeDtypeStruct(s, F32) for s in ((T, Q_WIDTH), (T * N_HEADS, V_DIM), (T, conv_ch))]
    return pl.pallas_call(
        functools.partial(_proj_kernel, key_tile=key_tile),
        grid=(T // tm,),
        in_specs=in_specs,
        out_specs=out_specs,
        out_shape=out_shape,
        compiler_params=_params(1),
        name="proj",
    )(*args)


def _diff_lambda(lam_ref, lam_init):
    lp = lam_ref[...]
    s1 = jnp.sum(lp[0:1] * lp[1:2], axis=-1, keepdims=True)
    s2 = jnp.sum(lp[2:3] * lp[3:4], axis=-1, keepdims=True)
    return jnp.exp(s1) - jnp.exp(s2) + lam_init


def _head_out(o1, o2, lam, g_sub, lam_init):
    d = o1 - lam * o2
    return _rms_rows(d, g_sub) * (1.0 - lam_init)


def _prompt_attn_kernel(qs_ref, ks_ref, vt_ref, lam_ref, gsub_ref, out_ref,
                        acc_ref, m_ref, l_ref, sa_ref, sb_ref, p_ref, a_ref, *, tq, tk, lam_init):
    hh = pl.program_id(0)
    qi = pl.program_id(1)
    cols = 2 * tq
    slope2 = jnp.float32(0.0)
    for h_static, sl in enumerate(ALIBI_SLOPES):
        slope2 = jnp.where(hh == h_static, jnp.float32(sl * LOG2E), slope2)

    acc_ref[...] = jnp.zeros(acc_ref.shape, F32)
    m_ref[...] = jnp.full(m_ref.shape, NEG_INF, F32)
    l_ref[...] = jnp.zeros(l_ref.shape, F32)

    q_start = qi * tq
    n_full = q_start // tk

    def tile_start(j):
        return pl.multiple_of(j * tk, tk)

    n_blocks = cols // QUERY_BLOCK

    def lanes(c):
        return slice(c * QUERY_BLOCK, (c + 1) * QUERY_BLOCK)

    def scores(j, s_ref, c):
        which, first = divmod(c * QUERY_BLOCK, tq)
        q = qs_ref[which, first:first + QUERY_BLOCK, :]
        s_ref[:, lanes(c)] = _dot_nt(ks_ref[which, pl.ds(tile_start(j), tk), :], q)

    def softmax_pv(j, s_ref, c, first_key=0, n_keys=tk):
        keys = slice(first_key, first_key + n_keys)
        t_off = slope2 * (j * tk - q_start).astype(F32)
        m = m_ref[:, lanes(c)]
        m_new = jnp.maximum(m, jnp.max(s_ref[keys, lanes(c)], axis=0, keepdims=True) + t_off)
        alpha = jnp.exp2(m - m_new)
        p = jnp.exp2(s_ref[keys, lanes(c)] - (m_new - t_off))
        l_ref[:, lanes(c)] = alpha * l_ref[:, lanes(c)] + jnp.sum(p, axis=0, keepdims=True)
        m_ref[:, lanes(c)] = m_new
        p_ref[keys, lanes(c)] = p.astype(BF16)
        a_ref[:, lanes(c)] = alpha
        vt = vt_ref[:, pl.ds(pl.multiple_of(j * tk + first_key, tq), n_keys)]
        acc_ref[:, lanes(c)] = a_ref[:, lanes(c)] * acc_ref[:, lanes(c)] + _dot(vt, p_ref[keys, lanes(c)])

    def mask_own(s_ref, c, first_key):
        keys = slice(first_key, first_key + tq)
        ko = lax.broadcasted_iota(jnp.int32, (tq, QUERY_BLOCK), 0)
        qo = (c * QUERY_BLOCK) % tq + lax.broadcasted_iota(jnp.int32, (tq, QUERY_BLOCK), 1)
        after = (-2.0 * slope2) * jnp.maximum(ko - qo, 0).astype(F32)
        visible = (ko // CHUNK) <= (qo // CHUNK)
        s_ref[keys, lanes(c)] = jnp.where(visible, s_ref[keys, lanes(c)] + after, NEG_INF)

    for c in range(n_blocks):
        scores(0, sa_ref, c)

    def step(j, cur, other):
        for c in range(n_blocks):
            scores(j + 1, other, c)
            softmax_pv(j, cur, c)

    def body(j, carry):
        @pl.when(j % 2 == 0)
        def _():
            step(j, sa_ref, sb_ref)

        @pl.when(j % 2 == 1)
        def _():
            step(j, sb_ref, sa_ref)

        return carry

    lax.fori_loop(0, n_full, body, 0)

    def last(cur):
        for sub in range(tk // tq):
            @pl.when(q_start - n_full * tk == sub * tq)
            def _(sub=sub):
                for c in range(n_blocks):
                    for before in range(sub):
                        softmax_pv(n_full, cur, c, before * tq, tq)
                    mask_own(cur, c, sub * tq)
                    softmax_pv(n_full, cur, c, sub * tq, tq)

    @pl.when(n_full % 2 == 0)
    def _():
        last(sa_ref)

    @pl.when(n_full % 2 == 1)
    def _():
        last(sb_ref)

    ot = acc_ref[...] * (1.0 / l_ref[...])
    lam = _diff_lambda(lam_ref, lam_init)
    dt = ot[:, :tq] - lam * ot[:, tq:]
    yt = dt * lax.rsqrt(jnp.mean(dt * dt, axis=0, keepdims=True) + EPS)
    out_ref[...] = (yt.T * gsub_ref[...] * (1.0 - lam_init)).astype(out_ref.dtype)


def _prompt_tiles(T):
    tk = _row_tile(T, 1024)
    tq = _row_tile(tk, 512)
    assert tq % QUERY_BLOCK == 0 and tk // 32 <= 256
    return tq, tk


def _prompt_attention(qs, ks, vt, lam_rows, g_sub, lam_init, tq, tk):
    T = ks.shape[2]
    return pl.pallas_call(
        functools.partial(_prompt_attn_kernel, tq=tq, tk=tk, lam_init=lam_init),
        grid=(N_HEADS, T // tq),
        in_specs=[
            pl.BlockSpec((None, 2, tq, V_DIM), lambda h, i: (h, 0, i, 0)),
            pl.BlockSpec((None, 2, T, V_DIM), lambda h, i: (h, 0, 0, 0)),
            pl.BlockSpec((None, V_DIM, T), lambda h, i: (h, 0, 0)),
            pl.BlockSpec((4, HEAD_DIM), lambda h, i: (0, 0)),
            pl.BlockSpec((1, V_DIM), lambda h, i: (0, 0)),
        ],
        out_specs=pl.BlockSpec((tq, V_DIM), lambda h, i: (i, h)),
        out_shape=jax.ShapeDtypeStruct((T, ATTN_WIDTH), BF16),
        scratch_shapes=[
            pltpu.VMEM((V_DIM, 2 * tq), F32),
            pltpu.VMEM((1, 2 * tq), F32),
            pltpu.VMEM((1, 2 * tq), F32),
            pltpu.VMEM((tk, 2 * tq), F32),
            pltpu.VMEM((tk, 2 * tq), F32),
            pltpu.VMEM((tk, 2 * tq), BF16),
            pltpu.VMEM((1, 2 * tq), F32),
        ],
        compiler_params=_params(2),
        name="prompt_attn",
    )(qs, ks, vt, lam_rows, g_sub)


def _sample_attn_kernel(qs_ref, kn_ref, vn_ref, ckt_ref, cv_ref, lam_ref, gsub_ref, out_ref,
                        m_ref, l_ref, acc_ref, *, t_new, past, tk, lam_init):
    grp = 2 * t_new
    rows = N_HEADS * grp
    blocks = []
    for hh in range(N_HEADS):
        qh = qs_ref[hh].reshape(grp, V_DIM)
        z = jnp.zeros((grp, V_DIM), BF16)
        blocks.append(jnp.concatenate([qh if c == hh else z for c in range(N_HEADS)], axis=1))
    q = jnp.concatenate(blocks, axis=0)

    r = lax.broadcasted_iota(jnp.int32, (rows, 1), 0)
    q_pos = past + (r % t_new)
    head = r // grp
    slope2 = jnp.zeros((rows, 1), F32)
    for h_static, sl in enumerate(ALIBI_SLOPES):
        slope2 = jnp.where(head == h_static, jnp.float32(sl * LOG2E), slope2)

    m_ref[...] = jnp.full(m_ref.shape, NEG_INF, F32)
    l_ref[...] = jnp.zeros(l_ref.shape, F32)
    acc_ref[...] = jnp.zeros(acc_ref.shape, F32)

    def step(qk, head_values, k_start, n_keys):
        k_pos = k_start + lax.broadcasted_iota(jnp.int32, (1, n_keys), 1)
        bias = -slope2 * jnp.abs(q_pos - k_pos).astype(F32)
        visible = (k_pos // CHUNK) <= (q_pos // CHUNK)
        s = jnp.where(visible, qk + bias, NEG_INF)
        m_prev = m_ref[...]
        m_new = jnp.maximum(m_prev, jnp.max(s, axis=1, keepdims=True))
        alpha = jnp.exp2(m_prev - m_new)
        p = jnp.exp2(s - m_new)
        l_ref[...] = alpha * l_ref[...] + jnp.sum(p, axis=1, keepdims=True)
        m_ref[...] = m_new
        p = p.astype(BF16)
        for hh in range(N_HEADS):
            mine = slice(hh * grp, (hh + 1) * grp)
            acc_ref[mine, :] = alpha[mine] * acc_ref[mine, :] + _dot(p[mine], head_values(hh).astype(BF16))

    for c in range(past // tk):
        step(_dot(q, ckt_ref[:, c * tk:(c + 1) * tk].astype(BF16)),
             lambda hh, c=c: cv_ref[pl.ds(c * tk * N_HEADS + hh, tk, stride=N_HEADS), :],
             c * tk, tk)
    step(_dot_nt(q, kn_ref[...].astype(BF16)), lambda hh: vn_ref[pl.ds(hh, t_new, stride=N_HEADS), :],
         past, t_new)

    o = acc_ref[...] / l_ref[...]
    lam = _diff_lambda(lam_ref, lam_init)
    outs = []
    for hh in range(N_HEADS):
        o1 = o[hh * grp:hh * grp + t_new]
        o2 = o[hh * grp + t_new:(hh + 1) * grp]
        outs.append(_head_out(o1, o2, lam, gsub_ref[...], lam_init))
    out_ref[...] = jnp.concatenate(outs, axis=1).astype(out_ref.dtype)


def _sample_attention(qs, k_new, v_new, cache_kt, cache_v, lam_rows, g_sub, lam_init, n_streams):
    t_new = k_new.shape[0] // n_streams
    past = cache_kt.shape[2]
    tk = _row_tile(past, 1024)
    rows = N_HEADS * 2 * t_new
    return pl.pallas_call(
        functools.partial(_sample_attn_kernel, t_new=t_new, past=past, tk=tk, lam_init=lam_init),
        grid=(n_streams,),
        in_specs=[
            pl.BlockSpec((N_HEADS, 2, t_new, V_DIM), lambda b: (0, 0, b, 0)),
            pl.BlockSpec((t_new, Q_WIDTH), lambda b: (b, 0)),
            pl.BlockSpec((t_new * N_HEADS, V_DIM), lambda b: (b, 0)),
            pl.BlockSpec((None, Q_WIDTH, past), lambda b: (b, 0, 0)),
            pl.BlockSpec((None, past * N_HEADS, V_DIM), lambda b: (b, 0, 0)),
            pl.BlockSpec((4, HEAD_DIM), lambda b: (0, 0)),
            pl.BlockSpec((1, V_DIM), lambda b: (0, 0)),
        ],
        out_specs=pl.BlockSpec((t_new, ATTN_WIDTH), lambda b: (b, 0)),
        out_shape=jax.ShapeDtypeStruct((n_streams * t_new, ATTN_WIDTH), BF16),
        scratch_shapes=[
            pltpu.VMEM((rows, 1), F32),
            pltpu.VMEM((rows, 1), F32),
            pltpu.VMEM((rows, V_DIM), F32),
        ],
        compiler_params=_params(1),
        name="sample_attn",
    )(qs, k_new, v_new, cache_kt, cache_v, lam_rows, g_sub)


def _conv_kernel(u_ref, halo_ref, init_ref, w_ref, b_ref, g_ref, beta_ref, out_ref,
                 xp_ref, y_ref, win_ref, *, tm):
    i = pl.program_id(1)
    hist = jnp.where(i == 0, init_ref[...], halo_ref[...])
    xp_ref[0:HALO_ROWS, :] = hist
    xp_ref[HALO_ROWS:HALO_ROWS + tm, :] = u_ref[...]
    first = HALO_ROWS - (CONV_WIDTH - 1)
    rows = min(tm, CONV_ROW_BLOCK)
    for r0 in range(0, tm, rows):
        for c0 in range(0, u_ref.shape[-1], V7X_LANES):
            ch = slice(c0, c0 + V7X_LANES)
            acc = jnp.zeros((rows, V7X_LANES), F32) + b_ref[:, ch]
            for phase in range(V7X_SUBLANES):
                n_taps = (CONV_WIDTH - 1 - phase) // V7X_SUBLANES + 1
                start = first + phase + r0
                n_win = rows + V7X_SUBLANES * (n_taps - 1)
                win_ref[0:n_win, :] = xp_ref[start:start + n_win, ch]
                for a in range(n_taps):
                    w = V7X_SUBLANES * a + phase
                    acc = acc + win_ref[V7X_SUBLANES * a:V7X_SUBLANES * a + rows, :] * w_ref[w:w + 1, ch]
            y_ref[r0:r0 + rows, ch] = acc
    y = y_ref[...]
    mu = jnp.mean(y, axis=-1, keepdims=True)
    d = y - mu
    var = jnp.mean(d * d, axis=-1, keepdims=True)
    yn = d * lax.rsqrt(var + EPS) * g_ref[...] + beta_ref[...]
    out_ref[...] = (yn * jax.nn.sigmoid(yn)).astype(out_ref.dtype)


def _conv_branch(u, init_hist, w_dw, b_dw, g_ln, b_ln):
    B, T, C = u.shape
    tm = _row_tile(T, 256)
    assert tm % HALO_ROWS == 0
    per = tm // HALO_ROWS
    vec = lambda: pl.BlockSpec((1, C), lambda b, i: (0, 0))
    return pl.pallas_call(
        functools.partial(_conv_kernel, tm=tm),
        grid=(B, T // tm),
        in_specs=[
            pl.BlockSpec((None, tm, C), lambda b, i: (b, i, 0)),
            pl.BlockSpec((None, HALO_ROWS, C), lambda b, i: (b, jnp.maximum(i * per - 1, 0), 0)),
            pl.BlockSpec((None, HALO_ROWS, C), lambda b, i: (b, 0, 0)),
            pl.BlockSpec((CONV_WIDTH, C), lambda b, i: (0, 0)),
            vec(), vec(), vec(),
        ],
        out_specs=pl.BlockSpec((None, tm, C), lambda b, i: (b, i, 0)),
        out_shape=jax.ShapeDtypeStruct((B, T, C), BF16),
        scratch_shapes=[
            pltpu.VMEM((HALO_ROWS + tm, C), F32),
            pltpu.VMEM((tm, C), F32),
            pltpu.VMEM((min(tm, CONV_ROW_BLOCK) + HALO_ROWS, V7X_LANES), F32),
        ],
        compiler_params=_params(2),
        name="conv",
    )(u, u, init_hist, w_dw, b_dw, g_ln, b_ln)


def _layer(x, n_streams, attend, key_tile, conv_hist, lam_init, p):
    T = x.shape[0] // n_streams
    x1 = _ffn(x, p["g_ffn1"], p["w1g"], p["w1u"], p["w1d"], name="ffn1")
    proj = _project(x1, p["g_mix"], p["w_in"], p["gq_t"], p["gk_t"], p["ones_bd"], key_tile)
    k32, v32, u = proj[-3:]
    attn = attend(*proj)
    conv_ch = u.shape[1]
    hist = jnp.pad(conv_hist, ((0, 0), (HALO_ROWS - conv_hist.shape[1], 0), (0, 0)))
    conv = _conv_branch(u.reshape(n_streams, T, conv_ch), hist, p["w_dw"], p["b_dw"], p["g_ln"], p["b_ln"])
    y = _ffn(x1, p["g_ffn2"], p["w2g"], p["w2u"], p["w2d"],
             mix=(attn, conv.reshape(n_streams * T, conv_ch), p["wo_attn"], p["wo_conv"]),
             g_final=p["g_final"], name="ffn2")
    return y, k32, v32, u


def kernel(x_prompt, x_sample, cache_k, cache_v, cache_conv, g_ffn1, w_ffn1_gu, w_ffn1_down, g_mix, w_in, g_q, g_k, lambda_q1, lambda_k1, lambda_q2, lambda_k2, g_sub, w_dw, b_dw, g_conv_ln, b_conv_ln, w_out, g_ffn2, w_ffn2_gu, w_ffn2_down, g_final):
    depth = cache_k.shape[0]
    n_p, t_p, d_model = x_prompt.shape
    n_s, t_s, _ = x_sample.shape
    past = cache_k.shape[2]
    d_ff = w_ffn1_down.shape[1]
    conv_ch = cache_conv.shape[-1]
    hist_rows = CONV_WIDTH - 1
    assert n_p == 1, "prompt attention kernel handles one prompt stream"
    assert t_p >= hist_rows and t_s >= hist_rows

    group = jnp.arange(Q_WIDTH) // HEAD_DIM
    ones_bd = (group[:, None] == group[None, :]).astype(BF16)
    row = lambda a: a.reshape(1, -1).astype(F32)

    yp = x_prompt.reshape(n_p * t_p, d_model)
    ys = x_sample.reshape(n_s * t_s, d_model)
    outs = [[] for _ in range(6)]
    for l in range(depth):
        lam_init = 0.8 - 0.6 * math.exp(-0.3 * l)
        lam_rows = jnp.stack([lambda_q1[l], lambda_k1[l], lambda_q2[l], lambda_k2[l]]).astype(F32)
        p = dict(
            g_ffn1=row(g_ffn1[l]), w1g=w_ffn1_gu[l][:, :d_ff].astype(BF16), w1u=w_ffn1_gu[l][:, d_ff:].astype(BF16),
            w1d=w_ffn1_down[l].astype(BF16), g_mix=row(g_mix[l]), w_in=w_in[l].astype(BF16),
            gq_t=row(jnp.tile(g_q[l], 2 * N_HEADS)), gk_t=row(jnp.tile(g_k[l], 2 * N_HEADS)), ones_bd=ones_bd,
            w_dw=w_dw[l].astype(F32), b_dw=row(b_dw[l]), g_ln=row(g_conv_ln[l]), b_ln=row(b_conv_ln[l]),
            wo_attn=w_out[l][:ATTN_WIDTH].astype(BF16), wo_conv=w_out[l][ATTN_WIDTH:].astype(BF16),
            g_ffn2=row(g_ffn2[l]), w2g=w_ffn2_gu[l][:, :d_ff].astype(BF16), w2u=w_ffn2_gu[l][:, d_ff:].astype(BF16),
            w2d=w_ffn2_down[l].astype(BF16), g_final=row(g_final[l]),
        )
        g_sub_row = row(g_sub[l])

        q_tile, key_tile = _prompt_tiles(t_p)

        def attend_prompt(qs, ks, vt, k32, v32, u):
            return _prompt_attention(qs, ks, vt, lam_rows, g_sub_row, lam_init, q_tile, key_tile)

        def attend_sample(qs, k32, v32, u, l=l):
            ck = jnp.swapaxes(cache_k[l].reshape(n_s, past, Q_WIDTH), 1, 2)
            cv = cache_v[l].reshape(n_s, past * N_HEADS, V_DIM)
            return _sample_attention(qs, k32, v32, ck, cv, lam_rows, g_sub_row, lam_init, n_s)

        zero_hist = jnp.zeros((n_p, hist_rows, conv_ch), F32)
        yp, kp, vp, up = _layer(yp, n_p, attend_prompt, key_tile, zero_hist, lam_init, p)
        ys, ks, vs, us = _layer(ys, n_s, attend_sample, None, cache_conv[l], lam_init, p)
        outs[0].append(kp.reshape(n_p, t_p, N_HEADS, 2, HEAD_DIM))
        outs[1].append(vp.reshape(n_p, t_p, N_HEADS, V_DIM))
        outs[2].append(up.reshape(n_p, t_p, conv_ch)[:, t_p - hist_rows:])
        outs[3].append(ks.reshape(n_s, t_s, N_HEADS, 2, HEAD_DIM))
        outs[4].append(vs.reshape(n_s, t_s, N_HEADS, V_DIM))
        outs[5].append(us.reshape(n_s, t_s, conv_ch)[:, t_s - hist_rows:])

    return (yp.reshape(n_p, t_p, d_model), ys.reshape(n_s, t_s, d_model),
            *[jnp.stack(o, axis=0) for o in outs])
```

```python
import functools
import math

import jax
import jax.numpy as jnp
from jax import lax
from jax.experimental import pallas as pl
from jax.experimental.pallas import tpu as pltpu

N_HEADS = 4
HEAD_DIM = 64
V_DIM = 2 * HEAD_DIM
Q_WIDTH = N_HEADS * 2 * HEAD_DIM
ATTN_WIDTH = N_HEADS * V_DIM
CHUNK = 64
CONV_WIDTH = 31
EPS = 1e-6
NEG_INF = -1e30
ALIBI_SLOPES = tuple(2.0 ** (-8.0 * (h + 1) / N_HEADS) for h in range(N_HEADS))
LOG2E = math.log2(math.e)

V7X_LANES = 128
V7X_SUBLANES = 8
V7X_MXU_DEPTH = 256
V7X_VMEM_BYTES = 64 * 1024 * 1024
VMEM_LIMIT_BYTES = V7X_VMEM_BYTES - 8 * 1024 * 1024

QUERY_BLOCK = 256
CONV_ROW_BLOCK = 128
HALO_ROWS = 32

F32 = jnp.float32
BF16 = jnp.bfloat16


def _dot(a, b):
    return jnp.dot(a, b, preferred_element_type=F32)


def _dot_nt(a, b):
    return lax.dot_general(a, b, (((1,), (1,)), ((), ())), preferred_element_type=F32)


def _rms_rows(x, g):
    return x * lax.rsqrt(jnp.mean(x * x, axis=-1, keepdims=True) + EPS) * g


def _params(n_grid_axes, flags=None):
    return pltpu.CompilerParams(
        dimension_semantics=("arbitrary",) * n_grid_axes,
        vmem_limit_bytes=VMEM_LIMIT_BYTES,
        flags=flags,
    )


def _const_spec(shape):
    nd = len(shape)
    return pl.BlockSpec(shape, lambda *_: (0,) * nd, pipeline_mode=pl.Buffered(1))


def _row_tile(n_rows, want):
    t = min(want, n_rows)
    assert n_rows % t == 0, (n_rows, t)
    return t


def _ffn_kernel(*refs, has_mix, has_final):
    it = iter(refs)
    x_ref = next(it)
    if has_mix:
        attn_ref, conv_ref, woa_ref, wob_ref = next(it), next(it), next(it), next(it)
    g_ref, wg_ref, wu_ref, wd_ref = next(it), next(it), next(it), next(it)
    gf_ref = next(it) if has_final else None
    out_ref = next(it)

    x = x_ref[...]
    if has_mix:
        x = x + _dot(attn_ref[...], woa_ref[...]) + _dot(conv_ref[...], wob_ref[...])
    h = _rms_rows(x, g_ref[...]).astype(BF16)
    a = _dot(h, wg_ref[...])
    b = _dot(h, wu_ref[...])
    act = (a * jax.nn.sigmoid(a) * b).astype(BF16)
    y = x + 0.5 * _dot(act, wd_ref[...])
    if has_final:
        y = _rms_rows(y, gf_ref[...])
    out_ref[...] = y


def _ffn(x, g, wg, wu, wd, mix=None, g_final=None, name="ffn"):
    T, D = x.shape
    F = wg.shape[1]
    tm = _row_tile(T, 512)
    row = lambda w: pl.BlockSpec((tm, w), lambda i: (i, 0))
    args, specs = [x], [row(D)]
    if mix is not None:
        attn, conv, woa, wob = mix
        args += [attn, conv, woa, wob]
        specs += [row(attn.shape[1]), row(conv.shape[1]), _const_spec(woa.shape), _const_spec(wob.shape)]
    args += [g, wg, wu, wd]
    specs += [_const_spec((1, D)), _const_spec((D, F)), _const_spec((D, F)), _const_spec((F, D))]
    if g_final is not None:
        args.append(g_final)
        specs.append(_const_spec((1, D)))
    return pl.pallas_call(
        functools.partial(_ffn_kernel, has_mix=mix is not None, has_final=g_final is not None),
        grid=(T // tm,),
        in_specs=specs,
        out_specs=row(D),
        out_shape=jax.ShapeDtypeStruct((T, D), F32),
        compiler_params=_params(1),
        name=name,
    )(*args)


def _group_rms(y, g, ones_bd):
    y2 = y * y
    hi = y2.astype(BF16)
    lo = (y2 - hi.astype(F32)).astype(BF16)
    ss = _dot(hi, ones_bd) + _dot(lo, ones_bd)
    return y * lax.rsqrt(ss * (1.0 / HEAD_DIM) + EPS) * g


def _alibi_query_lanes():
    c = jnp.asarray([s * LOG2E for s in ALIBI_SLOPES], F32)
    c1 = c.astype(BF16).astype(F32)
    c2 = (c - c1).astype(BF16).astype(F32)
    c3 = (c - c1 - c2).astype(BF16).astype(F32)
    pieces = jnp.stack([c1, c2, c3, c1, c2, c3], axis=1)
    return jnp.zeros((N_HEADS, V_DIM), F32).at[:, HEAD_DIM:HEAD_DIM + 6].set(pieces)


def _proj_kernel(*refs, key_tile):
    if key_tile is None:
        x_ref, g_ref, w_ref, gq_ref, gk_ref, ones_ref, qs_ref, k32_ref, v32_ref, u_ref = refs
    else:
        (x_ref, g_ref, w_ref, gq_ref, gk_ref, ones_ref, qext_ref,
         qs_ref, ks_ref, vt_ref, k32_ref, v32_ref, u_ref) = refs
    tm = x_ref.shape[0]
    h = _rms_rows(x_ref[...], g_ref[...]).astype(BF16)
    z = _dot(h, w_ref[...])
    ones_bd = ones_ref[...]
    qn = _group_rms(z[:, :Q_WIDTH], gq_ref[...], ones_bd) * (HEAD_DIM ** -0.5 * LOG2E)
    kn = _group_rms(z[:, Q_WIDTH:2 * Q_WIDTH], gk_ref[...], ones_bd)
    v = z[:, 2 * Q_WIDTH:2 * Q_WIDTH + ATTN_WIDTH]
    conv_ch = (z.shape[1] - 2 * Q_WIDTH - ATTN_WIDTH) // 2
    a = z[:, 2 * Q_WIDTH + ATTN_WIDTH:2 * Q_WIDTH + ATTN_WIDTH + conv_ch]
    gate = z[:, 2 * Q_WIDTH + ATTN_WIDTH + conv_ch:]
    k32_ref[...] = kn
    for hh in range(N_HEADS):
        v32_ref[pl.ds(hh, tm, stride=N_HEADS), :] = v[:, V_DIM * hh:V_DIM * (hh + 1)]
    u_ref[...] = a * jax.nn.sigmoid(gate)

    lane = lax.broadcasted_iota(jnp.int32, (tm, V_DIM), 1)
    low = lane < HEAD_DIM
    if key_tile is None:
        for hh in range(N_HEADS):
            qh = qn[:, V_DIM * hh:V_DIM * (hh + 1)]
            qs_ref[hh, 0] = jnp.where(low, qh, 0.0).astype(BF16)
            qs_ref[hh, 1] = jnp.where(low, 0.0, qh).astype(BF16)
        return

    row = pl.program_id(0) * tm + lax.broadcasted_iota(jnp.int32, (tm, V_DIM), 0)
    off = row % key_tile
    lo = off % 32
    in_hi = (lane >= HEAD_DIM) & (lane < HEAD_DIM + 3)
    in_lo = (lane >= HEAD_DIM + 3) & (lane < HEAD_DIM + 6)
    k_ext = jnp.where(in_hi, off - lo, jnp.where(in_lo, lo, 0)).astype(F32)
    for hh in range(N_HEADS):
        sl = slice(V_DIM * hh, V_DIM * (hh + 1))
        q_ext = qext_ref[hh:hh + 1, :]
        for x, ext, dst in ((qn[:, sl], q_ext, qs_ref), (kn[:, sl], k_ext, ks_ref)):
            dst[hh, 0] = jnp.where(low, x, ext).astype(BF16)
            dst[hh, 1] = jnp.where(low, pltpu.roll(x, HEAD_DIM, axis=1), ext).astype(BF16)
        vt_ref[hh] = v[:, sl].T.astype(BF16)


def _project(x1, g_mix, w_in, gq_t, gk_t, ones_bd, key_tile=None):
    T, D = x1.shape
    C = w_in.shape[1]
    conv_ch = (C - 2 * Q_WIDTH - ATTN_WIDTH) // 2
    tm = _row_tile(T, 512)
    row = lambda w: pl.BlockSpec((tm, w), lambda i: (i, 0))
    per_map = pl.BlockSpec((N_HEADS, 2, tm, V_DIM), lambda i: (0, 0, i, 0))
    per_map_shape = jax.ShapeDtypeStruct((N_HEADS, 2, T, V_DIM), BF16)
    args = [x1, g_mix, w_in, gq_t, gk_t, ones_bd]
    in_specs = [row(D), _const_spec((1, D)), _const_spec((D, C)), _const_spec((1, Q_WIDTH)),
                _const_spec((1, Q_WIDTH)), _const_spec((Q_WIDTH, Q_WIDTH))]
    out_specs, out_shape = [per_map], [per_map_shape]
    if key_tile is not None:
        args.append(_alibi_query_lanes())
        in_specs.append(_const_spec((N_HEADS, V_DIM)))
        out_specs += [per_map, pl.BlockSpec((N_HEADS, V_DIM, tm), lambda i: (0, 0, i))]
        out_shape += [per_map_shape, jax.ShapeDtypeStruct((N_HEADS, V_DIM, T), BF16)]
    out_specs += [row(Q_WIDTH), pl.BlockSpec((tm * N_HEADS, V_DIM), lambda i: (i, 0)), row(conv_ch)]
    out_shape += [jax.ShapeDtypeStruct(s, F32) for s in ((T, Q_WIDTH), (T * N_HEADS, V_DIM), (T, conv_ch))]
    return pl.pallas_call(
        functools.partial(_proj_kernel, key_tile=key_tile),
        grid=(T // tm,),
        in_specs=in_specs,
        out_specs=out_specs,
        out_shape=out_shape,
        compiler_params=_params(1),
        name="proj",
    )(*args)


def _diff_lambda(lam_ref, lam_init):
    lp = lam_ref[...]
    s1 = jnp.sum(lp[0:1] * lp[1:2], axis=-1, keepdims=True)
    s2 = jnp.sum(lp[2:3] * lp[3:4], axis=-1, keepdims=True)
    return jnp.exp(s1) - jnp.exp(s2) + lam_init


def _head_out(o1, o2, lam, g_sub, lam_init):
    d = o1 - lam * o2
    return _rms_rows(d, g_sub) * (1.0 - lam_init)


def _prompt_attn_kernel(qs_ref, ks_ref, vt_ref, lam_ref, gsub_ref, out_ref,
                        acc_ref, m_ref, l_ref, sa_ref, sb_ref, p_ref, a_ref, *, tq, tk, lam_init):
    hh = pl.program_id(0)
    qi = pl.program_id(1)
    cols = 2 * tq
    slope2 = jnp.float32(0.0)
    for h_static, sl in enumerate(ALIBI_SLOPES):
        slope2 = jnp.where(hh == h_static, jnp.float32(sl * LOG2E), slope2)

    acc_ref[...] = jnp.zeros(acc_ref.shape, F32)
    m_ref[...] = jnp.full(m_ref.shape, NEG_INF, F32)
    l_ref[...] = jnp.zeros(l_ref.shape, F32)

    q_start = qi * tq
    n_full = q_start // tk

    def tile_start(j):
        return pl.multiple_of(j * tk, tk)

    n_blocks = cols // QUERY_BLOCK

    def lanes(c):
        return slice(c * QUERY_BLOCK, (c + 1) * QUERY_BLOCK)

    def scores(j, s_ref, c):
        which, first = divmod(c * QUERY_BLOCK, tq)
        q = qs_ref[which, first:first + QUERY_BLOCK, :]
        s_ref[:, lanes(c)] = _dot_nt(ks_ref[which, pl.ds(tile_start(j), tk), :], q)

    def softmax(j, s_ref, c, first_key=0, n_keys=tk):
        keys = slice(first_key, first_key + n_keys)
        t_off = slope2 * (j * tk - q_start).astype(F32)
        m = m_ref[:, lanes(c)]
        m_new = jnp.maximum(m, jnp.max(s_ref[keys, lanes(c)], axis=0, keepdims=True) + t_off)
        alpha = jnp.exp2(m - m_new)
        p = jnp.exp2(s_ref[keys, lanes(c)] - (m_new - t_off))
        l_ref[:, lanes(c)] = alpha * l_ref[:, lanes(c)] + jnp.sum(p, axis=0, keepdims=True)
        m_ref[:, lanes(c)] = m_new
        p_ref[keys, lanes(c)] = p.astype(BF16)
        a_ref[:, lanes(c)] = alpha

    def pv(j, c, first_key=0, n_keys=tk):
        new = a_ref[:, lanes(c)] * acc_ref[:, lanes(c)]
        for k0 in range(first_key, first_key + n_keys, V7X_MXU_DEPTH):
            vt = vt_ref[:, pl.ds(pl.multiple_of(j * tk + k0, V7X_MXU_DEPTH), V7X_MXU_DEPTH)]
            new = new + _dot(vt, p_ref[k0:k0 + V7X_MXU_DEPTH, lanes(c)])
        acc_ref[:, lanes(c)] = new

    def softmax_pv(j, s_ref, c, first_key, n_keys):
        softmax(j, s_ref, c, first_key, n_keys)
        pv(j, c, first_key, n_keys)

    def mask_own(s_ref, c, first_key):
        keys = slice(first_key, first_key + tq)
        ko = lax.broadcasted_iota(jnp.int32, (tq, QUERY_BLOCK), 0)
        qo = (c * QUERY_BLOCK) % tq + lax.broadcasted_iota(jnp.int32, (tq, QUERY_BLOCK), 1)
        after = (-2.0 * slope2) * jnp.maximum(ko - qo, 0).astype(F32)
        visible = (ko // CHUNK) <= (qo // CHUNK)
        s_ref[keys, lanes(c)] = jnp.where(visible, s_ref[keys, lanes(c)] + after, NEG_INF)

    for c in range(n_blocks):
        scores(0, sa_ref, c)

    deferred = n_blocks - 1
    p_ref[:, lanes(deferred)] = jnp.zeros((tk, QUERY_BLOCK), BF16)
    a_ref[:, lanes(deferred)] = jnp.ones((1, QUERY_BLOCK), F32)

    def step(j, cur, other):
        for c in range(n_blocks):
            scores(j + 1, other, c)
            if c == 0:
                pv(jnp.maximum(j - 1, 0), deferred)
            softmax(j, cur, c)
            if c > 0:
                pv(j, c - 1)

    def body(j, carry):
        @pl.when(j % 2 == 0)
        def _():
            step(j, sa_ref, sb_ref)

        @pl.when(j % 2 == 1)
        def _():
            step(j, sb_ref, sa_ref)

        return carry

    lax.fori_loop(0, n_full, body, 0)

    def last(cur):
        pv(jnp.maximum(n_full - 1, 0), deferred)
        for sub in range(tk // tq):
            @pl.when(q_start - n_full * tk == sub * tq)
            def _(sub=sub):
                for c in range(n_blocks):
                    for before in range(sub):
                        softmax_pv(n_full, cur, c, before * tq, tq)
                    mask_own(cur, c, sub * tq)
                    softmax_pv(n_full, cur, c, sub * tq, tq)

    @pl.when(n_full % 2 == 0)
    def _():
        last(sa_ref)

    @pl.when(n_full % 2 == 1)
    def _():
        last(sb_ref)

    ot = acc_ref[...] * (1.0 / l_ref[...])
    lam = _diff_lambda(lam_ref, lam_init)
    dt = ot[:, :tq] - lam * ot[:, tq:]
    yt = dt * lax.rsqrt(jnp.mean(dt * dt, axis=0, keepdims=True) + EPS)
    out_ref[...] = (yt.T * gsub_ref[...] * (1.0 - lam_init)).astype(out_ref.dtype)


def _prompt_tiles(T):
    tk = _row_tile(T, 1024)
    tq = _row_tile(tk, 512)
    assert tq % QUERY_BLOCK == 0 and tk // 32 <= 256
    return tq, tk


def _prompt_attention(qs, ks, vt, lam_rows, g_sub, lam_init, tq, tk):
    T = ks.shape[2]
    return pl.pallas_call(
        functools.partial(_prompt_attn_kernel, tq=tq, tk=tk, lam_init=lam_init),
        grid=(N_HEADS, T // tq),
        in_specs=[
            pl.BlockSpec((None, 2, tq, V_DIM), lambda h, i: (h, 0, i, 0)),
            pl.BlockSpec((None, 2, T, V_DIM), lambda h, i: (h, 0, 0, 0)),
            pl.BlockSpec((None, V_DIM, T), lambda h, i: (h, 0, 0)),
            pl.BlockSpec((4, HEAD_DIM), lambda h, i: (0, 0)),
            pl.BlockSpec((1, V_DIM), lambda h, i: (0, 0)),
        ],
        out_specs=pl.BlockSpec((tq, V_DIM), lambda h, i: (i, h)),
        out_shape=jax.ShapeDtypeStruct((T, ATTN_WIDTH), BF16),
        scratch_shapes=[
            pltpu.VMEM((V_DIM, 2 * tq), F32),
            pltpu.VMEM((1, 2 * tq), F32),
            pltpu.VMEM((1, 2 * tq), F32),
            pltpu.VMEM((tk, 2 * tq), F32),
            pltpu.VMEM((tk, 2 * tq), F32),
            pltpu.VMEM((tk, 2 * tq), BF16),
            pltpu.VMEM((1, 2 * tq), F32),
        ],
        compiler_params=_params(2),
        name="prompt_attn",
    )(qs, ks, vt, lam_rows, g_sub)


def _sample_attn_kernel(qs_ref, kn_ref, vn_ref, ckt_ref, cv_ref, lam_ref, gsub_ref, out_ref,
                        m_ref, l_ref, acc_ref, *, t_new, past, tk, lam_init):
    grp = 2 * t_new
    rows = N_HEADS * grp
    blocks = []
    for hh in range(N_HEADS):
        qh = qs_ref[hh].reshape(grp, V_DIM)
        z = jnp.zeros((grp, V_DIM), BF16)
        blocks.append(jnp.concatenate([qh if c == hh else z for c in range(N_HEADS)], axis=1))
    q = jnp.concatenate(blocks, axis=0)

    r = lax.broadcasted_iota(jnp.int32, (rows, 1), 0)
    q_pos = past + (r % t_new)
    head = r // grp
    slope2 = jnp.zeros((rows, 1), F32)
    for h_static, sl in enumerate(ALIBI_SLOPES):
        slope2 = jnp.where(head == h_static, jnp.float32(sl * LOG2E), slope2)

    m_ref[...] = jnp.full(m_ref.shape, NEG_INF, F32)
    l_ref[...] = jnp.zeros(l_ref.shape, F32)
    acc_ref[...] = jnp.zeros(acc_ref.shape, F32)

    def step(qk, head_values, k_start, n_keys):
        k_pos = k_start + lax.broadcasted_iota(jnp.int32, (1, n_keys), 1)
        bias = -slope2 * jnp.abs(q_pos - k_pos).astype(F32)
        visible = (k_pos // CHUNK) <= (q_pos // CHUNK)
        s = jnp.where(visible, qk + bias, NEG_INF)
        m_prev = m_ref[...]
        m_new = jnp.maximum(m_prev, jnp.max(s, axis=1, keepdims=True))
        alpha = jnp.exp2(m_prev - m_new)
        p = jnp.exp2(s - m_new)
        l_ref[...] = alpha * l_ref[...] + jnp.sum(p, axis=1, keepdims=True)
        m_ref[...] = m_new
        p = p.astype(BF16)
        for hh in range(N_HEADS):
            mine = slice(hh * grp, (hh + 1) * grp)
            acc_ref[mine, :] = alpha[mine] * acc_ref[mine, :] + _dot(p[mine], head_values(hh).astype(BF16))

    for c in range(past // tk):
        step(_dot(q, ckt_ref[:, c * tk:(c + 1) * tk].astype(BF16)),
             lambda hh, c=c: cv_ref[pl.ds(c * tk * N_HEADS + hh, tk, stride=N_HEADS), :],
             c * tk, tk)
    step(_dot_nt(q, kn_ref[...].astype(BF16)), lambda hh: vn_ref[pl.ds(hh, t_new, stride=N_HEADS), :],
         past, t_new)

    o = acc_ref[...] / l_ref[...]
    lam = _diff_lambda(lam_ref, lam_init)
    outs = []
    for hh in range(N_HEADS):
        o1 = o[hh * grp:hh * grp + t_new]
        o2 = o[hh * grp + t_new:(hh + 1) * grp]
        outs.append(_head_out(o1, o2, lam, gsub_ref[...], lam_init))
    out_ref[...] = jnp.concatenate(outs, axis=1).astype(out_ref.dtype)


def _sample_attention(qs, k_new, v_new, cache_kt, cache_v, lam_rows, g_sub, lam_init, n_streams):
    t_new = k_new.shape[0] // n_streams
    past = cache_kt.shape[2]
    tk = _row_tile(past, 1024)
    rows = N_HEADS * 2 * t_new
    return pl.pallas_call(
        functools.partial(_sample_attn_kernel, t_new=t_new, past=past, tk=tk, lam_init=lam_init),
        grid=(n_streams,),
        in_specs=[
            pl.BlockSpec((N_HEADS, 2, t_new, V_DIM), lambda b: (0, 0, b, 0)),
            pl.BlockSpec((t_new, Q_WIDTH), lambda b: (b, 0)),
            pl.BlockSpec((t_new * N_HEADS, V_DIM), lambda b: (b, 0)),
            pl.BlockSpec((None, Q_WIDTH, past), lambda b: (b, 0, 0)),
            pl.BlockSpec((None, past * N_HEADS, V_DIM), lambda b: (b, 0, 0)),
            pl.BlockSpec((4, HEAD_DIM), lambda b: (0, 0)),
            pl.BlockSpec((1, V_DIM), lambda b: (0, 0)),
        ],
        out_specs=pl.BlockSpec((t_new, ATTN_WIDTH), lambda b: (b, 0)),
        out_shape=jax.ShapeDtypeStruct((n_streams * t_new, ATTN_WIDTH), BF16),
        scratch_shapes=[
            pltpu.VMEM((rows, 1), F32),
            pltpu.VMEM((rows, 1), F32),
            pltpu.VMEM((rows, V_DIM), F32),
        ],
        compiler_params=_params(1),
        name="sample_attn",
    )(qs, k_new, v_new, cache_kt, cache_v, lam_rows, g_sub)


def _conv_kernel(u_ref, halo_ref, init_ref, w_ref, b_ref, g_ref, beta_ref, out_ref,
                 xp_ref, y_ref, win_ref, *, tm):
    i = pl.program_id(1)
    hist = jnp.where(i == 0, init_ref[...], halo_ref[...])
    xp_ref[0:HALO_ROWS, :] = hist
    xp_ref[HALO_ROWS:HALO_ROWS + tm, :] = u_ref[...]
    first = HALO_ROWS - (CONV_WIDTH - 1)
    rows = min(tm, CONV_ROW_BLOCK)
    for r0 in range(0, tm, rows):
        for c0 in range(0, u_ref.shape[-1], V7X_LANES):
            ch = slice(c0, c0 + V7X_LANES)
            acc = jnp.zeros((rows, V7X_LANES), F32) + b_ref[:, ch]
            for phase in range(V7X_SUBLANES):
                n_taps = (CONV_WIDTH - 1 - phase) // V7X_SUBLANES + 1
                start = first + phase + r0
                n_win = rows + V7X_SUBLANES * (n_taps - 1)
                win_ref[0:n_win, :] = xp_ref[start:start + n_win, ch]
                win = win_ref[0:n_win, :]
                for a in range(n_taps):
                    w = V7X_SUBLANES * a + phase
                    acc = acc + win[V7X_SUBLANES * a:V7X_SUBLANES * a + rows] * w_ref[w:w + 1, ch]
            y_ref[r0:r0 + rows, ch] = acc
    y = y_ref[...]
    mu = jnp.mean(y, axis=-1, keepdims=True)
    d = y - mu
    var = jnp.mean(d * d, axis=-1, keepdims=True)
    yn = d * lax.rsqrt(var + EPS) * g_ref[...] + beta_ref[...]
    out_ref[...] = (yn * jax.nn.sigmoid(yn)).astype(out_ref.dtype)


def _conv_branch(u, init_hist, w_dw, b_dw, g_ln, b_ln):
    B, T, C = u.shape
    tm = _row_tile(T, 256)
    assert tm % HALO_ROWS == 0
    per = tm // HALO_ROWS
    vec = lambda: pl.BlockSpec((1, C), lambda b, i: (0, 0))
    return pl.pallas_call(
        functools.partial(_conv_kernel, tm=tm),
        grid=(B, T // tm),
        in_specs=[
            pl.BlockSpec((None, tm, C), lambda b, i: (b, i, 0)),
            pl.BlockSpec((None, HALO_ROWS, C), lambda b, i: (b, jnp.maximum(i * per - 1, 0), 0)),
            pl.BlockSpec((None, HALO_ROWS, C), lambda b, i: (b, 0, 0)),
            pl.BlockSpec((CONV_WIDTH, C), lambda b, i: (0, 0)),
            vec(), vec(), vec(),
        ],
        out_specs=pl.BlockSpec((None, tm, C), lambda b, i: (b, i, 0)),
        out_shape=jax.ShapeDtypeStruct((B, T, C), BF16),
        scratch_shapes=[
            pltpu.VMEM((HALO_ROWS + tm, C), F32),
            pltpu.VMEM((tm, C), F32),
            pltpu.VMEM((min(tm, CONV_ROW_BLOCK) + HALO_ROWS, V7X_LANES), F32),
        ],
        compiler_params=_params(2),
        name="conv",
    )(u, u, init_hist, w_dw, b_dw, g_ln, b_ln)


def _layer(x, n_streams, attend, key_tile, conv_hist, lam_init, p):
    T = x.shape[0] // n_streams
    x1 = _ffn(x, p["g_ffn1"], p["w1g"], p["w1u"], p["w1d"], name="ffn1")
    proj = _project(x1, p["g_mix"], p["w_in"], p["gq_t"], p["gk_t"], p["ones_bd"], key_tile)
    k32, v32, u = proj[-3:]
    attn = attend(*proj)
    conv_ch = u.shape[1]
    hist = jnp.pad(conv_hist, ((0, 0), (HALO_ROWS - conv_hist.shape[1], 0), (0, 0)))
    conv = _conv_branch(u.reshape(n_streams, T, conv_ch), hist, p["w_dw"], p["b_dw"], p["g_ln"], p["b_ln"])
    y = _ffn(x1, p["g_ffn2"], p["w2g"], p["w2u"], p["w2d"],
             mix=(attn, conv.reshape(n_streams * T, conv_ch), p["wo_attn"], p["wo_conv"]),
             g_final=p["g_final"], name="ffn2")
    return y, k32, v32, u


def kernel(x_prompt, x_sample, cache_k, cache_v, cache_conv, g_ffn1, w_ffn1_gu, w_ffn1_down, g_mix, w_in, g_q, g_k, lambda_q1, lambda_k1, lambda_q2, lambda_k2, g_sub, w_dw, b_dw, g_conv_ln, b_conv_ln, w_out, g_ffn2, w_ffn2_gu, w_ffn2_down, g_final):
    depth = cache_k.shape[0]
    n_p, t_p, d_model = x_prompt.shape
    n_s, t_s, _ = x_sample.shape
    past = cache_k.shape[2]
    d_ff = w_ffn1_down.shape[1]
    conv_ch = cache_conv.shape[-1]
    hist_rows = CONV_WIDTH - 1
    assert n_p == 1, "prompt attention kernel handles one prompt stream"
    assert t_p >= hist_rows and t_s >= hist_rows

    group = jnp.arange(Q_WIDTH) // HEAD_DIM
    ones_bd = (group[:, None] == group[None, :]).astype(BF16)
    row = lambda a: a.reshape(1, -1).astype(F32)

    yp = x_prompt.reshape(n_p * t_p, d_model)
    ys = x_sample.reshape(n_s * t_s, d_model)
    outs = [[] for _ in range(6)]
    for l in range(depth):
        lam_init = 0.8 - 0.6 * math.exp(-0.3 * l)
        lam_rows = jnp.stack([lambda_q1[l], lambda_k1[l], lambda_q2[l], lambda_k2[l]]).astype(F32)
        p = dict(
            g_ffn1=row(g_ffn1[l]), w1g=w_ffn1_gu[l][:, :d_ff].astype(BF16), w1u=w_ffn1_gu[l][:, d_ff:].astype(BF16),
            w1d=w_ffn1_down[l].astype(BF16), g_mix=row(g_mix[l]), w_in=w_in[l].astype(BF16),
            gq_t=row(jnp.tile(g_q[l], 2 * N_HEADS)), gk_t=row(jnp.tile(g_k[l], 2 * N_HEADS)), ones_bd=ones_bd,
            w_dw=w_dw[l].astype(F32), b_dw=row(b_dw[l]), g_ln=row(g_conv_ln[l]), b_ln=row(b_conv_ln[l]),
            wo_attn=w_out[l][:ATTN_WIDTH].astype(BF16), wo_conv=w_out[l][ATTN_WIDTH:].astype(BF16),
            g_ffn2=row(g_ffn2[l]), w2g=w_ffn2_gu[l][:, :d_ff].astype(BF16), w2u=w_ffn2_gu[l][:, d_ff:].astype(BF16),
            w2d=w_ffn2_down[l].astype(BF16), g_final=row(g_final[l]),
        )
        g_sub_row = row(g_sub[l])

        q_tile, key_tile = _prompt_tiles(t_p)

        def attend_prompt(qs, ks, vt, k32, v32, u):
            return _prompt_attention(qs, ks, vt, lam_rows, g_sub_row, lam_init, q_tile, key_tile)

        def attend_sample(qs, k32, v32, u, l=l):
            ck = jnp.swapaxes(cache_k[l].reshape(n_s, past, Q_WIDTH), 1, 2)
            cv = cache_v[l].reshape(n_s, past * N_HEADS, V_DIM)
            return _sample_attention(qs, k32, v32, ck, cv, lam_rows, g_sub_row, lam_init, n_s)

        zero_hist = jnp.zeros((n_p, hist_rows, conv_ch), F32)
        yp, kp, vp, up = _layer(yp, n_p, attend_prompt, key_tile, zero_hist, lam_init, p)
        ys, ks, vs, us = _layer(ys, n_s, attend_sample, None, cache_conv[l], lam_init, p)
        outs[0].append(kp.reshape(n_p, t_p, N_HEADS, 2, HEAD_DIM))
        outs[1].append(vp.reshape(n_p, t_p, N_HEADS, V_DIM))
        outs[2].append(up.reshape(n_p, t_p, conv_ch)[:, t_p - hist_rows:])
        outs[3].append(ks.reshape(n_s, t_s, N_HEADS, 2, HEAD_DIM))
        outs[4].append(vs.reshape(n_s, t_s, N_HEADS, V_DIM))
        outs[5].append(us.reshape(n_s, t_s, conv_ch)[:, t_s - hist_rows:])

    return (yp.reshape(n_p, t_p, d_model), ys.reshape(n_s, t_s, d_model),
            *[jnp.stack(o, axis=0) for o in outs])
```

```python
import functools
import math

import jax
import jax.numpy as jnp
from jax import lax
from jax.experimental import pallas as pl
from jax.experimental.pallas import tpu as pltpu

N_HEADS = 4
HEAD_DIM = 64
V_DIM = 2 * HEAD_DIM
Q_WIDTH = N_HEADS * 2 * HEAD_DIM
ATTN_WIDTH = N_HEADS * V_DIM
CHUNK = 64
CONV_WIDTH = 31
EPS = 1e-6
NEG_INF = -1e30
ALIBI_SLOPES = tuple(2.0 ** (-8.0 * (h + 1) / N_HEADS) for h in range(N_HEADS))
LOG2E = math.log2(math.e)

V7X_LANES = 128
V7X_SUBLANES = 8
V7X_VMEM_BYTES = 64 * 1024 * 1024
VMEM_LIMIT_BYTES = V7X_VMEM_BYTES - 8 * 1024 * 1024

QUERY_BLOCK = 256
CONV_ROW_BLOCK = 128
HALO_ROWS = 32

F32 = jnp.float32
BF16 = jnp.bfloat16


def _dot(a, b):
    return jnp.dot(a, b, preferred_element_type=F32)


def _dot_nt(a, b):
    return lax.dot_general(a, b, (((1,), (1,)), ((), ())), preferred_element_type=F32)


def _rms_rows(x, g):
    return x * lax.rsqrt(jnp.mean(x * x, axis=-1, keepdims=True) + EPS) * g


def _params(n_grid_axes, flags=None):
    return pltpu.CompilerParams(
        dimension_semantics=("arbitrary",) * n_grid_axes,
        vmem_limit_bytes=VMEM_LIMIT_BYTES,
        flags=flags,
    )


def _const_spec(shape):
    nd = len(shape)
    return pl.BlockSpec(shape, lambda *_: (0,) * nd, pipeline_mode=pl.Buffered(1))


def _row_tile(n_rows, want):
    t = min(want, n_rows)
    assert n_rows % t == 0, (n_rows, t)
    return t


def _ffn_kernel(*refs, has_mix, has_final):
    it = iter(refs)
    x_ref = next(it)
    if has_mix:
        attn_ref, conv_ref, woa_ref, wob_ref = next(it), next(it), next(it), next(it)
    g_ref, wg_ref, wu_ref, wd_ref = next(it), next(it), next(it), next(it)
    gf_ref = next(it) if has_final else None
    out_ref = next(it)

    x = x_ref[...]
    if has_mix:
        x = x + _dot(attn_ref[...], woa_ref[...]) + _dot(conv_ref[...], wob_ref[...])
    h = _rms_rows(x, g_ref[...]).astype(BF16)
    a = _dot(h, wg_ref[...])
    b = _dot(h, wu_ref[...])
    act = (a * jax.nn.sigmoid(a) * b).astype(BF16)
    y = x + 0.5 * _dot(act, wd_ref[...])
    if has_final:
        y = _rms_rows(y, gf_ref[...])
    out_ref[...] = y


def _ffn(x, g, wg, wu, wd, mix=None, g_final=None, name="ffn"):
    T, D = x.shape
    F = wg.shape[1]
    tm = _row_tile(T, 512)
    row = lambda w: pl.BlockSpec((tm, w), lambda i: (i, 0))
    args, specs = [x], [row(D)]
    if mix is not None:
        attn, conv, woa, wob = mix
        args += [attn, conv, woa, wob]
        specs += [row(attn.shape[1]), row(conv.shape[1]), _const_spec(woa.shape), _const_spec(wob.shape)]
    args += [g, wg, wu, wd]
    specs += [_const_spec((1, D)), _const_spec((D, F)), _const_spec((D, F)), _const_spec((F, D))]
    if g_final is not None:
        args.append(g_final)
        specs.append(_const_spec((1, D)))
    return pl.pallas_call(
        functools.partial(_ffn_kernel, has_mix=mix is not None, has_final=g_final is not None),
        grid=(T // tm,),
        in_specs=specs,
        out_specs=row(D),
        out_shape=jax.ShapeDtypeStruct((T, D), F32),
        compiler_params=_params(1),
        name=name,
    )(*args)


def _group_rms(y, g, ones_bd):
    y2 = y * y
    hi = y2.astype(BF16)
    lo = (y2 - hi.astype(F32)).astype(BF16)
    ss = _dot(hi, ones_bd) + _dot(lo, ones_bd)
    return y * lax.rsqrt(ss * (1.0 / HEAD_DIM) + EPS) * g


def _alibi_query_lanes():
    c = jnp.asarray([s * LOG2E for s in ALIBI_SLOPES], F32)
    c1 = c.astype(BF16).astype(F32)
    c2 = (c - c1).astype(BF16).astype(F32)
    c3 = (c - c1 - c2).astype(BF16).astype(F32)
    pieces = jnp.stack([c1, c2, c3, c1, c2, c3], axis=1)
    return jnp.zeros((N_HEADS, V_DIM), F32).at[:, HEAD_DIM:HEAD_DIM + 6].set(pieces)


def _proj_kernel(*refs, key_tile):
    if key_tile is None:
        x_ref, g_ref, w_ref, gq_ref, gk_ref, ones_ref, qs_ref, k32_ref, v32_ref, u_ref = refs
    else:
        (x_ref, g_ref, w_ref, gq_ref, gk_ref, ones_ref, qext_ref,
         qs_ref, ks_ref, vt_ref, k32_ref, v32_ref, u_ref) = refs
    tm = x_ref.shape[0]
    h = _rms_rows(x_ref[...], g_ref[...]).astype(BF16)
    z = _dot(h, w_ref[...])
    ones_bd = ones_ref[...]
    qn = _group_rms(z[:, :Q_WIDTH], gq_ref[...], ones_bd) * (HEAD_DIM ** -0.5 * LOG2E)
    kn = _group_rms(z[:, Q_WIDTH:2 * Q_WIDTH], gk_ref[...], ones_bd)
    v = z[:, 2 * Q_WIDTH:2 * Q_WIDTH + ATTN_WIDTH]
    conv_ch = (z.shape[1] - 2 * Q_WIDTH - ATTN_WIDTH) // 2
    a = z[:, 2 * Q_WIDTH + ATTN_WIDTH:2 * Q_WIDTH + ATTN_WIDTH + conv_ch]
    gate = z[:, 2 * Q_WIDTH + ATTN_WIDTH + conv_ch:]
    k32_ref[...] = kn
    for hh in range(N_HEADS):
        v32_ref[pl.ds(hh, tm, stride=N_HEADS), :] = v[:, V_DIM * hh:V_DIM * (hh + 1)]
    u_ref[...] = a * jax.nn.sigmoid(gate)

    lane = lax.broadcasted_iota(jnp.int32, (tm, V_DIM), 1)
    low = lane < HEAD_DIM
    if key_tile is None:
        for hh in range(N_HEADS):
            qh = qn[:, V_DIM * hh:V_DIM * (hh + 1)]
            qs_ref[hh, 0] = jnp.where(low, qh, 0.0).astype(BF16)
            qs_ref[hh, 1] = jnp.where(low, 0.0, qh).astype(BF16)
        return

    row = pl.program_id(0) * tm + lax.broadcasted_iota(jnp.int32, (tm, V_DIM), 0)
    off = row % key_tile
    lo = off % 32
    in_hi = (lane >= HEAD_DIM) & (lane < HEAD_DIM + 3)
    in_lo = (lane >= HEAD_DIM + 3) & (lane < HEAD_DIM + 6)
    k_ext = jnp.where(in_hi, off - lo, jnp.where(in_lo, lo, 0)).astype(F32)
    for hh in range(N_HEADS):
        sl = slice(V_DIM * hh, V_DIM * (hh + 1))
        q_ext = qext_ref[hh:hh + 1, :]
        for x, ext, dst in ((qn[:, sl], q_ext, qs_ref), (kn[:, sl], k_ext, ks_ref)):
            dst[hh, 0] = jnp.where(low, x, ext).astype(BF16)
            dst[hh, 1] = jnp.where(low, pltpu.roll(x, HEAD_DIM, axis=1), ext).astype(BF16)
        vt_ref[hh] = v[:, sl].T.astype(BF16)


def _project(x1, g_mix, w_in, gq_t, gk_t, ones_bd, key_tile=None):
    T, D = x1.shape
    C = w_in.shape[1]
    conv_ch = (C - 2 * Q_WIDTH - ATTN_WIDTH) // 2
    tm = _row_tile(T, 512)
    row = lambda w: pl.BlockSpec((tm, w), lambda i: (i, 0))
    per_map = pl.BlockSpec((N_HEADS, 2, tm, V_DIM), lambda i: (0, 0, i, 0))
    per_map_shape = jax.ShapeDtypeStruct((N_HEADS, 2, T, V_DIM), BF16)
    args = [x1, g_mix, w_in, gq_t, gk_t, ones_bd]
    in_specs = [row(D), _const_spec((1, D)), _const_spec((D, C)), _const_spec((1, Q_WIDTH)),
                _const_spec((1, Q_WIDTH)), _const_spec((Q_WIDTH, Q_WIDTH))]
    out_specs, out_shape = [per_map], [per_map_shape]
    if key_tile is not None:
        args.append(_alibi_query_lanes())
        in_specs.append(_const_spec((N_HEADS, V_DIM)))
        out_specs += [per_map, pl.BlockSpec((N_HEADS, V_DIM, tm), lambda i: (0, 0, i))]
        out_shape += [per_map_shape, jax.ShapeDtypeStruct((N_HEADS, V_DIM, T), BF16)]
    out_specs += [row(Q_WIDTH), pl.BlockSpec((tm * N_HEADS, V_DIM), lambda i: (i, 0)), row(conv_ch)]
    out_shape += [jax.ShapeDtypeStruct(s, F32) for s in ((T, Q_WIDTH), (T * N_HEADS, V_DIM), (T, conv_ch))]
    return pl.pallas_call(
        functools.partial(_proj_kernel, key_tile=key_tile),
        grid=(T // tm,),
        in_specs=in_specs,
        out_specs=out_specs,
        out_shape=out_shape,
        compiler_params=_params(1),
        name="proj",
    )(*args)


def _diff_lambda(lam_ref, lam_init):
    lp = lam_ref[...]
    s1 = jnp.sum(lp[0:1] * lp[1:2], axis=-1, keepdims=True)
    s2 = jnp.sum(lp[2:3] * lp[3:4], axis=-1, keepdims=True)
    return jnp.exp(s1) - jnp.exp(s2) + lam_init


def _head_out(o1, o2, lam, g_sub, lam_init):
    d = o1 - lam * o2
    return _rms_rows(d, g_sub) * (1.0 - lam_init)


def _prompt_attn_kernel(qs_ref, ks_ref, vt_ref, lam_ref, gsub_ref, out_ref,
                        acc_ref, m_ref, l_ref, sa_ref, sb_ref, p_ref, a_ref, *, tq, tk, lam_init):
    hh = pl.program_id(0)
    qi = pl.program_id(1)
    cols = 2 * tq
    slope2 = jnp.float32(0.0)
    for h_static, sl in enumerate(ALIBI_SLOPES):
        slope2 = jnp.where(hh == h_static, jnp.float32(sl * LOG2E), slope2)

    acc_ref[...] = jnp.zeros(acc_ref.shape, F32)
    m_ref[...] = jnp.full(m_ref.shape, NEG_INF, F32)
    l_ref[...] = jnp.zeros(l_ref.shape, F32)

    q_start = qi * tq
    n_full = q_start // tk

    def tile_start(j):
        return pl.multiple_of(j * tk, tk)

    n_blocks = cols // QUERY_BLOCK

    def lanes(c):
        return slice(c * QUERY_BLOCK, (c + 1) * QUERY_BLOCK)

    def scores(j, s_ref, c):
        which, first = divmod(c * QUERY_BLOCK, tq)
        q = qs_ref[which, first:first + QUERY_BLOCK, :]
        s_ref[:, lanes(c)] = _dot_nt(ks_ref[which, pl.ds(tile_start(j), tk), :], q)

    def softmax_pv(j, s_ref, c, first_key=0, n_keys=tk):
        keys = slice(first_key, first_key + n_keys)
        t_off = slope2 * (j * tk - q_start).astype(F32)
        m = m_ref[:, lanes(c)]
        m_new = jnp.maximum(m, jnp.max(s_ref[keys, lanes(c)], axis=0, keepdims=True) + t_off)
        alpha = jnp.exp2(m - m_new)
        p = jnp.exp2(s_ref[keys, lanes(c)] - (m_new - t_off))
        l_ref[:, lanes(c)] = alpha * l_ref[:, lanes(c)] + jnp.sum(p, axis=0, keepdims=True)
        m_ref[:, lanes(c)] = m_new
        p_ref[keys, lanes(c)] = p.astype(BF16)
        a_ref[:, lanes(c)] = alpha
        vt = vt_ref[:, pl.ds(pl.multiple_of(j * tk + first_key, tq), n_keys)]
        acc_ref[:, lanes(c)] = a_ref[:, lanes(c)] * acc_ref[:, lanes(c)] + _dot(vt, p_ref[keys, lanes(c)])

    def mask_own(s_ref, c, first_key):
        keys = slice(first_key, first_key + tq)
        ko = lax.broadcasted_iota(jnp.int32, (tq, QUERY_BLOCK), 0)
        qo = (c * QUERY_BLOCK) % tq + lax.broadcasted_iota(jnp.int32, (tq, QUERY_BLOCK), 1)
        after = (-2.0 * slope2) * jnp.maximum(ko - qo, 0).astype(F32)
        visible = (ko // CHUNK) <= (qo // CHUNK)
        s_ref[keys, lanes(c)] = jnp.where(visible, s_ref[keys, lanes(c)] + after, NEG_INF)

    for c in range(n_blocks):
        scores(0, sa_ref, c)

    def step(j, cur, other):
        for c in range(n_blocks):
            scores(j + 1, other, c)
            softmax_pv(j, cur, c)

    def body(j, carry):
        @pl.when(j % 2 == 0)
        def _():
            step(j, sa_ref, sb_ref)

        @pl.when(j % 2 == 1)
        def _():
            step(j, sb_ref, sa_ref)

        return carry

    lax.fori_loop(0, n_full, body, 0)

    def last(cur):
        for sub in range(tk // tq):
            @pl.when(q_start - n_full * tk == sub * tq)
            def _(sub=sub):
                for c in range(n_blocks):
                    for before in range(sub):
                        softmax_pv(n_full, cur, c, before * tq, tq)
                    mask_own(cur, c, sub * tq)
                    softmax_pv(n_full, cur, c, sub * tq, tq)

    @pl.when(n_full % 2 == 0)
    def _():
        last(sa_ref)

    @pl.when(n_full % 2 == 1)
    def _():
        last(sb_ref)

    ot = acc_ref[...] * (1.0 / l_ref[...])
    lam = _diff_lambda(lam_ref, lam_init)
    dt = ot[:, :tq] - lam * ot[:, tq:]
    yt = dt * lax.rsqrt(jnp.mean(dt * dt, axis=0, keepdims=True) + EPS)
    out_ref[...] = (yt.T * gsub_ref[...] * (1.0 - lam_init)).astype(out_ref.dtype)


def _prompt_tiles(T):
    tk = _row_tile(T, 1024)
    tq = _row_tile(tk, 1024)
    assert tq % QUERY_BLOCK == 0 and tk // 32 <= 256
    return tq, tk


def _prompt_attention(qs, ks, vt, lam_rows, g_sub, lam_init, tq, tk):
    T = ks.shape[2]
    return pl.pallas_call(
        functools.partial(_prompt_attn_kernel, tq=tq, tk=tk, lam_init=lam_init),
        grid=(N_HEADS, T // tq),
        in_specs=[
            pl.BlockSpec((None, 2, tq, V_DIM), lambda h, i: (h, 0, i, 0)),
            pl.BlockSpec((None, 2, T, V_DIM), lambda h, i: (h, 0, 0, 0), pipeline_mode=pl.Buffered(1)),
            pl.BlockSpec((None, V_DIM, T), lambda h, i: (h, 0, 0), pipeline_mode=pl.Buffered(1)),
            pl.BlockSpec((4, HEAD_DIM), lambda h, i: (0, 0)),
            pl.BlockSpec((1, V_DIM), lambda h, i: (0, 0)),
        ],
        out_specs=pl.BlockSpec((tq, V_DIM), lambda h, i: (i, h)),
        out_shape=jax.ShapeDtypeStruct((T, ATTN_WIDTH), BF16),
        scratch_shapes=[
            pltpu.VMEM((V_DIM, 2 * tq), F32),
            pltpu.VMEM((1, 2 * tq), F32),
            pltpu.VMEM((1, 2 * tq), F32),
            pltpu.VMEM((tk, 2 * tq), F32),
            pltpu.VMEM((tk, 2 * tq), F32),
            pltpu.VMEM((tk, 2 * tq), BF16),
            pltpu.VMEM((1, 2 * tq), F32),
        ],
        compiler_params=_params(2),
        name="prompt_attn",
    )(qs, ks, vt, lam_rows, g_sub)


def _sample_attn_kernel(qs_ref, kn_ref, vn_ref, ckt_ref, cv_ref, lam_ref, gsub_ref, out_ref,
                        m_ref, l_ref, acc_ref, *, t_new, past, tk, lam_init):
    grp = 2 * t_new
    rows = N_HEADS * grp
    blocks = []
    for hh in range(N_HEADS):
        qh = qs_ref[hh].reshape(grp, V_DIM)
        z = jnp.zeros((grp, V_DIM), BF16)
        blocks.append(jnp.concatenate([qh if c == hh else z for c in range(N_HEADS)], axis=1))
    q = jnp.concatenate(blocks, axis=0)

    r = lax.broadcasted_iota(jnp.int32, (rows, 1), 0)
    q_pos = past + (r % t_new)
    head = r // grp
    slope2 = jnp.zeros((rows, 1), F32)
    for h_static, sl in enumerate(ALIBI_SLOPES):
        slope2 = jnp.where(head == h_static, jnp.float32(sl * LOG2E), slope2)

    m_ref[...] = jnp.full(m_ref.shape, NEG_INF, F32)
    l_ref[...] = jnp.zeros(l_ref.shape, F32)
    acc_ref[...] = jnp.zeros(acc_ref.shape, F32)

    def step(qk, head_values, k_start, n_keys):
        k_pos = k_start + lax.broadcasted_iota(jnp.int32, (1, n_keys), 1)
        bias = -slope2 * jnp.abs(q_pos - k_pos).astype(F32)
        visible = (k_pos // CHUNK) <= (q_pos // CHUNK)
        s = jnp.where(visible, qk + bias, NEG_INF)
        m_prev = m_ref[...]
        m_new = jnp.maximum(m_prev, jnp.max(s, axis=1, keepdims=True))
        alpha = jnp.exp2(m_prev - m_new)
        p = jnp.exp2(s - m_new)
        l_ref[...] = alpha * l_ref[...] + jnp.sum(p, axis=1, keepdims=True)
        m_ref[...] = m_new
        p = p.astype(BF16)
        for hh in range(N_HEADS):
            mine = slice(hh * grp, (hh + 1) * grp)
            acc_ref[mine, :] = alpha[mine] * acc_ref[mine, :] + _dot(p[mine], head_values(hh).astype(BF16))

    for c in range(past // tk):
        step(_dot(q, ckt_ref[:, c * tk:(c + 1) * tk].astype(BF16)),
             lambda hh, c=c: cv_ref[pl.ds(c * tk * N_HEADS + hh, tk, stride=N_HEADS), :],
             c * tk, tk)
    step(_dot_nt(q, kn_ref[...].astype(BF16)), lambda hh: vn_ref[pl.ds(hh, t_new, stride=N_HEADS), :],
         past, t_new)

    o = acc_ref[...] / l_ref[...]
    lam = _diff_lambda(lam_ref, lam_init)
    outs = []
    for hh in range(N_HEADS):
        o1 = o[hh * grp:hh * grp + t_new]
        o2 = o[hh * grp + t_new:(hh + 1) * grp]
        outs.append(_head_out(o1, o2, lam, gsub_ref[...], lam_init))
    out_ref[...] = jnp.concatenate(outs, axis=1).astype(out_ref.dtype)


def _sample_attention(qs, k_new, v_new, cache_kt, cache_v, lam_rows, g_sub, lam_init, n_streams):
    t_new = k_new.shape[0] // n_streams
    past = cache_kt.shape[2]
    tk = _row_tile(past, 1024)
    rows = N_HEADS * 2 * t_new
    return pl.pallas_call(
        functools.partial(_sample_attn_kernel, t_new=t_new, past=past, tk=tk, lam_init=lam_init),
        grid=(n_streams,),
        in_specs=[
            pl.BlockSpec((N_HEADS, 2, t_new, V_DIM), lambda b: (0, 0, b, 0)),
            pl.BlockSpec((t_new, Q_WIDTH), lambda b: (b, 0)),
            pl.BlockSpec((t_new * N_HEADS, V_DIM), lambda b: (b, 0)),
            pl.BlockSpec((None, Q_WIDTH, past), lambda b: (b, 0, 0)),
            pl.BlockSpec((None, past * N_HEADS, V_DIM), lambda b: (b, 0, 0)),
            pl.BlockSpec((4, HEAD_DIM), lambda b: (0, 0)),
            pl.BlockSpec((1, V_DIM), lambda b: (0, 0)),
        ],
        out_specs=pl.BlockSpec((t_new, ATTN_WIDTH), lambda b: (b, 0)),
        out_shape=jax.ShapeDtypeStruct((n_streams * t_new, ATTN_WIDTH), BF16),
        scratch_shapes=[
            pltpu.VMEM((rows, 1), F32),
            pltpu.VMEM((rows, 1), F32),
            pltpu.VMEM((rows, V_DIM), F32),
        ],
        compiler_params=_params(1),
        name="sample_attn",
    )(qs, k_new, v_new, cache_kt, cache_v, lam_rows, g_sub)


def _conv_kernel(u_ref, halo_ref, init_ref, w_ref, b_ref, g_ref, beta_ref, out_ref,
                 xp_ref, y_ref, win_ref, *, tm):
    i = pl.program_id(1)
    hist = jnp.where(i == 0, init_ref[...], halo_ref[...])
    xp_ref[0:HALO_ROWS, :] = hist
    xp_ref[HALO_ROWS:HALO_ROWS + tm, :] = u_ref[...]
    first = HALO_ROWS - (CONV_WIDTH - 1)
    rows = min(tm, CONV_ROW_BLOCK)
    for r0 in range(0, tm, rows):
        for c0 in range(0, u_ref.shape[-1], V7X_LANES):
            ch = slice(c0, c0 + V7X_LANES)
            acc = jnp.zeros((rows, V7X_LANES), F32) + b_ref[:, ch]
            for phase in range(V7X_SUBLANES):
                n_taps = (CONV_WIDTH - 1 - phase) // V7X_SUBLANES + 1
                start = first + phase + r0
                n_win = rows + V7X_SUBLANES * (n_taps - 1)
                win_ref[0:n_win, :] = xp_ref[start:start + n_win, ch]
                win = win_ref[0:n_win, :]
                for a in range(n_taps):
                    w = V7X_SUBLANES * a + phase
                    acc = acc + win[V7X_SUBLANES * a:V7X_SUBLANES * a + rows] * w_ref[w:w + 1, ch]
            y_ref[r0:r0 + rows, ch] = acc
    y = y_ref[...]
    mu = jnp.mean(y, axis=-1, keepdims=True)
    d = y - mu
    var = jnp.mean(d * d, axis=-1, keepdims=True)
    yn = d * lax.rsqrt(var + EPS) * g_ref[...] + beta_ref[...]
    out_ref[...] = (yn * jax.nn.sigmoid(yn)).astype(out_ref.dtype)


def _conv_branch(u, init_hist, w_dw, b_dw, g_ln, b_ln):
    B, T, C = u.shape
    tm = _row_tile(T, 256)
    assert tm % HALO_ROWS == 0
    per = tm // HALO_ROWS
    vec = lambda: pl.BlockSpec((1, C), lambda b, i: (0, 0))
    return pl.pallas_call(
        functools.partial(_conv_kernel, tm=tm),
        grid=(B, T // tm),
        in_specs=[
            pl.BlockSpec((None, tm, C), lambda b, i: (b, i, 0)),
            pl.BlockSpec((None, HALO_ROWS, C), lambda b, i: (b, jnp.maximum(i * per - 1, 0), 0)),
            pl.BlockSpec((None, HALO_ROWS, C), lambda b, i: (b, 0, 0)),
            pl.BlockSpec((CONV_WIDTH, C), lambda b, i: (0, 0)),
            vec(), vec(), vec(),
        ],
        out_specs=pl.BlockSpec((None, tm, C), lambda b, i: (b, i, 0)),
        out_shape=jax.ShapeDtypeStruct((B, T, C), BF16),
        scratch_shapes=[
            pltpu.VMEM((HALO_ROWS + tm, C), F32),
            pltpu.VMEM((tm, C), F32),
            pltpu.VMEM((min(tm, CONV_ROW_BLOCK) + HALO_ROWS, V7X_LANES), F32),
        ],
        compiler_params=_params(2),
        name="conv",
    )(u, u, init_hist, w_dw, b_dw, g_ln, b_ln)


def _layer(x, n_streams, attend, key_tile, conv_hist, lam_init, p):
    T = x.shape[0] // n_streams
    x1 = _ffn(x, p["g_ffn1"], p["w1g"], p["w1u"], p["w1d"], name="ffn1")
    proj = _project(x1, p["g_mix"], p["w_in"], p["gq_t"], p["gk_t"], p["ones_bd"], key_tile)
    k32, v32, u = proj[-3:]
    attn = attend(*proj)
    conv_ch = u.shape[1]
    hist = jnp.pad(conv_hist, ((0, 0), (HALO_ROWS - conv_hist.shape[1], 0), (0, 0)))
    conv = _conv_branch(u.reshape(n_streams, T, conv_ch), hist, p["w_dw"], p["b_dw"], p["g_ln"], p["b_ln"])
    y = _ffn(x1, p["g_ffn2"], p["w2g"], p["w2u"], p["w2d"],
             mix=(attn, conv.reshape(n_streams * T, conv_ch), p["wo_attn"], p["wo_conv"]),
             g_final=p["g_final"], name="ffn2")
    return y, k32, v32, u


def kernel(x_prompt, x_sample, cache_k, cache_v, cache_conv, g_ffn1, w_ffn1_gu, w_ffn1_down, g_mix, w_in, g_q, g_k, lambda_q1, lambda_k1, lambda_q2, lambda_k2, g_sub, w_dw, b_dw, g_conv_ln, b_conv_ln, w_out, g_ffn2, w_ffn2_gu, w_ffn2_down, g_final):
    depth = cache_k.shape[0]
    n_p, t_p, d_model = x_prompt.shape
    n_s, t_s, _ = x_sample.shape
    past = cache_k.shape[2]
    d_ff = w_ffn1_down.shape[1]
    conv_ch = cache_conv.shape[-1]
    hist_rows = CONV_WIDTH - 1
    assert n_p == 1, "prompt attention kernel handles one prompt stream"
    assert t_p >= hist_rows and t_s >= hist_rows

    group = jnp.arange(Q_WIDTH) // HEAD_DIM
    ones_bd = (group[:, None] == group[None, :]).astype(BF16)
    row = lambda a: a.reshape(1, -1).astype(F32)

    yp = x_prompt.reshape(n_p * t_p, d_model)
    ys = x_sample.reshape(n_s * t_s, d_model)
    outs = [[] for _ in range(6)]
    for l in range(depth):
        lam_init = 0.8 - 0.6 * math.exp(-0.3 * l)
        lam_rows = jnp.stack([lambda_q1[l], lambda_k1[l], lambda_q2[l], lambda_k2[l]]).astype(F32)
        p = dict(
            g_ffn1=row(g_ffn1[l]), w1g=w_ffn1_gu[l][:, :d_ff].astype(BF16), w1u=w_ffn1_gu[l][:, d_ff:].astype(BF16),
            w1d=w_ffn1_down[l].astype(BF16), g_mix=row(g_mix[l]), w_in=w_in[l].astype(BF16),
            gq_t=row(jnp.tile(g_q[l], 2 * N_HEADS)), gk_t=row(jnp.tile(g_k[l], 2 * N_HEADS)), ones_bd=ones_bd,
            w_dw=w_dw[l].astype(F32), b_dw=row(b_dw[l]), g_ln=row(g_conv_ln[l]), b_ln=row(b_conv_ln[l]),
            wo_attn=w_out[l][:ATTN_WIDTH].astype(BF16), wo_conv=w_out[l][ATTN_WIDTH:].astype(BF16),
            g_ffn2=row(g_ffn2[l]), w2g=w_ffn2_gu[l][:, :d_ff].astype(BF16), w2u=w_ffn2_gu[l][:, d_ff:].astype(BF16),
            w2d=w_ffn2_down[l].astype(BF16), g_final=row(g_final[l]),
        )
        g_sub_row = row(g_sub[l])

        q_tile, key_tile = _prompt_tiles(t_p)

        def attend_prompt(qs, ks, vt, k32, v32, u):
            return _prompt_attention(qs, ks, vt, lam_rows, g_sub_row, lam_init, q_tile, key_tile)

        def attend_sample(qs, k32, v32, u, l=l):
            ck = jnp.swapaxes(cache_k[l].reshape(n_s, past, Q_WIDTH), 1, 2)
            cv = cache_v[l].reshape(n_s, past * N_HEADS, V_DIM)
            return _sample_attention(qs, k32, v32, ck, cv, lam_rows, g_sub_row, lam_init, n_s)

        zero_hist = jnp.zeros((n_p, hist_rows, conv_ch), F32)
        yp, kp, vp, up = _layer(yp, n_p, attend_prompt, key_tile, zero_hist, lam_init, p)
        ys, ks, vs, us = _layer(ys, n_s, attend_sample, None, cache_conv[l], lam_init, p)
        outs[0].append(kp.reshape(n_p, t_p, N_HEADS, 2, HEAD_DIM))
        outs[1].append(vp.reshape(n_p, t_p, N_HEADS, V_DIM))
        outs[2].append(up.reshape(n_p, t_p, conv_ch)[:, t_p - hist_rows:])
        outs[3].append(ks.reshape(n_s, t_s, N_HEADS, 2, HEAD_DIM))
        outs[4].append(vs.reshape(n_s, t_s, N_HEADS, V_DIM))
        outs[5].append(us.reshape(n_s, t_s, conv_ch)[:, t_s - hist_rows:])

    return (yp.reshape(n_p, t_p, d_model), ys.reshape(n_s, t_s, d_model),
            *[jnp.stack(o, axis=0) for o in outs])
```

```python
import functools
import math

import jax
import jax.numpy as jnp
from jax import lax
from jax.experimental import pallas as pl
from jax.experimental.pallas import tpu as pltpu

N_HEADS = 4
HEAD_DIM = 64
V_DIM = 2 * HEAD_DIM
Q_WIDTH = N_HEADS * 2 * HEAD_DIM
ATTN_WIDTH = N_HEADS * V_DIM
CHUNK = 64
CONV_WIDTH = 31
EPS = 1e-6
NEG_INF = -1e30
ALIBI_SLOPES = tuple(2.0 ** (-8.0 * (h + 1) / N_HEADS) for h in range(N_HEADS))
LOG2E = math.log2(math.e)

V7X_LANES = 128
V7X_SUBLANES = 8
V7X_VMEM_BYTES = 64 * 1024 * 1024
VMEM_LIMIT_BYTES = V7X_VMEM_BYTES - 8 * 1024 * 1024

QUERY_BLOCK = 256
CONV_ROW_BLOCK = 128
HALO_ROWS = 32

F32 = jnp.float32
BF16 = jnp.bfloat16


def _dot(a, b):
    return jnp.dot(a, b, preferred_element_type=F32)


def _dot_nt(a, b):
    return lax.dot_general(a, b, (((1,), (1,)), ((), ())), preferred_element_type=F32)


def _rms_rows(x, g):
    return x * lax.rsqrt(jnp.mean(x * x, axis=-1, keepdims=True) + EPS) * g


def _params(n_grid_axes, flags=None):
    return pltpu.CompilerParams(
        dimension_semantics=("arbitrary",) * n_grid_axes,
        vmem_limit_bytes=VMEM_LIMIT_BYTES,
        flags=flags,
    )


def _const_spec(shape):
    nd = len(shape)
    return pl.BlockSpec(shape, lambda *_: (0,) * nd, pipeline_mode=pl.Buffered(1))


def _row_tile(n_rows, want):
    t = min(want, n_rows)
    assert n_rows % t == 0, (n_rows, t)
    return t


def _ffn_kernel(*refs, has_mix, has_final):
    it = iter(refs)
    x_ref = next(it)
    if has_mix:
        attn_ref, conv_ref, woa_ref, wob_ref = next(it), next(it), next(it), next(it)
    g_ref, wg_ref, wu_ref, wd_ref = next(it), next(it), next(it), next(it)
    gf_ref = next(it) if has_final else None
    out_ref = next(it)

    x = x_ref[...]
    if has_mix:
        x = x + _dot(attn_ref[...], woa_ref[...]) + _dot(conv_ref[...], wob_ref[...])
    h = _rms_rows(x, g_ref[...]).astype(BF16)
    a = _dot(h, wg_ref[...])
    b = _dot(h, wu_ref[...])
    act = (a * jax.nn.sigmoid(a) * b).astype(BF16)
    y = x + 0.5 * _dot(act, wd_ref[...])
    if has_final:
        y = _rms_rows(y, gf_ref[...])
    out_ref[...] = y


def _ffn(x, g, wg, wu, wd, mix=None, g_final=None, name="ffn"):
    T, D = x.shape
    F = wg.shape[1]
    tm = _row_tile(T, 512)
    row = lambda w: pl.BlockSpec((tm, w), lambda i: (i, 0))
    args, specs = [x], [row(D)]
    if mix is not None:
        attn, conv, woa, wob = mix
        args += [attn, conv, woa, wob]
        specs += [row(attn.shape[1]), row(conv.shape[1]), _const_spec(woa.shape), _const_spec(wob.shape)]
    args += [g, wg, wu, wd]
    specs += [_const_spec((1, D)), _const_spec((D, F)), _const_spec((D, F)), _const_spec((F, D))]
    if g_final is not None:
        args.append(g_final)
        specs.append(_const_spec((1, D)))
    return pl.pallas_call(
        functools.partial(_ffn_kernel, has_mix=mix is not None, has_final=g_final is not None),
        grid=(T // tm,),
        in_specs=specs,
        out_specs=row(D),
        out_shape=jax.ShapeDtypeStruct((T, D), F32),
        compiler_params=_params(1),
        name=name,
    )(*args)


def _group_rms(y, g, ones_bd):
    y2 = y * y
    hi = y2.astype(BF16)
    lo = (y2 - hi.astype(F32)).astype(BF16)
    ss = _dot(hi, ones_bd) + _dot(lo, ones_bd)
    return y * lax.rsqrt(ss * (1.0 / HEAD_DIM) + EPS) * g


def _alibi_query_lanes():
    c = jnp.asarray([s * LOG2E for s in ALIBI_SLOPES], F32)
    c1 = c.astype(BF16).astype(F32)
    c2 = (c - c1).astype(BF16).astype(F32)
    c3 = (c - c1 - c2).astype(BF16).astype(F32)
    pieces = jnp.stack([c1, c2, c3, c1, c2, c3], axis=1)
    return jnp.zeros((N_HEADS, V_DIM), F32).at[:, HEAD_DIM:HEAD_DIM + 6].set(pieces)


def _proj_kernel(*refs, key_tile):
    if key_tile is None:
        x_ref, g_ref, w_ref, gq_ref, gk_ref, ones_ref, qs_ref, k32_ref, v32_ref, u_ref = refs
    else:
        (x_ref, g_ref, w_ref, gq_ref, gk_ref, ones_ref, qext_ref,
         qs_ref, ks_ref, vt_ref, k32_ref, v32_ref, u_ref) = refs
    tm = x_ref.shape[0]
    h = _rms_rows(x_ref[...], g_ref[...]).astype(BF16)
    z = _dot(h, w_ref[...])
    ones_bd = ones_ref[...]
    qn = _group_rms(z[:, :Q_WIDTH], gq_ref[...], ones_bd) * (HEAD_DIM ** -0.5 * LOG2E)
    kn = _group_rms(z[:, Q_WIDTH:2 * Q_WIDTH], gk_ref[...], ones_bd)
    v = z[:, 2 * Q_WIDTH:2 * Q_WIDTH + ATTN_WIDTH]
    conv_ch = (z.shape[1] - 2 * Q_WIDTH - ATTN_WIDTH) // 2
    a = z[:, 2 * Q_WIDTH + ATTN_WIDTH:2 * Q_WIDTH + ATTN_WIDTH + conv_ch]
    gate = z[:, 2 * Q_WIDTH + ATTN_WIDTH + conv_ch:]
    k32_ref[...] = kn
    for hh in range(N_HEADS):
        v32_ref[pl.ds(hh, tm, stride=N_HEADS), :] = v[:, V_DIM * hh:V_DIM * (hh + 1)]
    u_ref[...] = a * jax.nn.sigmoid(gate)

    lane = lax.broadcasted_iota(jnp.int32, (tm, V_DIM), 1)
    low = lane < HEAD_DIM
    if key_tile is None:
        for hh in range(N_HEADS):
            qh = qn[:, V_DIM * hh:V_DIM * (hh + 1)]
            qs_ref[hh, 0] = jnp.where(low, qh, 0.0).astype(BF16)
            qs_ref[hh, 1] = jnp.where(low, 0.0, qh).astype(BF16)
        return

    row = pl.program_id(0) * tm + lax.broadcasted_iota(jnp.int32, (tm, V_DIM), 0)
    off = row % key_tile
    lo = off % 32
    in_hi = (lane >= HEAD_DIM) & (lane < HEAD_DIM + 3)
    in_lo = (lane >= HEAD_DIM + 3) & (lane < HEAD_DIM + 6)
    k_ext = jnp.where(in_hi, off - lo, jnp.where(in_lo, lo, 0)).astype(F32)
    for hh in range(N_HEADS):
        sl = slice(V_DIM * hh, V_DIM * (hh + 1))
        q_ext = qext_ref[hh:hh + 1, :]
        for x, ext, dst in ((qn[:, sl], q_ext, qs_ref), (kn[:, sl], k_ext, ks_ref)):
            dst[hh, 0] = jnp.where(low, x, ext).astype(BF16)
            dst[hh, 1] = jnp.where(low, pltpu.roll(x, HEAD_DIM, axis=1), ext).astype(BF16)
        vt_ref[hh] = v[:, sl].T.astype(BF16)


def _project(x1, g_mix, w_in, gq_t, gk_t, ones_bd, key_tile=None):
    T, D = x1.shape
    C = w_in.shape[1]
    conv_ch = (C - 2 * Q_WIDTH - ATTN_WIDTH) // 2
    tm = _row_tile(T, 512)
    row = lambda w: pl.BlockSpec((tm, w), lambda i: (i, 0))
    per_map = pl.BlockSpec((N_HEADS, 2, tm, V_DIM), lambda i: (0, 0, i, 0))
    per_map_shape = jax.ShapeDtypeStruct((N_HEADS, 2, T, V_DIM), BF16)
    args = [x1, g_mix, w_in, gq_t, gk_t, ones_bd]
    in_specs = [row(D), _const_spec((1, D)), _const_spec((D, C)), _const_spec((1, Q_WIDTH)),
                _const_spec((1, Q_WIDTH)), _const_spec((Q_WIDTH, Q_WIDTH))]
    out_specs, out_shape = [per_map], [per_map_shape]
    if key_tile is not None:
        args.append(_alibi_query_lanes())
        in_specs.append(_const_spec((N_HEADS, V_DIM)))
        out_specs += [per_map, pl.BlockSpec((N_HEADS, V_DIM, tm), lambda i: (0, 0, i))]
        out_shape += [per_map_shape, jax.ShapeDtypeStruct((N_HEADS, V_DIM, T), BF16)]
    out_specs += [row(Q_WIDTH), pl.BlockSpec((tm * N_HEADS, V_DIM), lambda i: (i, 0)), row(conv_ch)]
    out_shape += [jax.ShapeDtypeStruct(s, F32) for s in ((T, Q_WIDTH), (T * N_HEADS, V_DIM), (T, conv_ch))]
    return pl.pallas_call(
        functools.partial(_proj_kernel, key_tile=key_tile),
        grid=(T // tm,),
        in_specs=in_specs,
        out_specs=out_specs,
        out_shape=out_shape,
        compiler_params=_params(1),
        name="proj",
    )(*args)


def _diff_lambda(lam_ref, lam_init):
    lp = lam_ref[...]
    s1 = jnp.sum(lp[0:1] * lp[1:2], axis=-1, keepdims=True)
    s2 = jnp.sum(lp[2:3] * lp[3:4], axis=-1, keepdims=True)
    return jnp.exp(s1) - jnp.exp(s2) + lam_init


def _head_out(o1, o2, lam, g_sub, lam_init):
    d = o1 - lam * o2
    return _rms_rows(d, g_sub) * (1.0 - lam_init)


def _prompt_attn_kernel(qs_ref, ks_ref, vt_ref, lam_ref, gsub_ref, out_ref,
                        acc_ref, m_ref, l_ref, sa_ref, sb_ref, maxa_ref, maxb_ref, p_ref, a_ref,
                        *, tq, tk, lam_init):
    hh = pl.program_id(0)
    qi = pl.program_id(1)
    cols = 2 * tq
    slope2 = jnp.float32(0.0)
    for h_static, sl in enumerate(ALIBI_SLOPES):
        slope2 = jnp.where(hh == h_static, jnp.float32(sl * LOG2E), slope2)

    acc_ref[...] = jnp.zeros(acc_ref.shape, F32)
    m_ref[...] = jnp.full(m_ref.shape, NEG_INF, F32)
    l_ref[...] = jnp.zeros(l_ref.shape, F32)

    q_start = qi * tq
    n_full = q_start // tk

    def tile_start(j):
        return pl.multiple_of(j * tk, tk)

    n_blocks = cols // QUERY_BLOCK

    def lanes(c):
        return slice(c * QUERY_BLOCK, (c + 1) * QUERY_BLOCK)

    def scores(j, buf, c):
        s_ref, max_ref = buf
        which, first = divmod(c * QUERY_BLOCK, tq)
        q = qs_ref[which, first:first + QUERY_BLOCK, :]
        s = _dot_nt(ks_ref[which, pl.ds(tile_start(j), tk), :], q)
        s_ref[:, lanes(c)] = s
        max_ref[:, lanes(c)] = jnp.max(s, axis=0, keepdims=True)

    def softmax_pv(j, buf, c, n_keys=None):
        s_ref, max_ref = buf
        if n_keys is None:
            keys, tile_max = slice(0, tk), max_ref[:, lanes(c)]
        else:
            keys = slice(0, n_keys)
            tile_max = jnp.max(s_ref[keys, lanes(c)], axis=0, keepdims=True)
        t_off = slope2 * (j * tk - q_start).astype(F32)
        m = m_ref[:, lanes(c)]
        m_new = jnp.maximum(m, tile_max + t_off)
        alpha = jnp.exp2(m - m_new)
        p = jnp.exp2(s_ref[keys, lanes(c)] - (m_new - t_off))
        l_ref[:, lanes(c)] = alpha * l_ref[:, lanes(c)] + jnp.sum(p, axis=0, keepdims=True)
        m_ref[:, lanes(c)] = m_new
        p_ref[keys, lanes(c)] = p.astype(BF16)
        a_ref[:, lanes(c)] = alpha
        vt = vt_ref[:, pl.ds(tile_start(j), keys.stop)]
        acc_ref[:, lanes(c)] = a_ref[:, lanes(c)] * acc_ref[:, lanes(c)] + _dot(vt, p_ref[keys, lanes(c)])

    def mask_own(buf, c, first_key):
        s_ref = buf[0]
        keys = slice(first_key, first_key + QUERY_BLOCK)
        ko = lax.broadcasted_iota(jnp.int32, (QUERY_BLOCK, QUERY_BLOCK), 0)
        qo = lax.broadcasted_iota(jnp.int32, (QUERY_BLOCK, QUERY_BLOCK), 1)
        after = (-2.0 * slope2) * jnp.maximum(ko - qo, 0).astype(F32)
        visible = (ko // CHUNK) <= (qo // CHUNK)
        s_ref[keys, lanes(c)] = jnp.where(visible, s_ref[keys, lanes(c)] + after, NEG_INF)

    even = (sa_ref, maxa_ref)
    odd = (sb_ref, maxb_ref)
    for c in range(n_blocks):
        scores(0, even, c)

    def step(j, cur, other):
        for c in range(n_blocks):
            scores(j + 1, other, c)
            softmax_pv(j, cur, c)

    def body(j, carry):
        @pl.when(j % 2 == 0)
        def _():
            step(j, even, odd)

        @pl.when(j % 2 == 1)
        def _():
            step(j, odd, even)

        return carry

    lax.fori_loop(0, n_full, body, 0)

    def last(cur):
        for sub in range(tk // tq):
            @pl.when(q_start - n_full * tk == sub * tq)
            def _(sub=sub):
                for c in range(n_blocks):
                    own = sub * tq + (c * QUERY_BLOCK) % tq
                    mask_own(cur, c, own)
                    softmax_pv(n_full, cur, c, own + QUERY_BLOCK)

    @pl.when(n_full % 2 == 0)
    def _():
        last(even)

    @pl.when(n_full % 2 == 1)
    def _():
        last(odd)

    ot = acc_ref[...] * (1.0 / l_ref[...])
    lam = _diff_lambda(lam_ref, lam_init)
    dt = ot[:, :tq] - lam * ot[:, tq:]
    yt = dt * lax.rsqrt(jnp.mean(dt * dt, axis=0, keepdims=True) + EPS)
    out_ref[...] = (yt.T * gsub_ref[...] * (1.0 - lam_init)).astype(out_ref.dtype)


def _prompt_tiles(T):
    tk = _row_tile(T, 1024)
    tq = _row_tile(tk, 1024)
    assert tq % QUERY_BLOCK == 0 and QUERY_BLOCK % CHUNK == 0
    assert tk // 32 <= 256
    return tq, tk


def _prompt_attention(qs, ks, vt, lam_rows, g_sub, lam_init, tq, tk):
    T = ks.shape[2]
    return pl.pallas_call(
        functools.partial(_prompt_attn_kernel, tq=tq, tk=tk, lam_init=lam_init),
        grid=(N_HEADS, T // tq),
        in_specs=[
            pl.BlockSpec((None, 2, tq, V_DIM), lambda h, i: (h, 0, i, 0)),
            pl.BlockSpec((None, 2, T, V_DIM), lambda h, i: (h, 0, 0, 0), pipeline_mode=pl.Buffered(1)),
            pl.BlockSpec((None, V_DIM, T), lambda h, i: (h, 0, 0), pipeline_mode=pl.Buffered(1)),
            pl.BlockSpec((4, HEAD_DIM), lambda h, i: (0, 0)),
            pl.BlockSpec((1, V_DIM), lambda h, i: (0, 0)),
        ],
        out_specs=pl.BlockSpec((tq, V_DIM), lambda h, i: (i, h)),
        out_shape=jax.ShapeDtypeStruct((T, ATTN_WIDTH), BF16),
        scratch_shapes=[
            pltpu.VMEM((V_DIM, 2 * tq), F32),
            pltpu.VMEM((1, 2 * tq), F32),
            pltpu.VMEM((1, 2 * tq), F32),
            pltpu.VMEM((tk, 2 * tq), F32),
            pltpu.VMEM((tk, 2 * tq), F32),
            pltpu.VMEM((1, 2 * tq), F32),
            pltpu.VMEM((1, 2 * tq), F32),
            pltpu.VMEM((tk, 2 * tq), BF16),
            pltpu.VMEM((1, 2 * tq), F32),
        ],
        compiler_params=_params(2),
        name="prompt_attn",
    )(qs, ks, vt, lam_rows, g_sub)


def _sample_attn_kernel(qs_ref, kn_ref, vn_ref, ckt_ref, cv_ref, lam_ref, gsub_ref, out_ref,
                        m_ref, l_ref, acc_ref, *, t_new, past, tk, lam_init):
    grp = 2 * t_new
    rows = N_HEADS * grp
    blocks = []
    for hh in range(N_HEADS):
        qh = qs_ref[hh].reshape(grp, V_DIM)
        z = jnp.zeros((grp, V_DIM), BF16)
        blocks.append(jnp.concatenate([qh if c == hh else z for c in range(N_HEADS)], axis=1))
    q = jnp.concatenate(blocks, axis=0)

    r = lax.broadcasted_iota(jnp.int32, (rows, 1), 0)
    q_pos = past + (r % t_new)
    head = r // grp
    slope2 = jnp.zeros((rows, 1), F32)
    for h_static, sl in enumerate(ALIBI_SLOPES):
        slope2 = jnp.where(head == h_static, jnp.float32(sl * LOG2E), slope2)

    m_ref[...] = jnp.full(m_ref.shape, NEG_INF, F32)
    l_ref[...] = jnp.zeros(l_ref.shape, F32)
    acc_ref[...] = jnp.zeros(acc_ref.shape, F32)

    def step(qk, head_values, k_start, n_keys):
        k_pos = k_start + lax.broadcasted_iota(jnp.int32, (1, n_keys), 1)
        bias = -slope2 * jnp.abs(q_pos - k_pos).astype(F32)
        visible = (k_pos // CHUNK) <= (q_pos // CHUNK)
        s = jnp.where(visible, qk + bias, NEG_INF)
        m_prev = m_ref[...]
        m_new = jnp.maximum(m_prev, jnp.max(s, axis=1, keepdims=True))
        alpha = jnp.exp2(m_prev - m_new)
        p = jnp.exp2(s - m_new)
        l_ref[...] = alpha * l_ref[...] + jnp.sum(p, axis=1, keepdims=True)
        m_ref[...] = m_new
        p = p.astype(BF16)
        for hh in range(N_HEADS):
            mine = slice(hh * grp, (hh + 1) * grp)
            acc_ref[mine, :] = alpha[mine] * acc_ref[mine, :] + _dot(p[mine], head_values(hh).astype(BF16))

    for c in range(past // tk):
        step(_dot(q, ckt_ref[:, c * tk:(c + 1) * tk].astype(BF16)),
             lambda hh, c=c: cv_ref[pl.ds(c * tk * N_HEADS + hh, tk, stride=N_HEADS), :],
             c * tk, tk)
    step(_dot_nt(q, kn_ref[...].astype(BF16)), lambda hh: vn_ref[pl.ds(hh, t_new, stride=N_HEADS), :],
         past, t_new)

    o = acc_ref[...] / l_ref[...]
    lam = _diff_lambda(lam_ref, lam_init)
    outs = []
    for hh in range(N_HEADS):
        o1 = o[hh * grp:hh * grp + t_new]
        o2 = o[hh * grp + t_new:(hh + 1) * grp]
        outs.append(_head_out(o1, o2, lam, gsub_ref[...], lam_init))
    out_ref[...] = jnp.concatenate(outs, axis=1).astype(out_ref.dtype)


def _sample_attention(qs, k_new, v_new, cache_kt, cache_v, lam_rows, g_sub, lam_init, n_streams):
    t_new = k_new.shape[0] // n_streams
    past = cache_kt.shape[2]
    tk = _row_tile(past, 1024)
    rows = N_HEADS * 2 * t_new
    return pl.pallas_call(
        functools.partial(_sample_attn_kernel, t_new=t_new, past=past, tk=tk, lam_init=lam_init),
        grid=(n_streams,),
        in_specs=[
            pl.BlockSpec((N_HEADS, 2, t_new, V_DIM), lambda b: (0, 0, b, 0)),
            pl.BlockSpec((t_new, Q_WIDTH), lambda b: (b, 0)),
            pl.BlockSpec((t_new * N_HEADS, V_DIM), lambda b: (b, 0)),
            pl.BlockSpec((None, Q_WIDTH, past), lambda b: (b, 0, 0)),
            pl.BlockSpec((None, past * N_HEADS, V_DIM), lambda b: (b, 0, 0)),
            pl.BlockSpec((4, HEAD_DIM), lambda b: (0, 0)),
            pl.BlockSpec((1, V_DIM), lambda b: (0, 0)),
        ],
        out_specs=pl.BlockSpec((t_new, ATTN_WIDTH), lambda b: (b, 0)),
        out_shape=jax.ShapeDtypeStruct((n_streams * t_new, ATTN_WIDTH), BF16),
        scratch_shapes=[
            pltpu.VMEM((rows, 1), F32),
            pltpu.VMEM((rows, 1), F32),
            pltpu.VMEM((rows, V_DIM), F32),
        ],
        compiler_params=_params(1),
        name="sample_attn",
    )(qs, k_new, v_new, cache_kt, cache_v, lam_rows, g_sub)


def _conv_kernel(u_ref, halo_ref, init_ref, w_ref, b_ref, g_ref, beta_ref, out_ref,
                 xp_ref, y_ref, win_ref, *, tm):
    i = pl.program_id(1)
    hist = jnp.where(i == 0, init_ref[...], halo_ref[...])
    xp_ref[0:HALO_ROWS, :] = hist
    xp_ref[HALO_ROWS:HALO_ROWS + tm, :] = u_ref[...]
    first = HALO_ROWS - (CONV_WIDTH - 1)
    rows = min(tm, CONV_ROW_BLOCK)
    for r0 in range(0, tm, rows):
        for c0 in range(0, u_ref.shape[-1], V7X_LANES):
            ch = slice(c0, c0 + V7X_LANES)
            acc = jnp.zeros((rows, V7X_LANES), F32) + b_ref[:, ch]
            for phase in range(V7X_SUBLANES):
                n_taps = (CONV_WIDTH - 1 - phase) // V7X_SUBLANES + 1
                start = first + phase + r0
                n_win = rows + V7X_SUBLANES * (n_taps - 1)
                win_ref[0:n_win, :] = xp_ref[start:start + n_win, ch]
                win = win_ref[0:n_win, :]
                for a in range(n_taps):
                    w = V7X_SUBLANES * a + phase
                    acc = acc + win[V7X_SUBLANES * a:V7X_SUBLANES * a + rows] * w_ref[w:w + 1, ch]
            y_ref[r0:r0 + rows, ch] = acc
    y = y_ref[...]
    mu = jnp.mean(y, axis=-1, keepdims=True)
    d = y - mu
    var = jnp.mean(d * d, axis=-1, keepdims=True)
    yn = d * lax.rsqrt(var + EPS) * g_ref[...] + beta_ref[...]
    out_ref[...] = (yn * jax.nn.sigmoid(yn)).astype(out_ref.dtype)


def _conv_branch(u, init_hist, w_dw, b_dw, g_ln, b_ln):
    B, T, C = u.shape
    tm = _row_tile(T, 256)
    assert tm % HALO_ROWS == 0
    per = tm // HALO_ROWS
    vec = lambda: pl.BlockSpec((1, C), lambda b, i: (0, 0))
    return pl.pallas_call(
        functools.partial(_conv_kernel, tm=tm),
        grid=(B, T // tm),
        in_specs=[
            pl.BlockSpec((None, tm, C), lambda b, i: (b, i, 0)),
            pl.BlockSpec((None, HALO_ROWS, C), lambda b, i: (b, jnp.maximum(i * per - 1, 0), 0)),
            pl.BlockSpec((None, HALO_ROWS, C), lambda b, i: (b, 0, 0)),
            pl.BlockSpec((CONV_WIDTH, C), lambda b, i: (0, 0)),
            vec(), vec(), vec(),
        ],
        out_specs=pl.BlockSpec((None, tm, C), lambda b, i: (b, i, 0)),
        out_shape=jax.ShapeDtypeStruct((B, T, C), BF16),
        scratch_shapes=[
            pltpu.VMEM((HALO_ROWS + tm, C), F32),
            pltpu.VMEM((tm, C), F32),
            pltpu.VMEM((min(tm, CONV_ROW_BLOCK) + HALO_ROWS, V7X_LANES), F32),
        ],
        compiler_params=_params(2),
        name="conv",
    )(u, u, init_hist, w_dw, b_dw, g_ln, b_ln)


def _layer(x, n_streams, attend, key_tile, conv_hist, lam_init, p):
    T = x.shape[0] // n_streams
    x1 = _ffn(x, p["g_ffn1"], p["w1g"], p["w1u"], p["w1d"], name="ffn1")
    proj = _project(x1, p["g_mix"], p["w_in"], p["gq_t"], p["gk_t"], p["ones_bd"], key_tile)
    k32, v32, u = proj[-3:]
    attn = attend(*proj)
    conv_ch = u.shape[1]
    hist = jnp.pad(conv_hist, ((0, 0), (HALO_ROWS - conv_hist.shape[1], 0), (0, 0)))
    conv = _conv_branch(u.reshape(n_streams, T, conv_ch), hist, p["w_dw"], p["b_dw"], p["g_ln"], p["b_ln"])
    y = _ffn(x1, p["g_ffn2"], p["w2g"], p["w2u"], p["w2d"],
             mix=(attn, conv.reshape(n_streams * T, conv_ch), p["wo_attn"], p["wo_conv"]),
             g_final=p["g_final"], name="ffn2")
    return y, k32, v32, u


def kernel(x_prompt, x_sample, cache_k, cache_v, cache_conv, g_ffn1, w_ffn1_gu, w_ffn1_down, g_mix, w_in, g_q, g_k, lambda_q1, lambda_k1, lambda_q2, lambda_k2, g_sub, w_dw, b_dw, g_conv_ln, b_conv_ln, w_out, g_ffn2, w_ffn2_gu, w_ffn2_down, g_final):
    depth = cache_k.shape[0]
    n_p, t_p, d_model = x_prompt.shape
    n_s, t_s, _ = x_sample.shape
    past = cache_k.shape[2]
    d_ff = w_ffn1_down.shape[1]
    conv_ch = cache_conv.shape[-1]
    hist_rows = CONV_WIDTH - 1
    assert n_p == 1, "prompt attention kernel handles one prompt stream"
    assert t_p >= hist_rows and t_s >= hist_rows

    group = jnp.arange(Q_WIDTH) // HEAD_DIM
    ones_bd = (group[:, None] == group[None, :]).astype(BF16)
    row = lambda a: a.reshape(1, -1).astype(F32)

    yp = x_prompt.reshape(n_p * t_p, d_model)
    ys = x_sample.reshape(n_s * t_s, d_model)
    outs = [[] for _ in range(6)]
    for l in range(depth):
        lam_init = 0.8 - 0.6 * math.exp(-0.3 * l)
        lam_rows = jnp.stack([lambda_q1[l], lambda_k1[l], lambda_q2[l], lambda_k2[l]]).astype(F32)
        p = dict(
            g_ffn1=row(g_ffn1[l]), w1g=w_ffn1_gu[l][:, :d_ff].astype(BF16), w1u=w_ffn1_gu[l][:, d_ff:].astype(BF16),
            w1d=w_ffn1_down[l].astype(BF16), g_mix=row(g_mix[l]), w_in=w_in[l].astype(BF16),
            gq_t=row(jnp.tile(g_q[l], 2 * N_HEADS)), gk_t=row(jnp.tile(g_k[l], 2 * N_HEADS)), ones_bd=ones_bd,
            w_dw=w_dw[l].astype(F32), b_dw=row(b_dw[l]), g_ln=row(g_conv_ln[l]), b_ln=row(b_conv_ln[l]),
            wo_attn=w_out[l][:ATTN_WIDTH].astype(BF16), wo_conv=w_out[l][ATTN_WIDTH:].astype(BF16),
            g_ffn2=row(g_ffn2[l]), w2g=w_ffn2_gu[l][:, :d_ff].astype(BF16), w2u=w_ffn2_gu[l][:, d_ff:].astype(BF16),
            w2d=w_ffn2_down[l].astype(BF16), g_final=row(g_final[l]),
        )
        g_sub_row = row(g_sub[l])

        q_tile, key_tile = _prompt_tiles(t_p)

        def attend_prompt(qs, ks, vt, k32, v32, u):
            return _prompt_attention(qs, ks, vt, lam_rows, g_sub_row, lam_init, q_tile, key_tile)

        def attend_sample(qs, k32, v32, u, l=l):
            ck = jnp.swapaxes(cache_k[l].reshape(n_s, past, Q_WIDTH), 1, 2)
            cv = cache_v[l].reshape(n_s, past * N_HEADS, V_DIM)
            return _sample_attention(qs, k32, v32, ck, cv, lam_rows, g_sub_row, lam_init, n_s)

        zero_hist = jnp.zeros((n_p, hist_rows, conv_ch), F32)
        yp, kp, vp, up = _layer(yp, n_p, attend_prompt, key_tile, zero_hist, lam_init, p)
        ys, ks, vs, us = _layer(ys, n_s, attend_sample, None, cache_conv[l], lam_init, p)
        outs[0].append(kp.reshape(n_p, t_p, N_HEADS, 2, HEAD_DIM))
        outs[1].append(vp.reshape(n_p, t_p, N_HEADS, V_DIM))
        outs[2].append(up.reshape(n_p, t_p, conv_ch)[:, t_p - hist_rows:])
        outs[3].append(ks.reshape(n_s, t_s, N_HEADS, 2, HEAD_DIM))
        outs[4].append(vs.reshape(n_s, t_s, N_HEADS, V_DIM))
        outs[5].append(us.reshape(n_s, t_s, conv_ch)[:, t_s - hist_rows:])

    return (yp.reshape(n_p, t_p, d_model), ys.reshape(n_s, t_s, d_model),
            *[jnp.stack(o, axis=0) for o in outs])
```

```python
import functools
import math

import jax
import jax.numpy as jnp
from jax import lax
from jax.experimental import pallas as pl
from jax.experimental.pallas import tpu as pltpu

N_HEADS = 4
HEAD_DIM = 64
V_DIM = 2 * HEAD_DIM
Q_WIDTH = N_HEADS * 2 * HEAD_DIM
ATTN_WIDTH = N_HEADS * V_DIM
CHUNK = 64
CONV_WIDTH = 31
EPS = 1e-6
NEG_INF = -1e30
ALIBI_SLOPES = tuple(2.0 ** (-8.0 * (h + 1) / N_HEADS) for h in range(N_HEADS))
LOG2E = math.log2(math.e)

V7X_LANES = 128
V7X_SUBLANES = 8
V7X_VMEM_BYTES = 64 * 1024 * 1024
VMEM_LIMIT_BYTES = V7X_VMEM_BYTES - 8 * 1024 * 1024

QUERY_BLOCK = 256
CONV_ROW_BLOCK = 128
HALO_ROWS = 32

F32 = jnp.float32
BF16 = jnp.bfloat16


def _dot(a, b):
    return jnp.dot(a, b, preferred_element_type=F32)


def _dot_nt(a, b):
    return lax.dot_general(a, b, (((1,), (1,)), ((), ())), preferred_element_type=F32)


def _rms_rows(x, g):
    return x * lax.rsqrt(jnp.mean(x * x, axis=-1, keepdims=True) + EPS) * g


def _params(n_grid_axes, flags=None):
    return pltpu.CompilerParams(
        dimension_semantics=("arbitrary",) * n_grid_axes,
        vmem_limit_bytes=VMEM_LIMIT_BYTES,
        flags=flags,
    )


def _const_spec(shape):
    nd = len(shape)
    return pl.BlockSpec(shape, lambda *_: (0,) * nd, pipeline_mode=pl.Buffered(1))


def _row_tile(n_rows, want):
    t = min(want, n_rows)
    assert n_rows % t == 0, (n_rows, t)
    return t


def _ffn_kernel(*refs, has_mix, has_final):
    it = iter(refs)
    x_ref = next(it)
    if has_mix:
        attn_ref, conv_ref, woa_ref, wob_ref = next(it), next(it), next(it), next(it)
    g_ref, wg_ref, wu_ref, wd_ref = next(it), next(it), next(it), next(it)
    gf_ref = next(it) if has_final else None
    out_ref = next(it)

    x = x_ref[...]
    if has_mix:
        x = x + _dot(attn_ref[...], woa_ref[...]) + _dot(conv_ref[...], wob_ref[...])
    h = _rms_rows(x, g_ref[...]).astype(BF16)
    a = _dot(h, wg_ref[...])
    b = _dot(h, wu_ref[...])
    act = (a * jax.nn.sigmoid(a) * b).astype(BF16)
    y = x + 0.5 * _dot(act, wd_ref[...])
    if has_final:
        y = _rms_rows(y, gf_ref[...])
    out_ref[...] = y


def _ffn(x, g, wg, wu, wd, mix=None, g_final=None, name="ffn"):
    T, D = x.shape
    F = wg.shape[1]
    tm = _row_tile(T, 512)
    row = lambda w: pl.BlockSpec((tm, w), lambda i: (i, 0))
    args, specs = [x], [row(D)]
    if mix is not None:
        attn, conv, woa, wob = mix
        args += [attn, conv, woa, wob]
        specs += [row(attn.shape[1]), row(conv.shape[1]), _const_spec(woa.shape), _const_spec(wob.shape)]
    args += [g, wg, wu, wd]
    specs += [_const_spec((1, D)), _const_spec((D, F)), _const_spec((D, F)), _const_spec((F, D))]
    if g_final is not None:
        args.append(g_final)
        specs.append(_const_spec((1, D)))
    return pl.pallas_call(
        functools.partial(_ffn_kernel, has_mix=mix is not None, has_final=g_final is not None),
        grid=(T // tm,),
        in_specs=specs,
        out_specs=row(D),
        out_shape=jax.ShapeDtypeStruct((T, D), F32),
        compiler_params=_params(1),
        name=name,
    )(*args)


def _group_rms(y, g, ones_bd):
    y2 = y * y
    hi = y2.astype(BF16)
    lo = (y2 - hi.astype(F32)).astype(BF16)
    ss = _dot(hi, ones_bd) + _dot(lo, ones_bd)
    return y * lax.rsqrt(ss * (1.0 / HEAD_DIM) + EPS) * g


def _alibi_query_lanes():
    c = jnp.asarray([s * LOG2E for s in ALIBI_SLOPES], F32)
    c1 = c.astype(BF16).astype(F32)
    c2 = (c - c1).astype(BF16).astype(F32)
    c3 = (c - c1 - c2).astype(BF16).astype(F32)
    pieces = jnp.stack([c1, c2, c3, c1, c2, c3], axis=1)
    return jnp.zeros((N_HEADS, V_DIM), F32).at[:, HEAD_DIM:HEAD_DIM + 6].set(pieces)


def _proj_kernel(*refs, key_tile):
    if key_tile is None:
        x_ref, g_ref, w_ref, gq_ref, gk_ref, ones_ref, qs_ref, k32_ref, v32_ref, u_ref = refs
    else:
        (x_ref, g_ref, w_ref, gq_ref, gk_ref, ones_ref, qext_ref,
         qs_ref, ks_ref, vt_ref, k32_ref, v32_ref, u_ref) = refs
    tm = x_ref.shape[0]
    h = _rms_rows(x_ref[...], g_ref[...]).astype(BF16)
    z = _dot(h, w_ref[...])
    ones_bd = ones_ref[...]
    qn = _group_rms(z[:, :Q_WIDTH], gq_ref[...], ones_bd) * (HEAD_DIM ** -0.5 * LOG2E)
    kn = _group_rms(z[:, Q_WIDTH:2 * Q_WIDTH], gk_ref[...], ones_bd)
    v = z[:, 2 * Q_WIDTH:2 * Q_WIDTH + ATTN_WIDTH]
    conv_ch = (z.shape[1] - 2 * Q_WIDTH - ATTN_WIDTH) // 2
    a = z[:, 2 * Q_WIDTH + ATTN_WIDTH:2 * Q_WIDTH + ATTN_WIDTH + conv_ch]
    gate = z[:, 2 * Q_WIDTH + ATTN_WIDTH + conv_ch:]
    k32_ref[...] = kn
    for hh in range(N_HEADS):
        v32_ref[pl.ds(hh, tm, stride=N_HEADS), :] = v[:, V_DIM * hh:V_DIM * (hh + 1)]
    u_ref[...] = a * jax.nn.sigmoid(gate)

    lane = lax.broadcasted_iota(jnp.int32, (tm, V_DIM), 1)
    low = lane < HEAD_DIM
    if key_tile is None:
        for hh in range(N_HEADS):
            qh = qn[:, V_DIM * hh:V_DIM * (hh + 1)]
            qs_ref[hh, 0] = jnp.where(low, qh, 0.0).astype(BF16)
            qs_ref[hh, 1] = jnp.where(low, 0.0, qh).astype(BF16)
        return

    row = pl.program_id(0) * tm + lax.broadcasted_iota(jnp.int32, (tm, V_DIM), 0)
    off = row % key_tile
    lo = off % 32
    in_hi = (lane >= HEAD_DIM) & (lane < HEAD_DIM + 3)
    in_lo = (lane >= HEAD_DIM + 3) & (lane < HEAD_DIM + 6)
    k_ext = jnp.where(in_hi, off - lo, jnp.where(in_lo, lo, 0)).astype(F32)
    for hh in range(N_HEADS):
        sl = slice(V_DIM * hh, V_DIM * (hh + 1))
        q_ext = qext_ref[hh:hh + 1, :]
        for x, ext, dst in ((qn[:, sl], q_ext, qs_ref), (kn[:, sl], k_ext, ks_ref)):
            dst[hh, 0] = jnp.where(low, x, ext).astype(BF16)
            dst[hh, 1] = jnp.where(low, pltpu.roll(x, HEAD_DIM, axis=1), ext).astype(BF16)
        vt_ref[hh] = v[:, sl].T.astype(BF16)


def _project(x1, g_mix, w_in, gq_t, gk_t, ones_bd, key_tile=None):
    T, D = x1.shape
    C = w_in.shape[1]
    conv_ch = (C - 2 * Q_WIDTH - ATTN_WIDTH) // 2
    tm = _row_tile(T, 512)
    row = lambda w: pl.BlockSpec((tm, w), lambda i: (i, 0))
    per_map = pl.BlockSpec((N_HEADS, 2, tm, V_DIM), lambda i: (0, 0, i, 0))
    per_map_shape = jax.ShapeDtypeStruct((N_HEADS, 2, T, V_DIM), BF16)
    args = [x1, g_mix, w_in, gq_t, gk_t, ones_bd]
    in_specs = [row(D), _const_spec((1, D)), _const_spec((D, C)), _const_spec((1, Q_WIDTH)),
                _const_spec((1, Q_WIDTH)), _const_spec((Q_WIDTH, Q_WIDTH))]
    out_specs, out_shape = [per_map], [per_map_shape]
    if key_tile is not None:
        args.append(_alibi_query_lanes())
        in_specs.append(_const_spec((N_HEADS, V_DIM)))
        out_specs += [per_map, pl.BlockSpec((N_HEADS, V_DIM, tm), lambda i: (0, 0, i))]
        out_shape += [per_map_shape, jax.ShapeDtypeStruct((N_HEADS, V_DIM, T), BF16)]
    out_specs += [row(Q_WIDTH), pl.BlockSpec((tm * N_HEADS, V_DIM), lambda i: (i, 0)), row(conv_ch)]
    out_shape += [jax.ShapeDtypeStruct(s, F32) for s in ((T, Q_WIDTH), (T * N_HEADS, V_DIM), (T, conv_ch))]
    return pl.pallas_call(
        functools.partial(_proj_kernel, key_tile=key_tile),
        grid=(T // tm,),
        in_specs=in_specs,
        out_specs=out_specs,
        out_shape=out_shape,
        compiler_params=_params(1),
        name="proj",
    )(*args)


def _diff_lambda(lam_ref, lam_init):
    lp = lam_ref[...]
    s1 = jnp.sum(lp[0:1] * lp[1:2], axis=-1, keepdims=True)
    s2 = jnp.sum(lp[2:3] * lp[3:4], axis=-1, keepdims=True)
    return jnp.exp(s1) - jnp.exp(s2) + lam_init


def _head_out(o1, o2, lam, g_sub, lam_init):
    d = o1 - lam * o2
    return _rms_rows(d, g_sub) * (1.0 - lam_init)


def _prompt_attn_kernel(qs_ref, qnext_ref, ks_ref, vt_ref, lam_ref, gsub_ref, out_ref,
                        acc_ref, m_ref, l_ref, sa_ref, sb_ref, maxa_ref, maxb_ref, p_ref, a_ref, base_ref,
                        *, tq, tk, lam_init):
    hh = pl.program_id(0)
    qi = pl.program_id(1)
    cols = 2 * tq
    slope2 = jnp.float32(0.0)
    for h_static, sl in enumerate(ALIBI_SLOPES):
        slope2 = jnp.where(hh == h_static, jnp.float32(sl * LOG2E), slope2)

    acc_ref[...] = jnp.zeros(acc_ref.shape, F32)
    m_ref[...] = jnp.full(m_ref.shape, NEG_INF, F32)
    l_ref[...] = jnp.zeros(l_ref.shape, F32)

    q_start = qi * tq
    n_full = q_start // tk

    def tile_start(j):
        return pl.multiple_of(j * tk, tk)

    n_blocks = cols // QUERY_BLOCK

    def lanes(c):
        return slice(c * QUERY_BLOCK, (c + 1) * QUERY_BLOCK)

    def scores(j, buf, c, q_ref=qs_ref):
        s_ref, max_ref = buf
        which, first = divmod(c * QUERY_BLOCK, tq)
        q = q_ref[which, first:first + QUERY_BLOCK, :]
        s = _dot_nt(ks_ref[which, pl.ds(tile_start(j), tk), :], q)
        s_ref[:, lanes(c)] = s
        max_ref[:, lanes(c)] = jnp.max(s, axis=0, keepdims=True)

    def softmax_pv(j, buf, c, n_keys=None):
        s_ref, max_ref = buf
        if n_keys is None:
            keys, tile_max = slice(0, tk), max_ref[:, lanes(c)]
        else:
            keys = slice(0, n_keys)
            tile_max = jnp.max(s_ref[keys, lanes(c)], axis=0, keepdims=True)
        t_off = slope2 * (j * tk - q_start).astype(F32)
        m = m_ref[:, lanes(c)]
        m_new = jnp.maximum(m, tile_max + t_off)
        alpha = jnp.exp2(m - m_new)
        p = jnp.exp2(s_ref[keys, lanes(c)] - (m_new - t_off))
        l_ref[:, lanes(c)] = alpha * l_ref[:, lanes(c)] + jnp.sum(p, axis=0, keepdims=True)
        m_ref[:, lanes(c)] = m_new
        p_ref[keys, lanes(c)] = p.astype(BF16)
        a_ref[:, lanes(c)] = alpha
        vt = vt_ref[:, pl.ds(tile_start(j), keys.stop)]
        acc_ref[:, lanes(c)] = a_ref[:, lanes(c)] * acc_ref[:, lanes(c)] + _dot(vt, p_ref[keys, lanes(c)])

    def mask_own(buf, c, first_key):
        s_ref = buf[0]
        keys = slice(first_key, first_key + QUERY_BLOCK)
        ko = lax.broadcasted_iota(jnp.int32, (QUERY_BLOCK, QUERY_BLOCK), 0)
        qo = lax.broadcasted_iota(jnp.int32, (QUERY_BLOCK, QUERY_BLOCK), 1)
        after = (-2.0 * slope2) * jnp.maximum(ko - qo, 0).astype(F32)
        visible = (ko // CHUNK) <= (qo // CHUNK)
        s_ref[keys, lanes(c)] = jnp.where(visible, s_ref[keys, lanes(c)] + after, NEG_INF)

    sets = ((sa_ref, maxa_ref), (sb_ref, maxb_ref))

    @pl.when(qi == 0)
    def _():
        base_ref[0] = 0
        for c in range(n_blocks):
            scores(0, sets[0], c)

    base = base_ref[0]

    def step(j, cur, other):
        for c in range(n_blocks):
            scores(j + 1, other, c)
            softmax_pv(j, cur, c)

    def body(j, carry):
        for parity in range(2):
            @pl.when((j + base) % 2 == parity)
            def _(parity=parity):
                step(j, sets[parity], sets[1 - parity])

        return carry

    lax.fori_loop(0, n_full, body, 0)

    def last(cur, other):
        for sub in range(tk // tq):
            @pl.when(q_start - n_full * tk == sub * tq)
            def _(sub=sub):
                for c in range(n_blocks):
                    if other is not None:
                        scores(0, other, c, qnext_ref)
                    own = sub * tq + (c * QUERY_BLOCK) % tq
                    mask_own(cur, c, own)
                    softmax_pv(n_full, cur, c, own + QUERY_BLOCK)

    has_next = qi + 1 < pl.num_programs(1)
    for parity in range(2):
        @pl.when(jnp.logical_and((n_full + base) % 2 == parity, has_next))
        def _(parity=parity):
            last(sets[parity], sets[1 - parity])
            base_ref[0] = 1 - parity

        @pl.when(jnp.logical_and((n_full + base) % 2 == parity, jnp.logical_not(has_next)))
        def _(parity=parity):
            last(sets[parity], None)

    ot = acc_ref[...] * (1.0 / l_ref[...])
    lam = _diff_lambda(lam_ref, lam_init)
    dt = ot[:, :tq] - lam * ot[:, tq:]
    yt = dt * lax.rsqrt(jnp.mean(dt * dt, axis=0, keepdims=True) + EPS)
    out_ref[...] = (yt.T * gsub_ref[...] * (1.0 - lam_init)).astype(out_ref.dtype)


def _prompt_tiles(T):
    tk = _row_tile(T, 1024)
    tq = _row_tile(tk, 1024)
    assert tq % QUERY_BLOCK == 0 and QUERY_BLOCK % CHUNK == 0
    assert tk // 32 <= 256
    return tq, tk


def _prompt_attention(qs, ks, vt, lam_rows, g_sub, lam_init, tq, tk):
    T = ks.shape[2]
    n_q = T // tq
    return pl.pallas_call(
        functools.partial(_prompt_attn_kernel, tq=tq, tk=tk, lam_init=lam_init),
        grid=(N_HEADS, n_q),
        in_specs=[
            pl.BlockSpec((None, 2, tq, V_DIM), lambda h, i: (h, 0, i, 0)),
            pl.BlockSpec((None, 2, tq, V_DIM), lambda h, i: (h, 0, jnp.minimum(i + 1, n_q - 1), 0)),
            pl.BlockSpec((None, 2, T, V_DIM), lambda h, i: (h, 0, 0, 0), pipeline_mode=pl.Buffered(1)),
            pl.BlockSpec((None, V_DIM, T), lambda h, i: (h, 0, 0), pipeline_mode=pl.Buffered(1)),
            pl.BlockSpec((4, HEAD_DIM), lambda h, i: (0, 0)),
            pl.BlockSpec((1, V_DIM), lambda h, i: (0, 0)),
        ],
        out_specs=pl.BlockSpec((tq, V_DIM), lambda h, i: (i, h)),
        out_shape=jax.ShapeDtypeStruct((T, ATTN_WIDTH), BF16),
        scratch_shapes=[
            pltpu.VMEM((V_DIM, 2 * tq), F32),
            pltpu.VMEM((1, 2 * tq), F32),
            pltpu.VMEM((1, 2 * tq), F32),
            pltpu.VMEM((tk, 2 * tq), F32),
            pltpu.VMEM((tk, 2 * tq), F32),
            pltpu.VMEM((1, 2 * tq), F32),
            pltpu.VMEM((1, 2 * tq), F32),
            pltpu.VMEM((tk, 2 * tq), BF16),
            pltpu.VMEM((1, 2 * tq), F32),
            pltpu.SMEM((1,), jnp.int32),
        ],
        compiler_params=_params(2),
        name="prompt_attn",
    )(qs, qs, ks, vt, lam_rows, g_sub)


def _sample_attn_kernel(qs_ref, kn_ref, vn_ref, ckt_ref, cv_ref, lam_ref, gsub_ref, out_ref,
                        m_ref, l_ref, acc_ref, *, t_new, past, tk, lam_init):
    grp = 2 * t_new
    rows = N_HEADS * grp
    blocks = []
    for hh in range(N_HEADS):
        qh = qs_ref[hh].reshape(grp, V_DIM)
        z = jnp.zeros((grp, V_DIM), BF16)
        blocks.append(jnp.concatenate([qh if c == hh else z for c in range(N_HEADS)], axis=1))
    q = jnp.concatenate(blocks, axis=0)

    r = lax.broadcasted_iota(jnp.int32, (rows, 1), 0)
    q_pos = past + (r % t_new)
    head = r // grp
    slope2 = jnp.zeros((rows, 1), F32)
    for h_static, sl in enumerate(ALIBI_SLOPES):
        slope2 = jnp.where(head == h_static, jnp.float32(sl * LOG2E), slope2)

    m_ref[...] = jnp.full(m_ref.shape, NEG_INF, F32)
    l_ref[...] = jnp.zeros(l_ref.shape, F32)
    acc_ref[...] = jnp.zeros(acc_ref.shape, F32)

    def step(qk, head_values, k_start, n_keys):
        k_pos = k_start + lax.broadcasted_iota(jnp.int32, (1, n_keys), 1)
        bias = -slope2 * jnp.abs(q_pos - k_pos).astype(F32)
        visible = (k_pos // CHUNK) <= (q_pos // CHUNK)
        s = jnp.where(visible, qk + bias, NEG_INF)
        m_prev = m_ref[...]
        m_new = jnp.maximum(m_prev, jnp.max(s, axis=1, keepdims=True))
        alpha = jnp.exp2(m_prev - m_new)
        p = jnp.exp2(s - m_new)
        l_ref[...] = alpha * l_ref[...] + jnp.sum(p, axis=1, keepdims=True)
        m_ref[...] = m_new
        p = p.astype(BF16)
        for hh in range(N_HEADS):
            mine = slice(hh * grp, (hh + 1) * grp)
            acc_ref[mine, :] = alpha[mine] * acc_ref[mine, :] + _dot(p[mine], head_values(hh).astype(BF16))

    for c in range(past // tk):
        step(_dot(q, ckt_ref[:, c * tk:(c + 1) * tk].astype(BF16)),
             lambda hh, c=c: cv_ref[pl.ds(c * tk * N_HEADS + hh, tk, stride=N_HEADS), :],
             c * tk, tk)
    step(_dot_nt(q, kn_ref[...].astype(BF16)), lambda hh: vn_ref[pl.ds(hh, t_new, stride=N_HEADS), :],
         past, t_new)

    o = acc_ref[...] / l_ref[...]
    lam = _diff_lambda(lam_ref, lam_init)
    outs = []
    for hh in range(N_HEADS):
        o1 = o[hh * grp:hh * grp + t_new]
        o2 = o[hh * grp + t_new:(hh + 1) * grp]
        outs.append(_head_out(o1, o2, lam, gsub_ref[...], lam_init))
    out_ref[...] = jnp.concatenate(outs, axis=1).astype(out_ref.dtype)


def _sample_attention(qs, k_new, v_new, cache_kt, cache_v, lam_rows, g_sub, lam_init, n_streams):
    t_new = k_new.shape[0] // n_streams
    past = cache_kt.shape[2]
    tk = _row_tile(past, 1024)
    rows = N_HEADS * 2 * t_new
    return pl.pallas_call(
        functools.partial(_sample_attn_kernel, t_new=t_new, past=past, tk=tk, lam_init=lam_init),
        grid=(n_streams,),
        in_specs=[
            pl.BlockSpec((N_HEADS, 2, t_new, V_DIM), lambda b: (0, 0, b, 0)),
            pl.BlockSpec((t_new, Q_WIDTH), lambda b: (b, 0)),
            pl.BlockSpec((t_new * N_HEADS, V_DIM), lambda b: (b, 0)),
            pl.BlockSpec((None, Q_WIDTH, past), lambda b: (b, 0, 0)),
            pl.BlockSpec((None, past * N_HEADS, V_DIM), lambda b: (b, 0, 0)),
            pl.BlockSpec((4, HEAD_DIM), lambda b: (0, 0)),
            pl.BlockSpec((1, V_DIM), lambda b: (0, 0)),
        ],
        out_specs=pl.BlockSpec((t_new, ATTN_WIDTH), lambda b: (b, 0)),
        out_shape=jax.ShapeDtypeStruct((n_streams * t_new, ATTN_WIDTH), BF16),
        scratch_shapes=[
            pltpu.VMEM((rows, 1), F32),
            pltpu.VMEM((rows, 1), F32),
            pltpu.VMEM((rows, V_DIM), F32),
        ],
        compiler_params=_params(1),
        name="sample_attn",
    )(qs, k_new, v_new, cache_kt, cache_v, lam_rows, g_sub)


def _conv_kernel(u_ref, halo_ref, init_ref, w_ref, b_ref, g_ref, beta_ref, out_ref,
                 xp_ref, shift_ref, y_ref, *, tm):
    i = pl.program_id(1)
    hist = jnp.where(i == 0, init_ref[...], halo_ref[...])
    xp_ref[0:HALO_ROWS, :] = hist
    xp_ref[HALO_ROWS:HALO_ROWS + tm, :] = u_ref[...]
    first = HALO_ROWS - (CONV_WIDTH - 1)
    n_lane_blocks = u_ref.shape[-1] // V7X_LANES
    taps_of = lambda phase: (CONV_WIDTH - 1 - phase) // V7X_SUBLANES + 1

    for phase in range(V7X_SUBLANES):
        n_win = tm + V7X_SUBLANES * (taps_of(phase) - 1)
        for c in range(n_lane_blocks):
            shift_ref[phase, c, 0:n_win, :] = xp_ref[first + phase:first + phase + n_win,
                                                     c * V7X_LANES:(c + 1) * V7X_LANES]

    rows = min(tm, CONV_ROW_BLOCK)
    n_row_blocks = tm // rows

    def block(t, carry):
        c = t // n_row_blocks
        r0 = pl.multiple_of((t % n_row_blocks) * rows, rows)
        acc = jnp.zeros((rows, V7X_LANES), F32) + b_ref[c]
        for phase in range(V7X_SUBLANES):
            for a in range(taps_of(phase)):
                w = V7X_SUBLANES * a + phase
                acc = acc + shift_ref[phase, c, pl.ds(r0 + V7X_SUBLANES * a, rows), :] * w_ref[c, w:w + 1, :]
        y_ref[c, pl.ds(r0, rows), :] = acc
        return carry

    lax.fori_loop(0, n_lane_blocks * n_row_blocks, block, 0)
    y = jnp.concatenate([y_ref[c] for c in range(n_lane_blocks)], axis=1)
    mu = jnp.mean(y, axis=-1, keepdims=True)
    d = y - mu
    var = jnp.mean(d * d, axis=-1, keepdims=True)
    yn = d * lax.rsqrt(var + EPS) * g_ref[...] + beta_ref[...]
    out_ref[...] = (yn * jax.nn.sigmoid(yn)).astype(out_ref.dtype)


def _conv_branch(u, init_hist, w_dw, b_dw, g_ln, b_ln):
    B, T, C = u.shape
    tm = _row_tile(T, 256)
    assert tm % HALO_ROWS == 0
    per = tm // HALO_ROWS
    n_lane_blocks = C // V7X_LANES
    vec = lambda: pl.BlockSpec((1, C), lambda b, i: (0, 0))
    w_blocks = w_dw.reshape(CONV_WIDTH, n_lane_blocks, V7X_LANES).swapaxes(0, 1)
    b_blocks = b_dw.reshape(n_lane_blocks, 1, V7X_LANES)
    return pl.pallas_call(
        functools.partial(_conv_kernel, tm=tm),
        grid=(B, T // tm),
        in_specs=[
            pl.BlockSpec((None, tm, C), lambda b, i: (b, i, 0)),
            pl.BlockSpec((None, HALO_ROWS, C), lambda b, i: (b, jnp.maximum(i * per - 1, 0), 0)),
            pl.BlockSpec((None, HALO_ROWS, C), lambda b, i: (b, 0, 0)),
            pl.BlockSpec((n_lane_blocks, CONV_WIDTH, V7X_LANES), lambda b, i: (0, 0, 0)),
            pl.BlockSpec((n_lane_blocks, 1, V7X_LANES), lambda b, i: (0, 0, 0)),
            vec(), vec(),
        ],
        out_specs=pl.BlockSpec((None, tm, C), lambda b, i: (b, i, 0)),
        out_shape=jax.ShapeDtypeStruct((B, T, C), BF16),
        scratch_shapes=[
            pltpu.VMEM((HALO_ROWS + tm, C), F32),
            pltpu.VMEM((V7X_SUBLANES, n_lane_blocks, tm + HALO_ROWS, V7X_LANES), F32),
            pltpu.VMEM((n_lane_blocks, tm, V7X_LANES), F32),
        ],
        compiler_params=_params(2),
        name="conv",
    )(u, u, init_hist, w_blocks, b_blocks, g_ln, b_ln)


def _layer(x, n_streams, attend, key_tile, conv_hist, lam_init, p):
    T = x.shape[0] // n_streams
    x1 = _ffn(x, p["g_ffn1"], p["w1g"], p["w1u"], p["w1d"], name="ffn1")
    proj = _project(x1, p["g_mix"], p["w_in"], p["gq_t"], p["gk_t"], p["ones_bd"], key_tile)
    k32, v32, u = proj[-3:]
    attn = attend(*proj)
    conv_ch = u.shape[1]
    hist = jnp.pad(conv_hist, ((0, 0), (HALO_ROWS - conv_hist.shape[1], 0), (0, 0)))
    conv = _conv_branch(u.reshape(n_streams, T, conv_ch), hist, p["w_dw"], p["b_dw"], p["g_ln"], p["b_ln"])
    y = _ffn(x1, p["g_ffn2"], p["w2g"], p["w2u"], p["w2d"],
             mix=(attn, conv.reshape(n_streams * T, conv_ch), p["wo_attn"], p["wo_conv"]),
             g_final=p["g_final"], name="ffn2")
    return y, k32, v32, u


def kernel(x_prompt, x_sample, cache_k, cache_v, cache_conv, g_ffn1, w_ffn1_gu, w_ffn1_down, g_mix, w_in, g_q, g_k, lambda_q1, lambda_k1, lambda_q2, lambda_k2, g_sub, w_dw, b_dw, g_conv_ln, b_conv_ln, w_out, g_ffn2, w_ffn2_gu, w_ffn2_down, g_final):
    depth = cache_k.shape[0]
    n_p, t_p, d_model = x_prompt.shape
    n_s, t_s, _ = x_sample.shape
    past = cache_k.shape[2]
    d_ff = w_ffn1_down.shape[1]
    conv_ch = cache_conv.shape[-1]
    hist_rows = CONV_WIDTH - 1
    assert n_p == 1, "prompt attention kernel handles one prompt stream"
    assert t_p >= hist_rows and t_s >= hist_rows

    group = jnp.arange(Q_WIDTH) // HEAD_DIM
    ones_bd = (group[:, None] == group[None, :]).astype(BF16)
    row = lambda a: a.reshape(1, -1).astype(F32)

    yp = x_prompt.reshape(n_p * t_p, d_model)
    ys = x_sample.reshape(n_s * t_s, d_model)
    outs = [[] for _ in range(6)]
    for l in range(depth):
        lam_init = 0.8 - 0.6 * math.exp(-0.3 * l)
        lam_rows = jnp.stack([lambda_q1[l], lambda_k1[l], lambda_q2[l], lambda_k2[l]]).astype(F32)
        p = dict(
            g_ffn1=row(g_ffn1[l]), w1g=w_ffn1_gu[l][:, :d_ff].astype(BF16), w1u=w_ffn1_gu[l][:, d_ff:].astype(BF16),
            w1d=w_ffn1_down[l].astype(BF16), g_mix=row(g_mix[l]), w_in=w_in[l].astype(BF16),
            gq_t=row(jnp.tile(g_q[l], 2 * N_HEADS)), gk_t=row(jnp.tile(g_k[l], 2 * N_HEADS)), ones_bd=ones_bd,
            w_dw=w_dw[l].astype(F32), b_dw=row(b_dw[l]), g_ln=row(g_conv_ln[l]), b_ln=row(b_conv_ln[l]),
            wo_attn=w_out[l][:ATTN_WIDTH].astype(BF16), wo_conv=w_out[l][ATTN_WIDTH:].astype(BF16),
            g_ffn2=row(g_ffn2[l]), w2g=w_ffn2_gu[l][:, :d_ff].astype(BF16), w2u=w_ffn2_gu[l][:, d_ff:].astype(BF16),
            w2d=w_ffn2_down[l].astype(BF16), g_final=row(g_final[l]),
        )
        g_sub_row = row(g_sub[l])

        q_tile, key_tile = _prompt_tiles(t_p)

        def attend_prompt(qs, ks, vt, k32, v32, u):
            return _prompt_attention(qs, ks, vt, lam_rows, g_sub_row, lam_init, q_tile, key_tile)

        def attend_sample(qs, k32, v32, u, l=l):
            ck = jnp.swapaxes(cache_k[l].reshape(n_s, past, Q_WIDTH), 1, 2)
            cv = cache_v[l].reshape(n_s, past * N_HEADS, V_DIM)
            return _sample_attention(qs, k32, v32, ck, cv, lam_rows, g_sub_row, lam_init, n_s)

        zero_hist = jnp.zeros((n_p, hist_rows, conv_ch), F32)
        yp, kp, vp, up = _layer(yp, n_p, attend_prompt, key_tile, zero_hist, lam_init, p)
        ys, ks, vs, us = _layer(ys, n_s, attend_sample, None, cache_conv[l], lam_init, p)
        outs[0].append(kp.reshape(n_p, t_p, N_HEADS, 2, HEAD_DIM))
        outs[1].append(vp.reshape(n_p, t_p, N_HEADS, V_DIM))
        outs[2].append(up.reshape(n_p, t_p, conv_ch)[:, t_p - hist_rows:])
        outs[3].append(ks.reshape(n_s, t_s, N_HEADS, 2, HEAD_DIM))
        outs[4].append(vs.reshape(n_s, t_s, N_HEADS, V_DIM))
        outs[5].append(us.reshape(n_s, t_s, conv_ch)[:, t_s - hist_rows:])

    return (yp.reshape(n_p, t_p, d_model), ys.reshape(n_s, t_s, d_model),
            *[jnp.stack(o, axis=0) for o in outs])
```

```python
import functools
import math

import jax
import jax.numpy as jnp
from jax import lax
from jax.experimental import pallas as pl
from jax.experimental.pallas import tpu as pltpu

N_HEADS = 4
HEAD_DIM = 64
V_DIM = 2 * HEAD_DIM
Q_WIDTH = N_HEADS * 2 * HEAD_DIM
ATTN_WIDTH = N_HEADS * V_DIM
CHUNK = 64
CONV_WIDTH = 31
EPS = 1e-6
NEG_INF = -1e30
ALIBI_SLOPES = tuple(2.0 ** (-8.0 * (h + 1) / N_HEADS) for h in range(N_HEADS))
LOG2E = math.log2(math.e)

V7X_LANES = 128
V7X_SUBLANES = 8
V7X_VMEM_BYTES = 64 * 1024 * 1024
VMEM_LIMIT_BYTES = V7X_VMEM_BYTES - 8 * 1024 * 1024

FFN_ROW_TILE = 512
CONV_ROW_TILE = 256
ATTN_KEY_TILE = 1024
ATTN_QUERY_TILE = 1024
SAMPLE_KEY_CHUNK = 1024
QUERY_BLOCK = 256
ALIBI_PIECES = 3
KEY_OFFSET_LO = 32
CONV_ROW_BLOCK = 128
HALO_ROWS = 32

F32 = jnp.float32
BF16 = jnp.bfloat16


def _dot(a, b):
    return jnp.dot(a, b, preferred_element_type=F32)


def _dot_nt(a, b):
    return lax.dot_general(a, b, (((1,), (1,)), ((), ())), preferred_element_type=F32)


def _rms_rows(x, g):
    return x * lax.rsqrt(jnp.mean(x * x, axis=-1, keepdims=True) + EPS) * g


def _params(n_grid_axes):
    return pltpu.CompilerParams(
        dimension_semantics=("arbitrary",) * n_grid_axes,
        vmem_limit_bytes=VMEM_LIMIT_BYTES,
    )


def _const_spec(shape):
    nd = len(shape)
    return pl.BlockSpec(shape, lambda *_: (0,) * nd, pipeline_mode=pl.Buffered(1))


def _row_tile(n_rows, want):
    t = min(want, n_rows)
    assert n_rows % t == 0, (n_rows, t)
    return t


def _ffn_kernel(*refs, has_mix, has_final):
    it = iter(refs)
    x_ref = next(it)
    if has_mix:
        attn_ref, conv_ref, woa_ref, wob_ref = next(it), next(it), next(it), next(it)
    g_ref, wg_ref, wu_ref, wd_ref = next(it), next(it), next(it), next(it)
    gf_ref = next(it) if has_final else None
    out_ref = next(it)

    x = x_ref[...]
    if has_mix:
        x = x + _dot(attn_ref[...], woa_ref[...]) + _dot(conv_ref[...], wob_ref[...])
    h = _rms_rows(x, g_ref[...]).astype(BF16)
    a = _dot(h, wg_ref[...])
    b = _dot(h, wu_ref[...])
    act = (a * jax.nn.sigmoid(a) * b).astype(BF16)
    y = x + 0.5 * _dot(act, wd_ref[...])
    if has_final:
        y = _rms_rows(y, gf_ref[...])
    out_ref[...] = y


def _ffn(x, g, wg, wu, wd, mix=None, g_final=None, name="ffn"):
    T, D = x.shape
    F = wg.shape[1]
    tm = _row_tile(T, FFN_ROW_TILE)
    row = lambda w: pl.BlockSpec((tm, w), lambda i: (i, 0))
    args, specs = [x], [row(D)]
    if mix is not None:
        attn, conv, woa, wob = mix
        args += [attn, conv, woa, wob]
        specs += [row(attn.shape[1]), row(conv.shape[1]), _const_spec(woa.shape), _const_spec(wob.shape)]
    args += [g, wg, wu, wd]
    specs += [_const_spec((1, D)), _const_spec((D, F)), _const_spec((D, F)), _const_spec((F, D))]
    if g_final is not None:
        args.append(g_final)
        specs.append(_const_spec((1, D)))
    return pl.pallas_call(
        functools.partial(_ffn_kernel, has_mix=mix is not None, has_final=g_final is not None),
        grid=(T // tm,),
        in_specs=specs,
        out_specs=row(D),
        out_shape=jax.ShapeDtypeStruct((T, D), F32),
        compiler_params=_params(1),
        name=name,
    )(*args)


def _group_rms(y, g, ones_bd):
    y2 = y * y
    hi = y2.astype(BF16)
    lo = (y2 - hi.astype(F32)).astype(BF16)
    ss = _dot(hi, ones_bd) + _dot(lo, ones_bd)
    return y * lax.rsqrt(ss * (1.0 / HEAD_DIM) + EPS) * g


def _alibi_query_lanes():
    c = jnp.asarray([s * LOG2E for s in ALIBI_SLOPES], F32)
    pieces, rest = [], c
    for _ in range(ALIBI_PIECES):
        pieces.append(rest.astype(BF16).astype(F32))
        rest = rest - pieces[-1]
    pieces = jnp.stack(pieces + pieces, axis=1)
    return jnp.zeros((N_HEADS, V_DIM), F32).at[:, HEAD_DIM:HEAD_DIM + 2 * ALIBI_PIECES].set(pieces)


def _proj_kernel(*refs, key_tile):
    if key_tile is None:
        x_ref, g_ref, w_ref, gq_ref, gk_ref, ones_ref, qs_ref, k32_ref, v32_ref, u_ref = refs
    else:
        (x_ref, g_ref, w_ref, gq_ref, gk_ref, ones_ref, qext_ref,
         qs_ref, ks_ref, vt_ref, k32_ref, v32_ref, u_ref) = refs
    tm = x_ref.shape[0]
    h = _rms_rows(x_ref[...], g_ref[...]).astype(BF16)
    z = _dot(h, w_ref[...])
    ones_bd = ones_ref[...]
    qn = _group_rms(z[:, :Q_WIDTH], gq_ref[...], ones_bd) * (HEAD_DIM ** -0.5 * LOG2E)
    kn = _group_rms(z[:, Q_WIDTH:2 * Q_WIDTH], gk_ref[...], ones_bd)
    v = z[:, 2 * Q_WIDTH:2 * Q_WIDTH + ATTN_WIDTH]
    conv_ch = (z.shape[1] - 2 * Q_WIDTH - ATTN_WIDTH) // 2
    a = z[:, 2 * Q_WIDTH + ATTN_WIDTH:2 * Q_WIDTH + ATTN_WIDTH + conv_ch]
    gate = z[:, 2 * Q_WIDTH + ATTN_WIDTH + conv_ch:]
    k32_ref[...] = kn
    for hh in range(N_HEADS):
        v32_ref[pl.ds(hh, tm, stride=N_HEADS), :] = v[:, V_DIM * hh:V_DIM * (hh + 1)]
    u_ref[...] = a * jax.nn.sigmoid(gate)

    lane = lax.broadcasted_iota(jnp.int32, (tm, V_DIM), 1)
    low = lane < HEAD_DIM
    if key_tile is None:
        for hh in range(N_HEADS):
            qh = qn[:, V_DIM * hh:V_DIM * (hh + 1)]
            qs_ref[hh, 0] = jnp.where(low, qh, 0.0).astype(BF16)
            qs_ref[hh, 1] = jnp.where(low, 0.0, qh).astype(BF16)
        return

    row = pl.program_id(0) * tm + lax.broadcasted_iota(jnp.int32, (tm, V_DIM), 0)
    off = row % key_tile
    lo = off % KEY_OFFSET_LO
    in_hi = (lane >= HEAD_DIM) & (lane < HEAD_DIM + ALIBI_PIECES)
    in_lo = (lane >= HEAD_DIM + ALIBI_PIECES) & (lane < HEAD_DIM + 2 * ALIBI_PIECES)
    k_ext = jnp.where(in_hi, off - lo, jnp.where(in_lo, lo, 0)).astype(F32)
    for hh in range(N_HEADS):
        sl = slice(V_DIM * hh, V_DIM * (hh + 1))
        q_ext = qext_ref[hh:hh + 1, :]
        for x, ext, dst in ((qn[:, sl], q_ext, qs_ref), (kn[:, sl], k_ext, ks_ref)):
            dst[hh, 0] = jnp.where(low, x, ext).astype(BF16)
            dst[hh, 1] = jnp.where(low, pltpu.roll(x, HEAD_DIM, axis=1), ext).astype(BF16)
        vt_ref[hh] = v[:, sl].T.astype(BF16)


def _project(x1, g_mix, w_in, gq_t, gk_t, ones_bd, key_tile=None):
    T, D = x1.shape
    C = w_in.shape[1]
    conv_ch = (C - 2 * Q_WIDTH - ATTN_WIDTH) // 2
    tm = _row_tile(T, FFN_ROW_TILE)
    row = lambda w: pl.BlockSpec((tm, w), lambda i: (i, 0))
    per_map = pl.BlockSpec((N_HEADS, 2, tm, V_DIM), lambda i: (0, 0, i, 0))
    per_map_shape = jax.ShapeDtypeStruct((N_HEADS, 2, T, V_DIM), BF16)
    args = [x1, g_mix, w_in, gq_t, gk_t, ones_bd]
    in_specs = [row(D), _const_spec((1, D)), _const_spec((D, C)), _const_spec((1, Q_WIDTH)),
                _const_spec((1, Q_WIDTH)), _const_spec((Q_WIDTH, Q_WIDTH))]
    out_specs, out_shape = [per_map], [per_map_shape]
    if key_tile is not None:
        args.append(_alibi_query_lanes())
        in_specs.append(_const_spec((N_HEADS, V_DIM)))
        out_specs += [per_map, pl.BlockSpec((N_HEADS, V_DIM, tm), lambda i: (0, 0, i))]
        out_shape += [per_map_shape, jax.ShapeDtypeStruct((N_HEADS, V_DIM, T), BF16)]
    out_specs += [row(Q_WIDTH), pl.BlockSpec((tm * N_HEADS, V_DIM), lambda i: (i, 0)), row(conv_ch)]
    out_shape += [jax.ShapeDtypeStruct(s, F32) for s in ((T, Q_WIDTH), (T * N_HEADS, V_DIM), (T, conv_ch))]
    return pl.pallas_call(
        functools.partial(_proj_kernel, key_tile=key_tile),
        grid=(T // tm,),
        in_specs=in_specs,
        out_specs=out_specs,
        out_shape=out_shape,
        compiler_params=_params(1),
        name="proj",
    )(*args)


def _diff_lambda(lam_ref, lam_init):
    lp = lam_ref[...]
    s1 = jnp.sum(lp[0:1] * lp[1:2], axis=-1, keepdims=True)
    s2 = jnp.sum(lp[2:3] * lp[3:4], axis=-1, keepdims=True)
    return jnp.exp(s1) - jnp.exp(s2) + lam_init


def _head_out(o1, o2, lam, g_sub, lam_init):
    d = o1 - lam * o2
    return _rms_rows(d, g_sub) * (1.0 - lam_init)


def _prompt_attn_kernel(qs_ref, qnext_ref, ks_ref, vt_ref, lam_ref, gsub_ref, out_ref,
                        acc_ref, m_ref, l_ref, sa_ref, sb_ref, maxa_ref, maxb_ref, p_ref, a_ref, base_ref,
                        *, tq, tk, lam_init):
    hh = pl.program_id(0)
    qi = pl.program_id(1)
    cols = 2 * tq
    slope2 = jnp.float32(0.0)
    for h_static, sl in enumerate(ALIBI_SLOPES):
        slope2 = jnp.where(hh == h_static, jnp.float32(sl * LOG2E), slope2)

    acc_ref[...] = jnp.zeros(acc_ref.shape, F32)
    m_ref[...] = jnp.full(m_ref.shape, NEG_INF, F32)
    l_ref[...] = jnp.zeros(l_ref.shape, F32)

    q_start = qi * tq
    n_full = q_start // tk

    def tile_start(j):
        return pl.multiple_of(j * tk, tk)

    n_blocks = cols // QUERY_BLOCK

    def scores(j, buf, c, q_ref=qs_ref):
        s_ref, max_ref = buf
        which, first = divmod(c * QUERY_BLOCK, tq)
        q = q_ref[which, first:first + QUERY_BLOCK, :]
        s = _dot_nt(ks_ref[which, pl.ds(tile_start(j), tk), :], q)
        s_ref[c] = s
        max_ref[c] = jnp.max(s, axis=0, keepdims=True)

    def softmax_pv(j, buf, c, n_keys=None):
        s_ref, max_ref = buf
        if n_keys is None:
            keys, tile_max = slice(0, tk), max_ref[c]
        else:
            keys = slice(0, n_keys)
            tile_max = jnp.max(s_ref[c, keys, :], axis=0, keepdims=True)
        t_off = slope2 * (j * tk - q_start).astype(F32)
        m = m_ref[c]
        m_new = jnp.maximum(m, tile_max + t_off)
        alpha = jnp.exp2(m - m_new)
        p = jnp.exp2(s_ref[c, keys, :] - (m_new - t_off))
        l_ref[c] = alpha * l_ref[c] + jnp.sum(p, axis=0, keepdims=True)
        m_ref[c] = m_new
        p_ref[c, keys, :] = p.astype(BF16)
        a_ref[c] = alpha
        vt = vt_ref[:, pl.ds(tile_start(j), keys.stop)]
        acc_ref[c] = a_ref[c] * acc_ref[c] + _dot(vt, p_ref[c, keys, :])

    def mask_own(buf, c, first_key):
        s_ref = buf[0]
        keys = slice(first_key, first_key + QUERY_BLOCK)
        ko = lax.broadcasted_iota(jnp.int32, (QUERY_BLOCK, QUERY_BLOCK), 0)
        qo = lax.broadcasted_iota(jnp.int32, (QUERY_BLOCK, QUERY_BLOCK), 1)
        after = (-2.0 * slope2) * jnp.maximum(ko - qo, 0).astype(F32)
        visible = (ko // CHUNK) <= (qo // CHUNK)
        s_ref[c, keys, :] = jnp.where(visible, s_ref[c, keys, :] + after, NEG_INF)

    sets = ((sa_ref, maxa_ref), (sb_ref, maxb_ref))

    @pl.when(qi == 0)
    def _():
        base_ref[0] = 0
        for c in range(n_blocks):
            scores(0, sets[0], c)

    base = base_ref[0]

    def step(j, cur, other):
        for c in range(n_blocks):
            scores(j + 1, other, c)
            softmax_pv(j, cur, c)

    def body(j, carry):
        for parity in range(2):
            @pl.when((j + base) % 2 == parity)
            def _(parity=parity):
                step(j, sets[parity], sets[1 - parity])

        return carry

    lax.fori_loop(0, n_full, body, 0)

    def last(cur, other):
        for sub in range(tk // tq):
            @pl.when(q_start - n_full * tk == sub * tq)
            def _(sub=sub):
                for c in range(n_blocks):
                    if other is not None:
                        scores(0, other, c, qnext_ref)
                    own = sub * tq + (c * QUERY_BLOCK) % tq
                    mask_own(cur, c, own)
                    softmax_pv(n_full, cur, c, own + QUERY_BLOCK)

    has_next = qi + 1 < pl.num_programs(1)
    for parity in range(2):
        @pl.when(jnp.logical_and((n_full + base) % 2 == parity, has_next))
        def _(parity=parity):
            last(sets[parity], sets[1 - parity])
            base_ref[0] = 1 - parity

        @pl.when(jnp.logical_and((n_full + base) % 2 == parity, jnp.logical_not(has_next)))
        def _(parity=parity):
            last(sets[parity], None)

    lam = _diff_lambda(lam_ref, lam_init)
    per_map = tq // QUERY_BLOCK
    ot = [acc_ref[c] * (1.0 / l_ref[c]) for c in range(n_blocks)]
    dt = jnp.concatenate([ot[c] - lam * ot[per_map + c] for c in range(per_map)], axis=1)
    yt = dt * lax.rsqrt(jnp.mean(dt * dt, axis=0, keepdims=True) + EPS)
    out_ref[...] = (yt.T * gsub_ref[...] * (1.0 - lam_init)).astype(out_ref.dtype)


def _prompt_tiles(T):
    tk = _row_tile(T, ATTN_KEY_TILE)
    tq = _row_tile(tk, ATTN_QUERY_TILE)
    assert tq % QUERY_BLOCK == 0 and QUERY_BLOCK % CHUNK == 0
    assert tk // KEY_OFFSET_LO <= 256
    return tq, tk


def _prompt_attention(qs, ks, vt, lam_rows, g_sub, lam_init, tq, tk):
    T = ks.shape[2]
    n_q = T // tq
    n_blocks = 2 * tq // QUERY_BLOCK
    return pl.pallas_call(
        functools.partial(_prompt_attn_kernel, tq=tq, tk=tk, lam_init=lam_init),
        grid=(N_HEADS, n_q),
        in_specs=[
            pl.BlockSpec((None, 2, tq, V_DIM), lambda h, i: (h, 0, i, 0)),
            pl.BlockSpec((None, 2, tq, V_DIM), lambda h, i: (h, 0, jnp.minimum(i + 1, n_q - 1), 0)),
            pl.BlockSpec((None, 2, T, V_DIM), lambda h, i: (h, 0, 0, 0), pipeline_mode=pl.Buffered(1)),
            pl.BlockSpec((None, V_DIM, T), lambda h, i: (h, 0, 0), pipeline_mode=pl.Buffered(1)),
            pl.BlockSpec((4, HEAD_DIM), lambda h, i: (0, 0)),
            pl.BlockSpec((1, V_DIM), lambda h, i: (0, 0)),
        ],
        out_specs=pl.BlockSpec((tq, V_DIM), lambda h, i: (i, h)),
        out_shape=jax.ShapeDtypeStruct((T, ATTN_WIDTH), BF16),
        scratch_shapes=[
            pltpu.VMEM((n_blocks, V_DIM, QUERY_BLOCK), F32),
            pltpu.VMEM((n_blocks, 1, QUERY_BLOCK), F32),
            pltpu.VMEM((n_blocks, 1, QUERY_BLOCK), F32),
            pltpu.VMEM((n_blocks, tk, QUERY_BLOCK), F32),
            pltpu.VMEM((n_blocks, tk, QUERY_BLOCK), F32),
            pltpu.VMEM((n_blocks, 1, QUERY_BLOCK), F32),
            pltpu.VMEM((n_blocks, 1, QUERY_BLOCK), F32),
            pltpu.VMEM((n_blocks, tk, QUERY_BLOCK), BF16),
            pltpu.VMEM((n_blocks, 1, QUERY_BLOCK), F32),
            pltpu.SMEM((1,), jnp.int32),
        ],
        compiler_params=_params(2),
        name="prompt_attn",
    )(qs, qs, ks, vt, lam_rows, g_sub)


def _sample_attn_kernel(qs_ref, kn_ref, vn_ref, ckt_ref, cv_ref, lam_ref, gsub_ref, out_ref,
                        m_ref, l_ref, acc_ref, *, t_new, past, tk, lam_init):
    grp = 2 * t_new
    rows = N_HEADS * grp
    blocks = []
    for hh in range(N_HEADS):
        qh = qs_ref[hh].reshape(grp, V_DIM)
        z = jnp.zeros((grp, V_DIM), BF16)
        blocks.append(jnp.concatenate([qh if c == hh else z for c in range(N_HEADS)], axis=1))
    q = jnp.concatenate(blocks, axis=0)

    r = lax.broadcasted_iota(jnp.int32, (rows, 1), 0)
    q_pos = past + (r % t_new)
    head = r // grp
    slope2 = jnp.zeros((rows, 1), F32)
    for h_static, sl in enumerate(ALIBI_SLOPES):
        slope2 = jnp.where(head == h_static, jnp.float32(sl * LOG2E), slope2)

    m_ref[...] = jnp.full(m_ref.shape, NEG_INF, F32)
    l_ref[...] = jnp.zeros(l_ref.shape, F32)
    acc_ref[...] = jnp.zeros(acc_ref.shape, F32)

    def step(qk, head_values, k_start, n_keys):
        k_pos = k_start + lax.broadcasted_iota(jnp.int32, (1, n_keys), 1)
        bias = -slope2 * jnp.abs(q_pos - k_pos).astype(F32)
        visible = (k_pos // CHUNK) <= (q_pos // CHUNK)
        s = jnp.where(visible, qk + bias, NEG_INF)
        m_prev = m_ref[...]
        m_new = jnp.maximum(m_prev, jnp.max(s, axis=1, keepdims=True))
        alpha = jnp.exp2(m_prev - m_new)
        p = jnp.exp2(s - m_new)
        l_ref[...] = alpha * l_ref[...] + jnp.sum(p, axis=1, keepdims=True)
        m_ref[...] = m_new
        p = p.astype(BF16)
        for hh in range(N_HEADS):
            mine = slice(hh * grp, (hh + 1) * grp)
            acc_ref[mine, :] = alpha[mine] * acc_ref[mine, :] + _dot(p[mine], head_values(hh).astype(BF16))

    for c in range(past // tk):
        step(_dot(q, ckt_ref[:, c * tk:(c + 1) * tk].astype(BF16)),
             lambda hh, c=c: cv_ref[pl.ds(c * tk * N_HEADS + hh, tk, stride=N_HEADS), :],
             c * tk, tk)
    step(_dot_nt(q, kn_ref[...].astype(BF16)), lambda hh: vn_ref[pl.ds(hh, t_new, stride=N_HEADS), :],
         past, t_new)

    o = acc_ref[...] / l_ref[...]
    lam = _diff_lambda(lam_ref, lam_init)
    outs = []
    for hh in range(N_HEADS):
        o1 = o[hh * grp:hh * grp + t_new]
        o2 = o[hh * grp + t_new:(hh + 1) * grp]
        outs.append(_head_out(o1, o2, lam, gsub_ref[...], lam_init))
    out_ref[...] = jnp.concatenate(outs, axis=1).astype(out_ref.dtype)


def _sample_attention(qs, k_new, v_new, cache_kt, cache_v, lam_rows, g_sub, lam_init, n_streams):
    t_new = k_new.shape[0] // n_streams
    past = cache_kt.shape[2]
    tk = _row_tile(past, SAMPLE_KEY_CHUNK)
    rows = N_HEADS * 2 * t_new
    return pl.pallas_call(
        functools.partial(_sample_attn_kernel, t_new=t_new, past=past, tk=tk, lam_init=lam_init),
        grid=(n_streams,),
        in_specs=[
            pl.BlockSpec((N_HEADS, 2, t_new, V_DIM), lambda b: (0, 0, b, 0)),
            pl.BlockSpec((t_new, Q_WIDTH), lambda b: (b, 0)),
            pl.BlockSpec((t_new * N_HEADS, V_DIM), lambda b: (b, 0)),
            pl.BlockSpec((None, Q_WIDTH, past), lambda b: (b, 0, 0)),
            pl.BlockSpec((None, past * N_HEADS, V_DIM), lambda b: (b, 0, 0)),
            pl.BlockSpec((4, HEAD_DIM), lambda b: (0, 0)),
            pl.BlockSpec((1, V_DIM), lambda b: (0, 0)),
        ],
        out_specs=pl.BlockSpec((t_new, ATTN_WIDTH), lambda b: (b, 0)),
        out_shape=jax.ShapeDtypeStruct((n_streams * t_new, ATTN_WIDTH), BF16),
        scratch_shapes=[
            pltpu.VMEM((rows, 1), F32),
            pltpu.VMEM((rows, 1), F32),
            pltpu.VMEM((rows, V_DIM), F32),
        ],
        compiler_params=_params(1),
        name="sample_attn",
    )(qs, k_new, v_new, cache_kt, cache_v, lam_rows, g_sub)


def _conv_kernel(u_ref, halo_ref, init_ref, w_ref, b_ref, g_ref, beta_ref, out_ref,
                 xp_ref, shift_ref, y_ref, *, tm):
    i = pl.program_id(1)
    hist = jnp.where(i == 0, init_ref[...], halo_ref[...])
    xp_ref[0:HALO_ROWS, :] = hist
    xp_ref[HALO_ROWS:HALO_ROWS + tm, :] = u_ref[...]
    first = HALO_ROWS - (CONV_WIDTH - 1)
    n_lane_blocks = u_ref.shape[-1] // V7X_LANES
    taps_of = lambda phase: (CONV_WIDTH - 1 - phase) // V7X_SUBLANES + 1

    for phase in range(V7X_SUBLANES):
        n_win = tm + V7X_SUBLANES * (taps_of(phase) - 1)
        for c in range(n_lane_blocks):
            shift_ref[phase, c, 0:n_win, :] = xp_ref[first + phase:first + phase + n_win,
                                                     c * V7X_LANES:(c + 1) * V7X_LANES]

    rows = min(tm, CONV_ROW_BLOCK)
    n_row_blocks = tm // rows

    def block(t, carry):
        c = t // n_row_blocks
        r0 = pl.multiple_of((t % n_row_blocks) * rows, rows)
        acc = jnp.zeros((rows, V7X_LANES), F32) + b_ref[c]
        for phase in range(V7X_SUBLANES):
            for a in range(taps_of(phase)):
                w = V7X_SUBLANES * a + phase
                acc = acc + shift_ref[phase, c, pl.ds(r0 + V7X_SUBLANES * a, rows), :] * w_ref[c, w:w + 1, :]
        y_ref[c, pl.ds(r0, rows), :] = acc
        return carry

    lax.fori_loop(0, n_lane_blocks * n_row_blocks, block, 0)
    y = jnp.concatenate([y_ref[c] for c in range(n_lane_blocks)], axis=1)
    mu = jnp.mean(y, axis=-1, keepdims=True)
    d = y - mu
    var = jnp.mean(d * d, axis=-1, keepdims=True)
    yn = d * lax.rsqrt(var + EPS) * g_ref[...] + beta_ref[...]
    out_ref[...] = (yn * jax.nn.sigmoid(yn)).astype(out_ref.dtype)


def _conv_branch(u, init_hist, w_dw, b_dw, g_ln, b_ln):
    B, T, C = u.shape
    tm = _row_tile(T, CONV_ROW_TILE)
    assert tm % HALO_ROWS == 0
    per = tm // HALO_ROWS
    n_lane_blocks = C // V7X_LANES
    vec = lambda: pl.BlockSpec((1, C), lambda b, i: (0, 0))
    w_blocks = w_dw.reshape(CONV_WIDTH, n_lane_blocks, V7X_LANES).swapaxes(0, 1)
    b_blocks = b_dw.reshape(n_lane_blocks, 1, V7X_LANES)
    return pl.pallas_call(
        functools.partial(_conv_kernel, tm=tm),
        grid=(B, T // tm),
        in_specs=[
            pl.BlockSpec((None, tm, C), lambda b, i: (b, i, 0)),
            pl.BlockSpec((None, HALO_ROWS, C), lambda b, i: (b, jnp.maximum(i * per - 1, 0), 0)),
            pl.BlockSpec((None, HALO_ROWS, C), lambda b, i: (b, 0, 0)),
            pl.BlockSpec((n_lane_blocks, CONV_WIDTH, V7X_LANES), lambda b, i: (0, 0, 0)),
            pl.BlockSpec((n_lane_blocks, 1, V7X_LANES), lambda b, i: (0, 0, 0)),
            vec(), vec(),
        ],
        out_specs=pl.BlockSpec((None, tm, C), lambda b, i: (b, i, 0)),
        out_shape=jax.ShapeDtypeStruct((B, T, C), BF16),
        scratch_shapes=[
            pltpu.VMEM((HALO_ROWS + tm, C), F32),
            pltpu.VMEM((V7X_SUBLANES, n_lane_blocks, tm + HALO_ROWS, V7X_LANES), F32),
            pltpu.VMEM((n_lane_blocks, tm, V7X_LANES), F32),
        ],
        compiler_params=_params(2),
        name="conv",
    )(u, u, init_hist, w_blocks, b_blocks, g_ln, b_ln)


def _layer(x, n_streams, attend, key_tile, conv_hist, lam_init, p):
    T = x.shape[0] // n_streams
    x1 = _ffn(x, p["g_ffn1"], p["w1g"], p["w1u"], p["w1d"], name="ffn1")
    proj = _project(x1, p["g_mix"], p["w_in"], p["gq_t"], p["gk_t"], p["ones_bd"], key_tile)
    k32, v32, u = proj[-3:]
    attn = attend(*proj)
    conv_ch = u.shape[1]
    hist = jnp.pad(conv_hist, ((0, 0), (HALO_ROWS - conv_hist.shape[1], 0), (0, 0)))
    conv = _conv_branch(u.reshape(n_streams, T, conv_ch), hist, p["w_dw"], p["b_dw"], p["g_ln"], p["b_ln"])
    y = _ffn(x1, p["g_ffn2"], p["w2g"], p["w2u"], p["w2d"],
             mix=(attn, conv.reshape(n_streams * T, conv_ch), p["wo_attn"], p["wo_conv"]),
             g_final=p["g_final"], name="ffn2")
    return y, k32, v32, u


def kernel(x_prompt, x_sample, cache_k, cache_v, cache_conv, g_ffn1, w_ffn1_gu, w_ffn1_down, g_mix, w_in, g_q, g_k, lambda_q1, lambda_k1, lambda_q2, lambda_k2, g_sub, w_dw, b_dw, g_conv_ln, b_conv_ln, w_out, g_ffn2, w_ffn2_gu, w_ffn2_down, g_final):
    depth = cache_k.shape[0]
    n_p, t_p, d_model = x_prompt.shape
    n_s, t_s, _ = x_sample.shape
    past = cache_k.shape[2]
    d_ff = w_ffn1_down.shape[1]
    conv_ch = cache_conv.shape[-1]
    hist_rows = CONV_WIDTH - 1
    assert n_p == 1, "prompt attention kernel handles one prompt stream"
    assert t_p >= hist_rows and t_s >= hist_rows

    group = jnp.arange(Q_WIDTH) // HEAD_DIM
    ones_bd = (group[:, None] == group[None, :]).astype(BF16)
    row = lambda a: a.reshape(1, -1)

    yp = x_prompt.reshape(n_p * t_p, d_model)
    ys = x_sample.reshape(n_s * t_s, d_model)
    outs = [[] for _ in range(6)]
    for l in range(depth):
        lam_init = 0.8 - 0.6 * math.exp(-0.3 * l)
        lam_rows = jnp.stack([lambda_q1[l], lambda_k1[l], lambda_q2[l], lambda_k2[l]])
        p = dict(
            g_ffn1=row(g_ffn1[l]), w1g=w_ffn1_gu[l][:, :d_ff].astype(BF16), w1u=w_ffn1_gu[l][:, d_ff:].astype(BF16),
            w1d=w_ffn1_down[l].astype(BF16), g_mix=row(g_mix[l]), w_in=w_in[l].astype(BF16),
            gq_t=row(jnp.tile(g_q[l], 2 * N_HEADS)), gk_t=row(jnp.tile(g_k[l], 2 * N_HEADS)), ones_bd=ones_bd,
            w_dw=w_dw[l], b_dw=row(b_dw[l]), g_ln=row(g_conv_ln[l]), b_ln=row(b_conv_ln[l]),
            wo_attn=w_out[l][:ATTN_WIDTH].astype(BF16), wo_conv=w_out[l][ATTN_WIDTH:].astype(BF16),
            g_ffn2=row(g_ffn2[l]), w2g=w_ffn2_gu[l][:, :d_ff].astype(BF16), w2u=w_ffn2_gu[l][:, d_ff:].astype(BF16),
            w2d=w_ffn2_down[l].astype(BF16), g_final=row(g_final[l]),
        )
        g_sub_row = row(g_sub[l])

        q_tile, key_tile = _prompt_tiles(t_p)

        def attend_prompt(qs, ks, vt, k32, v32, u):
            return _prompt_attention(qs, ks, vt, lam_rows, g_sub_row, lam_init, q_tile, key_tile)

        def attend_sample(qs, k32, v32, u, l=l):
            ck = jnp.swapaxes(cache_k[l].reshape(n_s, past, Q_WIDTH), 1, 2)
            cv = cache_v[l].reshape(n_s, past * N_HEADS, V_DIM)
            return _sample_attention(qs, k32, v32, ck, cv, lam_rows, g_sub_row, lam_init, n_s)

        zero_hist = jnp.zeros((n_p, hist_rows, conv_ch), F32)
        yp, kp, vp, up = _layer(yp, n_p, attend_prompt, key_tile, zero_hist, lam_init, p)
        ys, ks, vs, us = _layer(ys, n_s, attend_sample, None, cache_conv[l], lam_init, p)
        outs[0].append(kp.reshape(n_p, t_p, N_HEADS, 2, HEAD_DIM))
        outs[1].append(vp.reshape(n_p, t_p, N_HEADS, V_DIM))
        outs[2].append(up.reshape(n_p, t_p, conv_ch)[:, t_p - hist_rows:])
        outs[3].append(ks.reshape(n_s, t_s, N_HEADS, 2, HEAD_DIM))
        outs[4].append(vs.reshape(n_s, t_s, N_HEADS, V_DIM))
        outs[5].append(us.reshape(n_s, t_s, conv_ch)[:, t_s - hist_rows:])

    return (yp.reshape(n_p, t_p, d_model), ys.reshape(n_s, t_s, d_model),
            *[jnp.stack(o, axis=0) for o in outs])
```

```python
import functools
import math

import jax
import jax.numpy as jnp
from jax import lax
from jax.experimental import pallas as pl
from jax.experimental.pallas import tpu as pltpu

N_HEADS = 4
HEAD_DIM = 64
V_DIM = 2 * HEAD_DIM
Q_WIDTH = N_HEADS * 2 * HEAD_DIM
ATTN_WIDTH = N_HEADS * V_DIM
CHUNK = 64
CONV_WIDTH = 31
EPS = 1e-6
NEG_INF = -1e30
ALIBI_SLOPES = tuple(2.0 ** (-8.0 * (h + 1) / N_HEADS) for h in range(N_HEADS))
LOG2E = math.log2(math.e)

V7X_LANES = 128
V7X_SUBLANES = 8
V7X_VMEM_BYTES = 64 * 1024 * 1024
VMEM_LIMIT_BYTES = V7X_VMEM_BYTES - 8 * 1024 * 1024

FFN_ROW_TILE = 512
CONV_ROW_TILE = 256
ATTN_KEY_TILE = 1024
ATTN_QUERY_TILE = 1024
SAMPLE_KEY_CHUNK = 1024
QUERY_BLOCK = 256
ALIBI_PIECES = 3
KEY_OFFSET_LO = 32
CONV_ROW_BLOCK = 128
HALO_ROWS = 32

F32 = jnp.float32
BF16 = jnp.bfloat16


def _dot(a, b):
    return jnp.dot(a, b, preferred_element_type=F32)


def _dot_nt(a, b):
    return lax.dot_general(a, b, (((1,), (1,)), ((), ())), preferred_element_type=F32)


def _rms_rows(x, g):
    return x * lax.rsqrt(jnp.mean(x * x, axis=-1, keepdims=True) + EPS) * g


def _params(n_grid_axes):
    return pltpu.CompilerParams(
        dimension_semantics=("arbitrary",) * n_grid_axes,
        vmem_limit_bytes=VMEM_LIMIT_BYTES,
    )


def _const_spec(shape):
    nd = len(shape)
    return pl.BlockSpec(shape, lambda *_: (0,) * nd, pipeline_mode=pl.Buffered(1))


def _row_tile(n_rows, want):
    t = min(want, n_rows)
    assert n_rows % t == 0, (n_rows, t)
    return t


def _ffn_kernel(*refs, has_mix, has_final):
    it = iter(refs)
    x_ref = next(it)
    if has_mix:
        attn_ref, conv_ref, wo_ref = next(it), next(it), next(it)
    g_ref, wgu_ref, wd_ref = next(it), next(it), next(it)
    gf_ref = next(it) if has_final else None
    out_ref = next(it)

    x = x_ref[...]
    if has_mix:
        n_attn = attn_ref.shape[1]
        x = x + _dot(attn_ref[...], wo_ref[:n_attn, :]) + _dot(conv_ref[...], wo_ref[n_attn:, :])
    h = _rms_rows(x, g_ref[...]).astype(BF16)
    d_ff = wd_ref.shape[0]
    a = _dot(h, wgu_ref[:, :d_ff])
    b = _dot(h, wgu_ref[:, d_ff:])
    act = (a * jax.nn.sigmoid(a) * b).astype(BF16)
    y = x + 0.5 * _dot(act, wd_ref[...])
    if has_final:
        y = _rms_rows(y, gf_ref[...])
    out_ref[...] = y


def _ffn(x, g, w_gu, wd, mix=None, g_final=None, name="ffn"):
    T, D = x.shape
    tm = _row_tile(T, FFN_ROW_TILE)
    row = lambda w: pl.BlockSpec((tm, w), lambda i: (i, 0))
    args, specs = [x], [row(D)]
    if mix is not None:
        attn, conv, w_out = mix
        args += [attn, conv, w_out]
        specs += [row(attn.shape[1]), row(conv.shape[1]), _const_spec(w_out.shape)]
    args += [g, w_gu, wd]
    specs += [_const_spec((1, D)), _const_spec(w_gu.shape), _const_spec(wd.shape)]
    if g_final is not None:
        args.append(g_final)
        specs.append(_const_spec((1, D)))
    return pl.pallas_call(
        functools.partial(_ffn_kernel, has_mix=mix is not None, has_final=g_final is not None),
        grid=(T // tm,),
        in_specs=specs,
        out_specs=row(D),
        out_shape=jax.ShapeDtypeStruct((T, D), F32),
        compiler_params=_params(1),
        name=name,
    )(*args)


def _group_rms(y, g, ones_bd):
    y2 = y * y
    hi = y2.astype(BF16)
    lo = (y2 - hi.astype(F32)).astype(BF16)
    ss = _dot(hi, ones_bd) + _dot(lo, ones_bd)
    return y * lax.rsqrt(ss * (1.0 / HEAD_DIM) + EPS) * g


def _alibi_query_lanes():
    c = jnp.asarray([s * LOG2E for s in ALIBI_SLOPES], F32)
    pieces, rest = [], c
    for _ in range(ALIBI_PIECES):
        pieces.append(rest.astype(BF16).astype(F32))
        rest = rest - pieces[-1]
    pieces = jnp.stack(pieces + pieces, axis=1)
    return jnp.zeros((N_HEADS, V_DIM), F32).at[:, HEAD_DIM:HEAD_DIM + 2 * ALIBI_PIECES].set(pieces)


def _proj_kernel(*refs, key_tile):
    if key_tile is None:
        x_ref, g_ref, w_ref, gq_ref, gk_ref, ones_ref, qs_ref, k32_ref, v32_ref, u_ref = refs
    else:
        (x_ref, g_ref, w_ref, gq_ref, gk_ref, ones_ref, qext_ref,
         qs_ref, ks_ref, vt_ref, k32_ref, v32_ref, u_ref) = refs
    tm = x_ref.shape[0]
    h = _rms_rows(x_ref[...], g_ref[...]).astype(BF16)
    z = _dot(h, w_ref[...])
    ones_bd = ones_ref[...]
    qn = _group_rms(z[:, :Q_WIDTH], gq_ref[...], ones_bd) * (HEAD_DIM ** -0.5 * LOG2E)
    kn = _group_rms(z[:, Q_WIDTH:2 * Q_WIDTH], gk_ref[...], ones_bd)
    v = z[:, 2 * Q_WIDTH:2 * Q_WIDTH + ATTN_WIDTH]
    conv_ch = (z.shape[1] - 2 * Q_WIDTH - ATTN_WIDTH) // 2
    a = z[:, 2 * Q_WIDTH + ATTN_WIDTH:2 * Q_WIDTH + ATTN_WIDTH + conv_ch]
    gate = z[:, 2 * Q_WIDTH + ATTN_WIDTH + conv_ch:]
    k32_ref[...] = kn
    for hh in range(N_HEADS):
        v32_ref[pl.ds(hh, tm, stride=N_HEADS), :] = v[:, V_DIM * hh:V_DIM * (hh + 1)]
    u_ref[...] = a * jax.nn.sigmoid(gate)

    lane = lax.broadcasted_iota(jnp.int32, (tm, V_DIM), 1)
    low = lane < HEAD_DIM
    if key_tile is None:
        for hh in range(N_HEADS):
            qh = qn[:, V_DIM * hh:V_DIM * (hh + 1)]
            qs_ref[hh, 0] = jnp.where(low, qh, 0.0).astype(BF16)
            qs_ref[hh, 1] = jnp.where(low, 0.0, qh).astype(BF16)
        return

    row = pl.program_id(0) * tm + lax.broadcasted_iota(jnp.int32, (tm, V_DIM), 0)
    off = row % key_tile
    lo = off % KEY_OFFSET_LO
    in_hi = (lane >= HEAD_DIM) & (lane < HEAD_DIM + ALIBI_PIECES)
    in_lo = (lane >= HEAD_DIM + ALIBI_PIECES) & (lane < HEAD_DIM + 2 * ALIBI_PIECES)
    k_ext = jnp.where(in_hi, off - lo, jnp.where(in_lo, lo, 0)).astype(F32)
    for hh in range(N_HEADS):
        sl = slice(V_DIM * hh, V_DIM * (hh + 1))
        q_ext = qext_ref[hh:hh + 1, :]
        for x, ext, dst in ((qn[:, sl], q_ext, qs_ref), (kn[:, sl], k_ext, ks_ref)):
            dst[hh, 0] = jnp.where(low, x, ext).astype(BF16)
            dst[hh, 1] = jnp.where(low, pltpu.roll(x, HEAD_DIM, axis=1), ext).astype(BF16)
        vt_ref[hh] = v[:, sl].T.astype(BF16)


def _project(x1, g_mix, w_in, gq_t, gk_t, ones_bd, key_tile=None):
    T, D = x1.shape
    C = w_in.shape[1]
    conv_ch = (C - 2 * Q_WIDTH - ATTN_WIDTH) // 2
    tm = _row_tile(T, FFN_ROW_TILE)
    row = lambda w: pl.BlockSpec((tm, w), lambda i: (i, 0))
    per_map = pl.BlockSpec((N_HEADS, 2, tm, V_DIM), lambda i: (0, 0, i, 0))
    per_map_shape = jax.ShapeDtypeStruct((N_HEADS, 2, T, V_DIM), BF16)
    args = [x1, g_mix, w_in, gq_t, gk_t, ones_bd]
    in_specs = [row(D), _const_spec((1, D)), _const_spec((D, C)), _const_spec((1, Q_WIDTH)),
                _const_spec((1, Q_WIDTH)), _const_spec((Q_WIDTH, Q_WIDTH))]
    out_specs, out_shape = [per_map], [per_map_shape]
    if key_tile is not None:
        args.append(_alibi_query_lanes())
        in_specs.append(_const_spec((N_HEADS, V_DIM)))
        out_specs += [per_map, pl.BlockSpec((N_HEADS, V_DIM, tm), lambda i: (0, 0, i))]
        out_shape += [per_map_shape, jax.ShapeDtypeStruct((N_HEADS, V_DIM, T), BF16)]
    out_specs += [row(Q_WIDTH), pl.BlockSpec((tm * N_HEADS, V_DIM), lambda i: (i, 0)), row(conv_ch)]
    out_shape += [jax.ShapeDtypeStruct(s, F32) for s in ((T, Q_WIDTH), (T * N_HEADS, V_DIM), (T, conv_ch))]
    return pl.pallas_call(
        functools.partial(_proj_kernel, key_tile=key_tile),
        grid=(T // tm,),
        in_specs=in_specs,
        out_specs=out_specs,
        out_shape=out_shape,
        compiler_params=_params(1),
        name="proj",
    )(*args)


def _diff_lambda(lam_ref, lam_init):
    lp = lam_ref[...]
    s1 = jnp.sum(lp[0:1] * lp[1:2], axis=-1, keepdims=True)
    s2 = jnp.sum(lp[2:3] * lp[3:4], axis=-1, keepdims=True)
    return jnp.exp(s1) - jnp.exp(s2) + lam_init


def _head_out(o1, o2, lam, g_sub, lam_init):
    d = o1 - lam * o2
    return _rms_rows(d, g_sub) * (1.0 - lam_init)


def _prompt_attn_kernel(qs_ref, qnext_ref, ks_ref, vt_ref, lam_ref, gsub_ref, out_ref,
                        acc_ref, m_ref, l_ref, sa_ref, sb_ref, maxa_ref, maxb_ref, p_ref, a_ref, base_ref,
                        *, tq, tk, lam_init):
    hh = pl.program_id(0)
    qi = pl.program_id(1)
    cols = 2 * tq
    slope2 = jnp.float32(0.0)
    for h_static, sl in enumerate(ALIBI_SLOPES):
        slope2 = jnp.where(hh == h_static, jnp.float32(sl * LOG2E), slope2)

    acc_ref[...] = jnp.zeros(acc_ref.shape, F32)
    m_ref[...] = jnp.full(m_ref.shape, NEG_INF, F32)
    l_ref[...] = jnp.zeros(l_ref.shape, F32)

    q_start = qi * tq
    n_full = q_start // tk

    def tile_start(j):
        return pl.multiple_of(j * tk, tk)

    n_blocks = cols // QUERY_BLOCK

    def scores(j, buf, c, q_ref=qs_ref):
        s_ref, max_ref = buf
        which, first = divmod(c * QUERY_BLOCK, tq)
        q = q_ref[which, first:first + QUERY_BLOCK, :]
        s = _dot_nt(ks_ref[which, pl.ds(tile_start(j), tk), :], q)
        s_ref[c] = s
        max_ref[c] = jnp.max(s, axis=0, keepdims=True)

    def softmax_pv(j, buf, c, n_keys=None):
        s_ref, max_ref = buf
        if n_keys is None:
            keys, tile_max = slice(0, tk), max_ref[c]
        else:
            keys = slice(0, n_keys)
            tile_max = jnp.max(s_ref[c, keys, :], axis=0, keepdims=True)
        t_off = slope2 * (j * tk - q_start).astype(F32)
        m = m_ref[c]
        m_new = jnp.maximum(m, tile_max + t_off)
        alpha = jnp.exp2(m - m_new)
        p = jnp.exp2(s_ref[c, keys, :] - (m_new - t_off))
        l_ref[c] = alpha * l_ref[c] + jnp.sum(p, axis=0, keepdims=True)
        m_ref[c] = m_new
        p_ref[c, keys, :] = p.astype(BF16)
        a_ref[c] = alpha
        vt = vt_ref[:, pl.ds(tile_start(j), keys.stop)]
        acc_ref[c] = a_ref[c] * acc_ref[c] + _dot(vt, p_ref[c, keys, :])

    def mask_own(buf, c, first_key):
        s_ref = buf[0]
        keys = slice(first_key, first_key + QUERY_BLOCK)
        ko = lax.broadcasted_iota(jnp.int32, (QUERY_BLOCK, QUERY_BLOCK), 0)
        qo = lax.broadcasted_iota(jnp.int32, (QUERY_BLOCK, QUERY_BLOCK), 1)
        after = (-2.0 * slope2) * jnp.maximum(ko - qo, 0).astype(F32)
        visible = (ko // CHUNK) <= (qo // CHUNK)
        s_ref[c, keys, :] = jnp.where(visible, s_ref[c, keys, :] + after, NEG_INF)

    sets = ((sa_ref, maxa_ref), (sb_ref, maxb_ref))

    @pl.when(qi == 0)
    def _():
        base_ref[0] = 0
        for c in range(n_blocks):
            scores(0, sets[0], c)

    base = base_ref[0]

    def step(j, cur, other):
        for c in range(n_blocks):
            scores(j + 1, other, c)
            softmax_pv(j, cur, c)

    def body(j, carry):
        for parity in range(2):
            @pl.when((j + base) % 2 == parity)
            def _(parity=parity):
                step(j, sets[parity], sets[1 - parity])

        return carry

    lax.fori_loop(0, n_full, body, 0)

    def last(cur, other):
        for sub in range(tk // tq):
            @pl.when(q_start - n_full * tk == sub * tq)
            def _(sub=sub):
                for c in range(n_blocks):
                    if other is not None:
                        scores(0, other, c, qnext_ref)
                    own = sub * tq + (c * QUERY_BLOCK) % tq
                    mask_own(cur, c, own)
                    softmax_pv(n_full, cur, c, own + QUERY_BLOCK)

    has_next = qi + 1 < pl.num_programs(1)
    for parity in range(2):
        @pl.when(jnp.logical_and((n_full + base) % 2 == parity, has_next))
        def _(parity=parity):
            last(sets[parity], sets[1 - parity])
            base_ref[0] = 1 - parity

        @pl.when(jnp.logical_and((n_full + base) % 2 == parity, jnp.logical_not(has_next)))
        def _(parity=parity):
            last(sets[parity], None)

    lam = _diff_lambda(lam_ref, lam_init)
    per_map = tq // QUERY_BLOCK
    ot = [acc_ref[c] * (1.0 / l_ref[c]) for c in range(n_blocks)]
    dt = jnp.concatenate([ot[c] - lam * ot[per_map + c] for c in range(per_map)], axis=1)
    yt = dt * lax.rsqrt(jnp.mean(dt * dt, axis=0, keepdims=True) + EPS)
    out_ref[...] = (yt.T * gsub_ref[...] * (1.0 - lam_init)).astype(out_ref.dtype)


def _prompt_tiles(T):
    tk = _row_tile(T, ATTN_KEY_TILE)
    tq = _row_tile(tk, ATTN_QUERY_TILE)
    assert tq % QUERY_BLOCK == 0 and QUERY_BLOCK % CHUNK == 0
    assert tk // KEY_OFFSET_LO <= 256
    return tq, tk


def _prompt_attention(qs, ks, vt, lam_rows, g_sub, lam_init, tq, tk):
    T = ks.shape[2]
    n_q = T // tq
    n_blocks = 2 * tq // QUERY_BLOCK
    return pl.pallas_call(
        functools.partial(_prompt_attn_kernel, tq=tq, tk=tk, lam_init=lam_init),
        grid=(N_HEADS, n_q),
        in_specs=[
            pl.BlockSpec((None, 2, tq, V_DIM), lambda h, i: (h, 0, i, 0)),
            pl.BlockSpec((None, 2, tq, V_DIM), lambda h, i: (h, 0, jnp.minimum(i + 1, n_q - 1), 0)),
            pl.BlockSpec((None, 2, T, V_DIM), lambda h, i: (h, 0, 0, 0)),
            pl.BlockSpec((None, V_DIM, T), lambda h, i: (h, 0, 0)),
            pl.BlockSpec((4, HEAD_DIM), lambda h, i: (0, 0)),
            pl.BlockSpec((1, V_DIM), lambda h, i: (0, 0)),
        ],
        out_specs=pl.BlockSpec((tq, V_DIM), lambda h, i: (i, h)),
        out_shape=jax.ShapeDtypeStruct((T, ATTN_WIDTH), BF16),
        scratch_shapes=[
            pltpu.VMEM((n_blocks, V_DIM, QUERY_BLOCK), F32),
            pltpu.VMEM((n_blocks, 1, QUERY_BLOCK), F32),
            pltpu.VMEM((n_blocks, 1, QUERY_BLOCK), F32),
            pltpu.VMEM((n_blocks, tk, QUERY_BLOCK), F32),
            pltpu.VMEM((n_blocks, tk, QUERY_BLOCK), F32),
            pltpu.VMEM((n_blocks, 1, QUERY_BLOCK), F32),
            pltpu.VMEM((n_blocks, 1, QUERY_BLOCK), F32),
            pltpu.VMEM((n_blocks, tk, QUERY_BLOCK), BF16),
            pltpu.VMEM((n_blocks, 1, QUERY_BLOCK), F32),
            pltpu.SMEM((1,), jnp.int32),
        ],
        compiler_params=_params(2),
        name="prompt_attn",
    )(qs, qs, ks, vt, lam_rows, g_sub)


def _sample_attn_kernel(qs_ref, kn_ref, vn_ref, ckt_ref, cv_ref, lam_ref, gsub_ref, out_ref,
                        m_ref, l_ref, acc_ref, *, t_new, past, tk, lam_init):
    grp = 2 * t_new
    rows = N_HEADS * grp
    blocks = []
    for hh in range(N_HEADS):
        qh = qs_ref[hh].reshape(grp, V_DIM)
        z = jnp.zeros((grp, V_DIM), BF16)
        blocks.append(jnp.concatenate([qh if c == hh else z for c in range(N_HEADS)], axis=1))
    q = jnp.concatenate(blocks, axis=0)

    r = lax.broadcasted_iota(jnp.int32, (rows, 1), 0)
    q_pos = past + (r % t_new)
    head = r // grp
    slope2 = jnp.zeros((rows, 1), F32)
    for h_static, sl in enumerate(ALIBI_SLOPES):
        slope2 = jnp.where(head == h_static, jnp.float32(sl * LOG2E), slope2)

    m_ref[...] = jnp.full(m_ref.shape, NEG_INF, F32)
    l_ref[...] = jnp.zeros(l_ref.shape, F32)
    acc_ref[...] = jnp.zeros(acc_ref.shape, F32)

    def step(qk, head_values, k_start, n_keys):
        k_pos = k_start + lax.broadcasted_iota(jnp.int32, (1, n_keys), 1)
        bias = -slope2 * jnp.abs(q_pos - k_pos).astype(F32)
        visible = (k_pos // CHUNK) <= (q_pos // CHUNK)
        s = jnp.where(visible, qk + bias, NEG_INF)
        m_prev = m_ref[...]
        m_new = jnp.maximum(m_prev, jnp.max(s, axis=1, keepdims=True))
        alpha = jnp.exp2(m_prev - m_new)
        p = jnp.exp2(s - m_new)
        l_ref[...] = alpha * l_ref[...] + jnp.sum(p, axis=1, keepdims=True)
        m_ref[...] = m_new
        p = p.astype(BF16)
        for hh in range(N_HEADS):
            mine = slice(hh * grp, (hh + 1) * grp)
            acc_ref[mine, :] = alpha[mine] * acc_ref[mine, :] + _dot(p[mine], head_values(hh).astype(BF16))

    for c in range(past // tk):
        step(_dot(q, ckt_ref[:, c * tk:(c + 1) * tk].astype(BF16)),
             lambda hh, c=c: cv_ref[pl.ds(c * tk * N_HEADS + hh, tk, stride=N_HEADS), :],
             c * tk, tk)
    step(_dot_nt(q, kn_ref[...].astype(BF16)), lambda hh: vn_ref[pl.ds(hh, t_new, stride=N_HEADS), :],
         past, t_new)

    o = acc_ref[...] / l_ref[...]
    lam = _diff_lambda(lam_ref, lam_init)
    outs = []
    for hh in range(N_HEADS):
        o1 = o[hh * grp:hh * grp + t_new]
        o2 = o[hh * grp + t_new:(hh + 1) * grp]
        outs.append(_head_out(o1, o2, lam, gsub_ref[...], lam_init))
    out_ref[...] = jnp.concatenate(outs, axis=1).astype(out_ref.dtype)


def _sample_attention(qs, k_new, v_new, cache_kt, cache_v, lam_rows, g_sub, lam_init, n_streams):
    t_new = k_new.shape[0] // n_streams
    past = cache_kt.shape[2]
    tk = _row_tile(past, SAMPLE_KEY_CHUNK)
    rows = N_HEADS * 2 * t_new
    return pl.pallas_call(
        functools.partial(_sample_attn_kernel, t_new=t_new, past=past, tk=tk, lam_init=lam_init),
        grid=(n_streams,),
        in_specs=[
            pl.BlockSpec((N_HEADS, 2, t_new, V_DIM), lambda b: (0, 0, b, 0)),
            pl.BlockSpec((t_new, Q_WIDTH), lambda b: (b, 0)),
            pl.BlockSpec((t_new * N_HEADS, V_DIM), lambda b: (b, 0)),
            pl.BlockSpec((None, Q_WIDTH, past), lambda b: (b, 0, 0)),
            pl.BlockSpec((None, past * N_HEADS, V_DIM), lambda b: (b, 0, 0)),
            pl.BlockSpec((4, HEAD_DIM), lambda b: (0, 0)),
            pl.BlockSpec((1, V_DIM), lambda b: (0, 0)),
        ],
        out_specs=pl.BlockSpec((t_new, ATTN_WIDTH), lambda b: (b, 0)),
        out_shape=jax.ShapeDtypeStruct((n_streams * t_new, ATTN_WIDTH), BF16),
        scratch_shapes=[
            pltpu.VMEM((rows, 1), F32),
            pltpu.VMEM((rows, 1), F32),
            pltpu.VMEM((rows, V_DIM), F32),
        ],
        compiler_params=_params(1),
        name="sample_attn",
    )(qs, k_new, v_new, cache_kt, cache_v, lam_rows, g_sub)


def _conv_kernel(u_ref, halo_ref, init_ref, w_ref, b_ref, g_ref, beta_ref, out_ref,
                 xp_ref, shift_ref, y_ref, *, tm):
    i = pl.program_id(1)
    hist = jnp.where(i == 0, init_ref[...], halo_ref[...])
    xp_ref[0:HALO_ROWS, :] = hist
    xp_ref[HALO_ROWS:HALO_ROWS + tm, :] = u_ref[...]
    first = HALO_ROWS - (CONV_WIDTH - 1)
    n_lane_blocks = u_ref.shape[-1] // V7X_LANES
    taps_of = lambda phase: (CONV_WIDTH - 1 - phase) // V7X_SUBLANES + 1

    for phase in range(V7X_SUBLANES):
        n_win = tm + V7X_SUBLANES * (taps_of(phase) - 1)
        for c in range(n_lane_blocks):
            shift_ref[phase, c, 0:n_win, :] = xp_ref[first + phase:first + phase + n_win,
                                                     c * V7X_LANES:(c + 1) * V7X_LANES]

    rows = min(tm, CONV_ROW_BLOCK)
    n_row_blocks = tm // rows

    def block(t, carry):
        c = t // n_row_blocks
        r0 = pl.multiple_of((t % n_row_blocks) * rows, rows)
        acc = jnp.zeros((rows, V7X_LANES), F32) + b_ref[c]
        for phase in range(V7X_SUBLANES):
            for a in range(taps_of(phase)):
                w = V7X_SUBLANES * a + phase
                acc = acc + shift_ref[phase, c, pl.ds(r0 + V7X_SUBLANES * a, rows), :] * w_ref[c, w:w + 1, :]
        y_ref[c, pl.ds(r0, rows), :] = acc
        return carry

    lax.fori_loop(0, n_lane_blocks * n_row_blocks, block, 0)
    y = jnp.concatenate([y_ref[c] for c in range(n_lane_blocks)], axis=1)
    mu = jnp.mean(y, axis=-1, keepdims=True)
    d = y - mu
    var = jnp.mean(d * d, axis=-1, keepdims=True)
    yn = d * lax.rsqrt(var + EPS) * g_ref[...] + beta_ref[...]
    out_ref[...] = (yn * jax.nn.sigmoid(yn)).astype(out_ref.dtype)


def _conv_branch(u, init_hist, w_dw, b_dw, g_ln, b_ln):
    B, T, C = u.shape
    tm = _row_tile(T, CONV_ROW_TILE)
    assert tm % HALO_ROWS == 0
    per = tm // HALO_ROWS
    n_lane_blocks = C // V7X_LANES
    vec = lambda: pl.BlockSpec((1, C), lambda b, i: (0, 0))
    w_blocks = w_dw.reshape(CONV_WIDTH, n_lane_blocks, V7X_LANES).swapaxes(0, 1)
    b_blocks = b_dw.reshape(n_lane_blocks, 1, V7X_LANES)
    return pl.pallas_call(
        functools.partial(_conv_kernel, tm=tm),
        grid=(B, T // tm),
        in_specs=[
            pl.BlockSpec((None, tm, C), lambda b, i: (b, i, 0)),
            pl.BlockSpec((None, HALO_ROWS, C), lambda b, i: (b, jnp.maximum(i * per - 1, 0), 0)),
            pl.BlockSpec((None, HALO_ROWS, C), lambda b, i: (b, 0, 0)),
            pl.BlockSpec((n_lane_blocks, CONV_WIDTH, V7X_LANES), lambda b, i: (0, 0, 0)),
            pl.BlockSpec((n_lane_blocks, 1, V7X_LANES), lambda b, i: (0, 0, 0)),
            vec(), vec(),
        ],
        out_specs=pl.BlockSpec((None, tm, C), lambda b, i: (b, i, 0)),
        out_shape=jax.ShapeDtypeStruct((B, T, C), BF16),
        scratch_shapes=[
            pltpu.VMEM((HALO_ROWS + tm, C), F32),
            pltpu.VMEM((V7X_SUBLANES, n_lane_blocks, tm + HALO_ROWS, V7X_LANES), F32),
            pltpu.VMEM((n_lane_blocks, tm, V7X_LANES), F32),
        ],
        compiler_params=_params(2),
        name="conv",
    )(u, u, init_hist, w_blocks, b_blocks, g_ln, b_ln)


def _layer(x, n_streams, attend, key_tile, conv_hist, lam_init, p):
    T = x.shape[0] // n_streams
    x1 = _ffn(x, p["g_ffn1"], p["w1gu"], p["w1d"], name="ffn1")
    proj = _project(x1, p["g_mix"], p["w_in"], p["gq_t"], p["gk_t"], p["ones_bd"], key_tile)
    k32, v32, u = proj[-3:]
    attn = attend(*proj)
    conv_ch = u.shape[1]
    hist = jnp.pad(conv_hist, ((0, 0), (HALO_ROWS - conv_hist.shape[1], 0), (0, 0)))
    conv = _conv_branch(u.reshape(n_streams, T, conv_ch), hist, p["w_dw"], p["b_dw"], p["g_ln"], p["b_ln"])
    y = _ffn(x1, p["g_ffn2"], p["w2gu"], p["w2d"],
             mix=(attn, conv.reshape(n_streams * T, conv_ch), p["w_out"]),
             g_final=p["g_final"], name="ffn2")
    return y, k32, v32, u


def kernel(x_prompt, x_sample, cache_k, cache_v, cache_conv, g_ffn1, w_ffn1_gu, w_ffn1_down, g_mix, w_in, g_q, g_k, lambda_q1, lambda_k1, lambda_q2, lambda_k2, g_sub, w_dw, b_dw, g_conv_ln, b_conv_ln, w_out, g_ffn2, w_ffn2_gu, w_ffn2_down, g_final):
    depth = cache_k.shape[0]
    n_p, t_p, d_model = x_prompt.shape
    n_s, t_s, _ = x_sample.shape
    past = cache_k.shape[2]
    conv_ch = cache_conv.shape[-1]
    hist_rows = CONV_WIDTH - 1
    assert n_p == 1, "prompt attention kernel handles one prompt stream"
    assert t_p >= hist_rows and t_s >= hist_rows

    group = jnp.arange(Q_WIDTH) // HEAD_DIM
    ones_bd = (group[:, None] == group[None, :]).astype(BF16)
    row = lambda a: a.reshape(1, -1)

    yp = x_prompt.reshape(n_p * t_p, d_model)
    ys = x_sample.reshape(n_s * t_s, d_model)
    outs = [[] for _ in range(6)]
    for l in range(depth):
        lam_init = 0.8 - 0.6 * math.exp(-0.3 * l)
        lam_rows = jnp.stack([lambda_q1[l], lambda_k1[l], lambda_q2[l], lambda_k2[l]])
        p = dict(
            g_ffn1=row(g_ffn1[l]), w1gu=w_ffn1_gu[l].astype(BF16),
            w1d=w_ffn1_down[l].astype(BF16), g_mix=row(g_mix[l]), w_in=w_in[l].astype(BF16),
            gq_t=row(jnp.tile(g_q[l], 2 * N_HEADS)), gk_t=row(jnp.tile(g_k[l], 2 * N_HEADS)), ones_bd=ones_bd,
            w_dw=w_dw[l], b_dw=row(b_dw[l]), g_ln=row(g_conv_ln[l]), b_ln=row(b_conv_ln[l]),
            w_out=w_out[l].astype(BF16), g_ffn2=row(g_ffn2[l]), w2gu=w_ffn2_gu[l].astype(BF16),
            w2d=w_ffn2_down[l].astype(BF16), g_final=row(g_final[l]),
        )
        g_sub_row = row(g_sub[l])

        q_tile, key_tile = _prompt_tiles(t_p)

        def attend_prompt(qs, ks, vt, k32, v32, u):
            return _prompt_attention(qs, ks, vt, lam_rows, g_sub_row, lam_init, q_tile, key_tile)

        def attend_sample(qs, k32, v32, u, l=l):
            ck = jnp.swapaxes(cache_k[l].reshape(n_s, past, Q_WIDTH), 1, 2)
            cv = cache_v[l].reshape(n_s, past * N_HEADS, V_DIM)
            return _sample_attention(qs, k32, v32, ck, cv, lam_rows, g_sub_row, lam_init, n_s)

        zero_hist = jnp.zeros((n_p, hist_rows, conv_ch), F32)
        yp, kp, vp, up = _layer(yp, n_p, attend_prompt, key_tile, zero_hist, lam_init, p)
        ys, ks, vs, us = _layer(ys, n_s, attend_sample, None, cache_conv[l], lam_init, p)
        outs[0].append(kp.reshape(n_p, t_p, N_HEADS, 2, HEAD_DIM))
        outs[1].append(vp.reshape(n_p, t_p, N_HEADS, V_DIM))
        outs[2].append(up.reshape(n_p, t_p, conv_ch)[:, t_p - hist_rows:])
        outs[3].append(ks.reshape(n_s, t_s, N_HEADS, 2, HEAD_DIM))
        outs[4].append(vs.reshape(n_s, t_s, N_HEADS, V_DIM))
        outs[5].append(us.reshape(n_s, t_s, conv_ch)[:, t_s - hist_rows:])

    return (yp.reshape(n_p, t_p, d_model), ys.reshape(n_s, t_s, d_model),
            *[jnp.stack(o, axis=0) for o in outs])
```

```python
import functools
import math

import jax
import jax.numpy as jnp
from jax import lax
from jax.experimental import pallas as pl
from jax.experimental.pallas import tpu as pltpu

N_HEADS = 4
HEAD_DIM = 64
V_DIM = 2 * HEAD_DIM
Q_WIDTH = N_HEADS * 2 * HEAD_DIM
ATTN_WIDTH = N_HEADS * V_DIM
CHUNK = 64
CONV_WIDTH = 31
EPS = 1e-6
NEG_INF = -1e30
ALIBI_SLOPES = tuple(2.0 ** (-8.0 * (h + 1) / N_HEADS) for h in range(N_HEADS))
LOG2E = math.log2(math.e)

V7X_LANES = 128
V7X_SUBLANES = 8
V7X_VMEM_BYTES = 64 * 1024 * 1024
VMEM_LIMIT_BYTES = V7X_VMEM_BYTES - 8 * 1024 * 1024

FFN_ROW_TILE = 512
CONV_ROW_TILE = 256
ATTN_KEY_TILE = 1024
ATTN_QUERY_TILE = 1024
SAMPLE_KEY_CHUNK = 1024
QUERY_BLOCK = 256
ALIBI_PIECES = 3
KEY_OFFSET_LO = 32
CONV_ROW_BLOCK = 128
HALO_ROWS = 32

F32 = jnp.float32
BF16 = jnp.bfloat16


def _dot(a, b):
    return jnp.dot(a, b, preferred_element_type=F32)


def _dot_nt(a, b):
    return lax.dot_general(a, b, (((1,), (1,)), ((), ())), preferred_element_type=F32)


def _rms_rows(x, g):
    return x * lax.rsqrt(jnp.mean(x * x, axis=-1, keepdims=True) + EPS) * g


def _params(n_grid_axes):
    return pltpu.CompilerParams(
        dimension_semantics=("arbitrary",) * n_grid_axes,
        vmem_limit_bytes=VMEM_LIMIT_BYTES,
    )


def _const_spec(shape):
    nd = len(shape)
    return pl.BlockSpec(shape, lambda *_: (0,) * nd, pipeline_mode=pl.Buffered(1))


def _row_tile(n_rows, want):
    t = min(want, n_rows)
    assert n_rows % t == 0, (n_rows, t)
    return t


def _ffn_kernel(*refs, starts, has_mix, has_final):
    n_src = len(starts) - 1
    it = iter(refs)
    x_refs = [next(it) for _ in range(n_src)]
    if has_mix:
        attn_refs = [next(it) for _ in range(n_src)]
        conv_refs = [next(it) for _ in range(n_src)]
        wo_ref = next(it)
    g_ref, wgu_ref, wd_ref = next(it), next(it), next(it)
    gf_ref = next(it) if has_final else None
    out_refs = [next(it) for _ in range(n_src)]
    step = pl.program_id(0)

    for s in range(n_src):
        @pl.when(jnp.logical_and(step >= starts[s], step < starts[s + 1]))
        def _(s=s):
            x = x_refs[s][...]
            if has_mix:
                n_attn = attn_refs[s].shape[1]
                x = x + _dot(attn_refs[s][...], wo_ref[:n_attn, :]) + _dot(conv_refs[s][...], wo_ref[n_attn:, :])
            h = _rms_rows(x, g_ref[...]).astype(BF16)
            d_ff = wd_ref.shape[0]
            a = _dot(h, wgu_ref[:, :d_ff])
            b = _dot(h, wgu_ref[:, d_ff:])
            act = (a * jax.nn.sigmoid(a) * b).astype(BF16)
            y = x + 0.5 * _dot(act, wd_ref[...])
            if has_final:
                y = _rms_rows(y, gf_ref[...])
            out_refs[s][...] = y


def _ffn(xs, g, w_gu, wd, mix=None, g_final=None, name="ffn"):
    D = xs[0].shape[1]
    tm = _row_tile(min(x.shape[0] for x in xs), FFN_ROW_TILE)
    starts = [0]
    for x in xs:
        assert x.shape[0] % tm == 0
        starts.append(starts[-1] + x.shape[0] // tm)

    def rows(s, width):
        lo, n = starts[s], starts[s + 1] - starts[s]
        return pl.BlockSpec((tm, width), lambda i: (jnp.clip(i - lo, 0, n - 1), 0))

    n_src = len(xs)
    args, specs = list(xs), [rows(s, D) for s in range(n_src)]
    if mix is not None:
        attns, convs, w_out = mix
        args += [*attns, *convs, w_out]
        specs += [rows(s, a.shape[1]) for s, a in enumerate(attns)]
        specs += [rows(s, c.shape[1]) for s, c in enumerate(convs)]
        specs.append(_const_spec(w_out.shape))
    args += [g, w_gu, wd]
    specs += [_const_spec((1, D)), _const_spec(w_gu.shape), _const_spec(wd.shape)]
    if g_final is not None:
        args.append(g_final)
        specs.append(_const_spec((1, D)))
    return pl.pallas_call(
        functools.partial(_ffn_kernel, starts=tuple(starts), has_mix=mix is not None,
                          has_final=g_final is not None),
        grid=(starts[-1],),
        in_specs=specs,
        out_specs=[rows(s, D) for s in range(n_src)],
        out_shape=[jax.ShapeDtypeStruct(x.shape, F32) for x in xs],
        compiler_params=_params(1),
        name=name,
    )(*args)


def _group_rms(y, g, ones_bd):
    y2 = y * y
    hi = y2.astype(BF16)
    lo = (y2 - hi.astype(F32)).astype(BF16)
    ss = _dot(hi, ones_bd) + _dot(lo, ones_bd)
    return y * lax.rsqrt(ss * (1.0 / HEAD_DIM) + EPS) * g


def _alibi_query_lanes():
    c = jnp.asarray([s * LOG2E for s in ALIBI_SLOPES], F32)
    pieces, rest = [], c
    for _ in range(ALIBI_PIECES):
        pieces.append(rest.astype(BF16).astype(F32))
        rest = rest - pieces[-1]
    pieces = jnp.stack(pieces + pieces, axis=1)
    return jnp.zeros((N_HEADS, V_DIM), F32).at[:, HEAD_DIM:HEAD_DIM + 2 * ALIBI_PIECES].set(pieces)


def _proj_kernel(*refs, key_tile):
    if key_tile is None:
        x_ref, g_ref, w_ref, gq_ref, gk_ref, ones_ref, qs_ref, k32_ref, v32_ref, u_ref = refs
    else:
        (x_ref, g_ref, w_ref, gq_ref, gk_ref, ones_ref, qext_ref,
         qs_ref, ks_ref, vt_ref, k32_ref, v32_ref, u_ref) = refs
    tm = x_ref.shape[0]
    h = _rms_rows(x_ref[...], g_ref[...]).astype(BF16)
    z = _dot(h, w_ref[...])
    ones_bd = ones_ref[...]
    qn = _group_rms(z[:, :Q_WIDTH], gq_ref[...], ones_bd) * (HEAD_DIM ** -0.5 * LOG2E)
    kn = _group_rms(z[:, Q_WIDTH:2 * Q_WIDTH], gk_ref[...], ones_bd)
    v = z[:, 2 * Q_WIDTH:2 * Q_WIDTH + ATTN_WIDTH]
    conv_ch = (z.shape[1] - 2 * Q_WIDTH - ATTN_WIDTH) // 2
    a = z[:, 2 * Q_WIDTH + ATTN_WIDTH:2 * Q_WIDTH + ATTN_WIDTH + conv_ch]
    gate = z[:, 2 * Q_WIDTH + ATTN_WIDTH + conv_ch:]
    k32_ref[...] = kn
    for hh in range(N_HEADS):
        v32_ref[pl.ds(hh, tm, stride=N_HEADS), :] = v[:, V_DIM * hh:V_DIM * (hh + 1)]
    u_ref[...] = a * jax.nn.sigmoid(gate)

    lane = lax.broadcasted_iota(jnp.int32, (tm, V_DIM), 1)
    low = lane < HEAD_DIM
    if key_tile is None:
        for hh in range(N_HEADS):
            qh = qn[:, V_DIM * hh:V_DIM * (hh + 1)]
            qs_ref[hh, 0] = jnp.where(low, qh, 0.0).astype(BF16)
            qs_ref[hh, 1] = jnp.where(low, 0.0, qh).astype(BF16)
        return

    row = pl.program_id(0) * tm + lax.broadcasted_iota(jnp.int32, (tm, V_DIM), 0)
    off = row % key_tile
    lo = off % KEY_OFFSET_LO
    in_hi = (lane >= HEAD_DIM) & (lane < HEAD_DIM + ALIBI_PIECES)
    in_lo = (lane >= HEAD_DIM + ALIBI_PIECES) & (lane < HEAD_DIM + 2 * ALIBI_PIECES)
    k_ext = jnp.where(in_hi, off - lo, jnp.where(in_lo, lo, 0)).astype(F32)
    for hh in range(N_HEADS):
        sl = slice(V_DIM * hh, V_DIM * (hh + 1))
        q_ext = qext_ref[hh:hh + 1, :]
        for x, ext, dst in ((qn[:, sl], q_ext, qs_ref), (kn[:, sl], k_ext, ks_ref)):
            dst[hh, 0] = jnp.where(low, x, ext).astype(BF16)
            dst[hh, 1] = jnp.where(low, pltpu.roll(x, HEAD_DIM, axis=1), ext).astype(BF16)
        vt_ref[hh] = v[:, sl].T.astype(BF16)


def _project(x1, g_mix, w_in, gq_t, gk_t, ones_bd, key_tile=None):
    T, D = x1.shape
    C = w_in.shape[1]
    conv_ch = (C - 2 * Q_WIDTH - ATTN_WIDTH) // 2
    tm = _row_tile(T, FFN_ROW_TILE)
    row = lambda w: pl.BlockSpec((tm, w), lambda i: (i, 0))
    per_map = pl.BlockSpec((N_HEADS, 2, tm, V_DIM), lambda i: (0, 0, i, 0))
    per_map_shape = jax.ShapeDtypeStruct((N_HEADS, 2, T, V_DIM), BF16)
    args = [x1, g_mix, w_in, gq_t, gk_t, ones_bd]
    in_specs = [row(D), _const_spec((1, D)), _const_spec((D, C)), _const_spec((1, Q_WIDTH)),
                _const_spec((1, Q_WIDTH)), _const_spec((Q_WIDTH, Q_WIDTH))]
    out_specs, out_shape = [per_map], [per_map_shape]
    if key_tile is not None:
        args.append(_alibi_query_lanes())
        in_specs.append(_const_spec((N_HEADS, V_DIM)))
        out_specs += [per_map, pl.BlockSpec((N_HEADS, V_DIM, tm), lambda i: (0, 0, i))]
        out_shape += [per_map_shape, jax.ShapeDtypeStruct((N_HEADS, V_DIM, T), BF16)]
    out_specs += [row(Q_WIDTH), pl.BlockSpec((tm * N_HEADS, V_DIM), lambda i: (i, 0)), row(conv_ch)]
    out_shape += [jax.ShapeDtypeStruct(s, F32) for s in ((T, Q_WIDTH), (T * N_HEADS, V_DIM), (T, conv_ch))]
    return pl.pallas_call(
        functools.partial(_proj_kernel, key_tile=key_tile),
        grid=(T // tm,),
        in_specs=in_specs,
        out_specs=out_specs,
        out_shape=out_shape,
        compiler_params=_params(1),
        name="proj",
    )(*args)


def _diff_lambda(lam_ref, lam_init):
    lp = lam_ref[...]
    s1 = jnp.sum(lp[0:1] * lp[1:2], axis=-1, keepdims=True)
    s2 = jnp.sum(lp[2:3] * lp[3:4], axis=-1, keepdims=True)
    return jnp.exp(s1) - jnp.exp(s2) + lam_init


def _head_out(o1, o2, lam, g_sub, lam_init):
    d = o1 - lam * o2
    return _rms_rows(d, g_sub) * (1.0 - lam_init)


def _prompt_attn_kernel(qs_ref, qnext_ref, ks_ref, vt_ref, lam_ref, gsub_ref, out_ref,
                        acc_ref, m_ref, l_ref, sa_ref, sb_ref, maxa_ref, maxb_ref, p_ref, a_ref, base_ref,
                        *, tq, tk, lam_init):
    hh = pl.program_id(0)
    qi = pl.program_id(1)
    cols = 2 * tq
    slope2 = jnp.float32(0.0)
    for h_static, sl in enumerate(ALIBI_SLOPES):
        slope2 = jnp.where(hh == h_static, jnp.float32(sl * LOG2E), slope2)

    acc_ref[...] = jnp.zeros(acc_ref.shape, F32)
    m_ref[...] = jnp.full(m_ref.shape, NEG_INF, F32)
    l_ref[...] = jnp.zeros(l_ref.shape, F32)

    q_start = qi * tq
    n_full = q_start // tk

    def tile_start(j):
        return pl.multiple_of(j * tk, tk)

    n_blocks = cols // QUERY_BLOCK

    def scores(j, buf, c, q_ref=qs_ref):
        s_ref, max_ref = buf
        which, first = divmod(c * QUERY_BLOCK, tq)
        q = q_ref[which, first:first + QUERY_BLOCK, :]
        s = _dot_nt(ks_ref[which, pl.ds(tile_start(j), tk), :], q)
        s_ref[c] = s
        max_ref[c] = jnp.max(s, axis=0, keepdims=True)

    def softmax_pv(j, buf, c, n_keys=None):
        s_ref, max_ref = buf
        if n_keys is None:
            keys, tile_max = slice(0, tk), max_ref[c]
        else:
            keys = slice(0, n_keys)
            tile_max = jnp.max(s_ref[c, keys, :], axis=0, keepdims=True)
        t_off = slope2 * (j * tk - q_start).astype(F32)
        m = m_ref[c]
        m_new = jnp.maximum(m, tile_max + t_off)
        alpha = jnp.exp2(m - m_new)
        p = jnp.exp2(s_ref[c, keys, :] - (m_new - t_off))
        l_ref[c] = alpha * l_ref[c] + jnp.sum(p, axis=0, keepdims=True)
        m_ref[c] = m_new
        p_ref[c, keys, :] = p.astype(BF16)
        a_ref[c] = alpha
        vt = vt_ref[:, pl.ds(tile_start(j), keys.stop)]
        acc_ref[c] = a_ref[c] * acc_ref[c] + _dot(vt, p_ref[c, keys, :])

    def mask_own(buf, c, first_key):
        s_ref = buf[0]
        keys = slice(first_key, first_key + QUERY_BLOCK)
        ko = lax.broadcasted_iota(jnp.int32, (QUERY_BLOCK, QUERY_BLOCK), 0)
        qo = lax.broadcasted_iota(jnp.int32, (QUERY_BLOCK, QUERY_BLOCK), 1)
        after = (-2.0 * slope2) * jnp.maximum(ko - qo, 0).astype(F32)
        visible = (ko // CHUNK) <= (qo // CHUNK)
        s_ref[c, keys, :] = jnp.where(visible, s_ref[c, keys, :] + after, NEG_INF)

    sets = ((sa_ref, maxa_ref), (sb_ref, maxb_ref))

    @pl.when(qi == 0)
    def _():
        base_ref[0] = 0
        for c in range(n_blocks):
            scores(0, sets[0], c)

    base = base_ref[0]

    def step(j, cur, other):
        for c in range(n_blocks):
            scores(j + 1, other, c)
            softmax_pv(j, cur, c)

    def body(j, carry):
        for parity in range(2):
            @pl.when((j + base) % 2 == parity)
            def _(parity=parity):
                step(j, sets[parity], sets[1 - parity])

        return carry

    lax.fori_loop(0, n_full, body, 0)

    def last(cur, other):
        for sub in range(tk // tq):
            @pl.when(q_start - n_full * tk == sub * tq)
            def _(sub=sub):
                for c in range(n_blocks):
                    if other is not None:
                        scores(0, other, c, qnext_ref)
                    own = sub * tq + (c * QUERY_BLOCK) % tq
                    mask_own(cur, c, own)
                    softmax_pv(n_full, cur, c, own + QUERY_BLOCK)

    has_next = qi + 1 < pl.num_programs(1)
    for parity in range(2):
        @pl.when(jnp.logical_and((n_full + base) % 2 == parity, has_next))
        def _(parity=parity):
            last(sets[parity], sets[1 - parity])
            base_ref[0] = 1 - parity

        @pl.when(jnp.logical_and((n_full + base) % 2 == parity, jnp.logical_not(has_next)))
        def _(parity=parity):
            last(sets[parity], None)

    lam = _diff_lambda(lam_ref, lam_init)
    per_map = tq // QUERY_BLOCK
    ot = [acc_ref[c] * (1.0 / l_ref[c]) for c in range(n_blocks)]
    dt = jnp.concatenate([ot[c] - lam * ot[per_map + c] for c in range(per_map)], axis=1)
    yt = dt * lax.rsqrt(jnp.mean(dt * dt, axis=0, keepdims=True) + EPS)
    out_ref[...] = (yt.T * gsub_ref[...] * (1.0 - lam_init)).astype(out_ref.dtype)


def _prompt_tiles(T):
    tk = _row_tile(T, ATTN_KEY_TILE)
    tq = _row_tile(tk, ATTN_QUERY_TILE)
    assert tq % QUERY_BLOCK == 0 and QUERY_BLOCK % CHUNK == 0
    assert tk // KEY_OFFSET_LO <= 256
    return tq, tk


def _prompt_attention(qs, ks, vt, lam_rows, g_sub, lam_init, tq, tk):
    T = ks.shape[2]
    n_q = T // tq
    n_blocks = 2 * tq // QUERY_BLOCK
    return pl.pallas_call(
        functools.partial(_prompt_attn_kernel, tq=tq, tk=tk, lam_init=lam_init),
        grid=(N_HEADS, n_q),
        in_specs=[
            pl.BlockSpec((None, 2, tq, V_DIM), lambda h, i: (h, 0, i, 0)),
            pl.BlockSpec((None, 2, tq, V_DIM), lambda h, i: (h, 0, jnp.minimum(i + 1, n_q - 1), 0)),
            pl.BlockSpec((None, 2, T, V_DIM), lambda h, i: (h, 0, 0, 0)),
            pl.BlockSpec((None, V_DIM, T), lambda h, i: (h, 0, 0)),
            pl.BlockSpec((4, HEAD_DIM), lambda h, i: (0, 0)),
            pl.BlockSpec((1, V_DIM), lambda h, i: (0, 0)),
        ],
        out_specs=pl.BlockSpec((tq, V_DIM), lambda h, i: (i, h)),
        out_shape=jax.ShapeDtypeStruct((T, ATTN_WIDTH), BF16),
        scratch_shapes=[
            pltpu.VMEM((n_blocks, V_DIM, QUERY_BLOCK), F32),
            pltpu.VMEM((n_blocks, 1, QUERY_BLOCK), F32),
            pltpu.VMEM((n_blocks, 1, QUERY_BLOCK), F32),
            pltpu.VMEM((n_blocks, tk, QUERY_BLOCK), F32),
            pltpu.VMEM((n_blocks, tk, QUERY_BLOCK), F32),
            pltpu.VMEM((n_blocks, 1, QUERY_BLOCK), F32),
            pltpu.VMEM((n_blocks, 1, QUERY_BLOCK), F32),
            pltpu.VMEM((n_blocks, tk, QUERY_BLOCK), BF16),
            pltpu.VMEM((n_blocks, 1, QUERY_BLOCK), F32),
            pltpu.SMEM((1,), jnp.int32),
        ],
        compiler_params=_params(2),
        name="prompt_attn",
    )(qs, qs, ks, vt, lam_rows, g_sub)


def _sample_attn_kernel(qs_ref, kn_ref, vn_ref, ckt_ref, cv_ref, lam_ref, gsub_ref, out_ref,
                        m_ref, l_ref, acc_ref, *, t_new, past, tk, lam_init):
    grp = 2 * t_new
    rows = N_HEADS * grp
    blocks = []
    for hh in range(N_HEADS):
        qh = qs_ref[hh].reshape(grp, V_DIM)
        z = jnp.zeros((grp, V_DIM), BF16)
        blocks.append(jnp.concatenate([qh if c == hh else z for c in range(N_HEADS)], axis=1))
    q = jnp.concatenate(blocks, axis=0)

    r = lax.broadcasted_iota(jnp.int32, (rows, 1), 0)
    q_pos = past + (r % t_new)
    head = r // grp
    slope2 = jnp.zeros((rows, 1), F32)
    for h_static, sl in enumerate(ALIBI_SLOPES):
        slope2 = jnp.where(head == h_static, jnp.float32(sl * LOG2E), slope2)

    m_ref[...] = jnp.full(m_ref.shape, NEG_INF, F32)
    l_ref[...] = jnp.zeros(l_ref.shape, F32)
    acc_ref[...] = jnp.zeros(acc_ref.shape, F32)

    def step(qk, head_values, k_start, n_keys):
        k_pos = k_start + lax.broadcasted_iota(jnp.int32, (1, n_keys), 1)
        bias = -slope2 * jnp.abs(q_pos - k_pos).astype(F32)
        visible = (k_pos // CHUNK) <= (q_pos // CHUNK)
        s = jnp.where(visible, qk + bias, NEG_INF)
        m_prev = m_ref[...]
        m_new = jnp.maximum(m_prev, jnp.max(s, axis=1, keepdims=True))
        alpha = jnp.exp2(m_prev - m_new)
        p = jnp.exp2(s - m_new)
        l_ref[...] = alpha * l_ref[...] + jnp.sum(p, axis=1, keepdims=True)
        m_ref[...] = m_new
        p = p.astype(BF16)
        for hh in range(N_HEADS):
            mine = slice(hh * grp, (hh + 1) * grp)
            acc_ref[mine, :] = alpha[mine] * acc_ref[mine, :] + _dot(p[mine], head_values(hh).astype(BF16))

    for c in range(past // tk):
        step(_dot(q, ckt_ref[:, c * tk:(c + 1) * tk].astype(BF16)),
             lambda hh, c=c: cv_ref[pl.ds(c * tk * N_HEADS + hh, tk, stride=N_HEADS), :],
             c * tk, tk)
    step(_dot_nt(q, kn_ref[...].astype(BF16)), lambda hh: vn_ref[pl.ds(hh, t_new, stride=N_HEADS), :],
         past, t_new)

    o = acc_ref[...] / l_ref[...]
    lam = _diff_lambda(lam_ref, lam_init)
    outs = []
    for hh in range(N_HEADS):
        o1 = o[hh * grp:hh * grp + t_new]
        o2 = o[hh * grp + t_new:(hh + 1) * grp]
        outs.append(_head_out(o1, o2, lam, gsub_ref[...], lam_init))
    out_ref[...] = jnp.concatenate(outs, axis=1).astype(out_ref.dtype)


def _sample_attention(qs, k_new, v_new, cache_kt, cache_v, lam_rows, g_sub, lam_init, n_streams):
    t_new = k_new.shape[0] // n_streams
    past = cache_kt.shape[2]
    tk = _row_tile(past, SAMPLE_KEY_CHUNK)
    rows = N_HEADS * 2 * t_new
    return pl.pallas_call(
        functools.partial(_sample_attn_kernel, t_new=t_new, past=past, tk=tk, lam_init=lam_init),
        grid=(n_streams,),
        in_specs=[
            pl.BlockSpec((N_HEADS, 2, t_new, V_DIM), lambda b: (0, 0, b, 0)),
            pl.BlockSpec((t_new, Q_WIDTH), lambda b: (b, 0)),
            pl.BlockSpec((t_new * N_HEADS, V_DIM), lambda b: (b, 0)),
            pl.BlockSpec((None, Q_WIDTH, past), lambda b: (b, 0, 0)),
            pl.BlockSpec((None, past * N_HEADS, V_DIM), lambda b: (b, 0, 0)),
            pl.BlockSpec((4, HEAD_DIM), lambda b: (0, 0)),
            pl.BlockSpec((1, V_DIM), lambda b: (0, 0)),
        ],
        out_specs=pl.BlockSpec((t_new, ATTN_WIDTH), lambda b: (b, 0)),
        out_shape=jax.ShapeDtypeStruct((n_streams * t_new, ATTN_WIDTH), BF16),
        scratch_shapes=[
            pltpu.VMEM((rows, 1), F32),
            pltpu.VMEM((rows, 1), F32),
            pltpu.VMEM((rows, V_DIM), F32),
        ],
        compiler_params=_params(1),
        name="sample_attn",
    )(qs, k_new, v_new, cache_kt, cache_v, lam_rows, g_sub)


def _conv_kernel(u_ref, halo_ref, init_ref, w_ref, b_ref, g_ref, beta_ref, out_ref,
                 xp_ref, shift_ref, y_ref, *, tm):
    i = pl.program_id(1)
    hist = jnp.where(i == 0, init_ref[...], halo_ref[...])
    xp_ref[0:HALO_ROWS, :] = hist
    xp_ref[HALO_ROWS:HALO_ROWS + tm, :] = u_ref[...]
    first = HALO_ROWS - (CONV_WIDTH - 1)
    n_lane_blocks = u_ref.shape[-1] // V7X_LANES
    taps_of = lambda phase: (CONV_WIDTH - 1 - phase) // V7X_SUBLANES + 1

    for phase in range(V7X_SUBLANES):
        n_win = tm + V7X_SUBLANES * (taps_of(phase) - 1)
        for c in range(n_lane_blocks):
            shift_ref[phase, c, 0:n_win, :] = xp_ref[first + phase:first + phase + n_win,
                                                     c * V7X_LANES:(c + 1) * V7X_LANES]

    rows = min(tm, CONV_ROW_BLOCK)
    n_row_blocks = tm // rows

    def block(t, carry):
        c = t // n_row_blocks
        r0 = pl.multiple_of((t % n_row_blocks) * rows, rows)
        acc = jnp.zeros((rows, V7X_LANES), F32) + b_ref[c]
        for phase in range(V7X_SUBLANES):
            for a in range(taps_of(phase)):
                w = V7X_SUBLANES * a + phase
                acc = acc + shift_ref[phase, c, pl.ds(r0 + V7X_SUBLANES * a, rows), :] * w_ref[c, w:w + 1, :]
        y_ref[c, pl.ds(r0, rows), :] = acc
        return carry

    lax.fori_loop(0, n_lane_blocks * n_row_blocks, block, 0)
    y = jnp.concatenate([y_ref[c] for c in range(n_lane_blocks)], axis=1)
    mu = jnp.mean(y, axis=-1, keepdims=True)
    d = y - mu
    var = jnp.mean(d * d, axis=-1, keepdims=True)
    yn = d * lax.rsqrt(var + EPS) * g_ref[...] + beta_ref[...]
    out_ref[...] = (yn * jax.nn.sigmoid(yn)).astype(out_ref.dtype)


def _conv_branch(u, init_hist, w_dw, b_dw, g_ln, b_ln):
    B, T, C = u.shape
    tm = _row_tile(T, CONV_ROW_TILE)
    assert tm % HALO_ROWS == 0
    per = tm // HALO_ROWS
    n_lane_blocks = C // V7X_LANES
    vec = lambda: pl.BlockSpec((1, C), lambda b, i: (0, 0))
    w_blocks = w_dw.reshape(CONV_WIDTH, n_lane_blocks, V7X_LANES).swapaxes(0, 1)
    b_blocks = b_dw.reshape(n_lane_blocks, 1, V7X_LANES)
    return pl.pallas_call(
        functools.partial(_conv_kernel, tm=tm),
        grid=(B, T // tm),
        in_specs=[
            pl.BlockSpec((None, tm, C), lambda b, i: (b, i, 0)),
            pl.BlockSpec((None, HALO_ROWS, C), lambda b, i: (b, jnp.maximum(i * per - 1, 0), 0)),
            pl.BlockSpec((None, HALO_ROWS, C), lambda b, i: (b, 0, 0)),
            pl.BlockSpec((n_lane_blocks, CONV_WIDTH, V7X_LANES), lambda b, i: (0, 0, 0)),
            pl.BlockSpec((n_lane_blocks, 1, V7X_LANES), lambda b, i: (0, 0, 0)),
            vec(), vec(),
        ],
        out_specs=pl.BlockSpec((None, tm, C), lambda b, i: (b, i, 0)),
        out_shape=jax.ShapeDtypeStruct((B, T, C), BF16),
        scratch_shapes=[
            pltpu.VMEM((HALO_ROWS + tm, C), F32),
            pltpu.VMEM((V7X_SUBLANES, n_lane_blocks, tm + HALO_ROWS, V7X_LANES), F32),
            pltpu.VMEM((n_lane_blocks, tm, V7X_LANES), F32),
        ],
        compiler_params=_params(2),
        name="conv",
    )(u, u, init_hist, w_blocks, b_blocks, g_ln, b_ln)


def _mixers(x1, n_streams, attend, key_tile, conv_hist, p):
    T = x1.shape[0] // n_streams
    proj = _project(x1, p["g_mix"], p["w_in"], p["gq_t"], p["gk_t"], p["ones_bd"], key_tile)
    k32, v32, u = proj[-3:]
    attn = attend(*proj)
    conv_ch = u.shape[1]
    hist = jnp.pad(conv_hist, ((0, 0), (HALO_ROWS - conv_hist.shape[1], 0), (0, 0)))
    conv = _conv_branch(u.reshape(n_streams, T, conv_ch), hist, p["w_dw"], p["b_dw"], p["g_ln"], p["b_ln"])
    return attn, conv.reshape(n_streams * T, conv_ch), k32, v32, u


def kernel(x_prompt, x_sample, cache_k, cache_v, cache_conv, g_ffn1, w_ffn1_gu, w_ffn1_down, g_mix, w_in, g_q, g_k, lambda_q1, lambda_k1, lambda_q2, lambda_k2, g_sub, w_dw, b_dw, g_conv_ln, b_conv_ln, w_out, g_ffn2, w_ffn2_gu, w_ffn2_down, g_final):
    depth = cache_k.shape[0]
    n_p, t_p, d_model = x_prompt.shape
    n_s, t_s, _ = x_sample.shape
    past = cache_k.shape[2]
    conv_ch = cache_conv.shape[-1]
    hist_rows = CONV_WIDTH - 1
    assert n_p == 1, "prompt attention kernel handles one prompt stream"
    assert t_p >= hist_rows and t_s >= hist_rows

    group = jnp.arange(Q_WIDTH) // HEAD_DIM
    ones_bd = (group[:, None] == group[None, :]).astype(BF16)
    row = lambda a: a.reshape(1, -1)

    yp = x_prompt.reshape(n_p * t_p, d_model)
    ys = x_sample.reshape(n_s * t_s, d_model)
    outs = [[] for _ in range(6)]
    for l in range(depth):
        lam_init = 0.8 - 0.6 * math.exp(-0.3 * l)
        lam_rows = jnp.stack([lambda_q1[l], lambda_k1[l], lambda_q2[l], lambda_k2[l]])
        p = dict(
            g_ffn1=row(g_ffn1[l]), w1gu=w_ffn1_gu[l].astype(BF16),
            w1d=w_ffn1_down[l].astype(BF16), g_mix=row(g_mix[l]), w_in=w_in[l].astype(BF16),
            gq_t=row(jnp.tile(g_q[l], 2 * N_HEADS)), gk_t=row(jnp.tile(g_k[l], 2 * N_HEADS)), ones_bd=ones_bd,
            w_dw=w_dw[l], b_dw=row(b_dw[l]), g_ln=row(g_conv_ln[l]), b_ln=row(b_conv_ln[l]),
            w_out=w_out[l].astype(BF16), g_ffn2=row(g_ffn2[l]), w2gu=w_ffn2_gu[l].astype(BF16),
            w2d=w_ffn2_down[l].astype(BF16), g_final=row(g_final[l]),
        )
        g_sub_row = row(g_sub[l])

        q_tile, key_tile = _prompt_tiles(t_p)

        def attend_prompt(qs, ks, vt, k32, v32, u):
            return _prompt_attention(qs, ks, vt, lam_rows, g_sub_row, lam_init, q_tile, key_tile)

        def attend_sample(qs, k32, v32, u, l=l):
            ck = jnp.swapaxes(cache_k[l].reshape(n_s, past, Q_WIDTH), 1, 2)
            cv = cache_v[l].reshape(n_s, past * N_HEADS, V_DIM)
            return _sample_attention(qs, k32, v32, ck, cv, lam_rows, g_sub_row, lam_init, n_s)

        zero_hist = jnp.zeros((n_p, hist_rows, conv_ch), F32)
        x1p, x1s = _ffn((yp, ys), p["g_ffn1"], p["w1gu"], p["w1d"], name="ffn1")
        attn_p, conv_p, kp, vp, up = _mixers(x1p, n_p, attend_prompt, key_tile, zero_hist, p)
        attn_s, conv_s, ks, vs, us = _mixers(x1s, n_s, attend_sample, None, cache_conv[l], p)
        yp, ys = _ffn((x1p, x1s), p["g_ffn2"], p["w2gu"], p["w2d"],
                      mix=((attn_p, attn_s), (conv_p, conv_s), p["w_out"]), g_final=p["g_final"], name="ffn2")
        outs[0].append(kp.reshape(n_p, t_p, N_HEADS, 2, HEAD_DIM))
        outs[1].append(vp.reshape(n_p, t_p, N_HEADS, V_DIM))
        outs[2].append(up.reshape(n_p, t_p, conv_ch)[:, t_p - hist_rows:])
        outs[3].append(ks.reshape(n_s, t_s, N_HEADS, 2, HEAD_DIM))
        outs[4].append(vs.reshape(n_s, t_s, N_HEADS, V_DIM))
        outs[5].append(us.reshape(n_s, t_s, conv_ch)[:, t_s - hist_rows:])

    return (yp.reshape(n_p, t_p, d_model), ys.reshape(n_s, t_s, d_model),
            *[jnp.stack(o, axis=0) for o in outs])
```

```python
import functools
import math

import jax
import jax.numpy as jnp
from jax import lax
from jax.experimental import pallas as pl
from jax.experimental.pallas import tpu as pltpu

N_HEADS = 4
HEAD_DIM = 64
V_DIM = 2 * HEAD_DIM
Q_WIDTH = N_HEADS * 2 * HEAD_DIM
ATTN_WIDTH = N_HEADS * V_DIM
CHUNK = 64
CONV_WIDTH = 31
EPS = 1e-6
NEG_INF = -1e30
ALIBI_SLOPES = tuple(2.0 ** (-8.0 * (h + 1) / N_HEADS) for h in range(N_HEADS))
LOG2E = math.log2(math.e)

V7X_LANES = 128
V7X_SUBLANES = 8
V7X_MXU_WIDTH = 256
V7X_VMEM_BYTES = 64 * 1024 * 1024
VMEM_LIMIT_BYTES = V7X_VMEM_BYTES - 8 * 1024 * 1024

FFN_ROW_TILE = 512
CONV_ROW_TILE = 256
ATTN_KEY_TILE = 1024
ATTN_QUERY_TILE = 1024
SAMPLE_KEY_CHUNK = 1024
QUERY_BLOCK = 256
ALIBI_PIECES = 3
KEY_OFFSET_LO = 32
CONV_ROW_BLOCK = 128
HALO_ROWS = 32

F32 = jnp.float32
BF16 = jnp.bfloat16


def _dot(a, b):
    return jnp.dot(a, b, preferred_element_type=F32)


def _dot_nt(a, b):
    return lax.dot_general(a, b, (((1,), (1,)), ((), ())), preferred_element_type=F32)


def _rms_rows(x, g):
    return x * lax.rsqrt(jnp.mean(x * x, axis=-1, keepdims=True) + EPS) * g


def _params(n_grid_axes):
    return pltpu.CompilerParams(
        dimension_semantics=("arbitrary",) * n_grid_axes,
        vmem_limit_bytes=VMEM_LIMIT_BYTES,
    )


def _const_spec(shape):
    nd = len(shape)
    return pl.BlockSpec(shape, lambda *_: (0,) * nd, pipeline_mode=pl.Buffered(1))


def _row_tile(n_rows, want):
    t = min(want, n_rows)
    assert n_rows % t == 0, (n_rows, t)
    return t


def _ffn_kernel(*refs, starts, has_mix, has_final):
    n_src = len(starts) - 1
    it = iter(refs)
    x_refs = [next(it) for _ in range(n_src)]
    if has_mix:
        attn_refs = [next(it) for _ in range(n_src)]
        conv_refs = [next(it) for _ in range(n_src)]
        wo_ref = next(it)
    g_ref, wgu_ref, wd_ref = next(it), next(it), next(it)
    gf_ref = next(it) if has_final else None
    out_refs = [next(it) for _ in range(n_src)]
    step = pl.program_id(0)

    for s in range(n_src):
        @pl.when(jnp.logical_and(step >= starts[s], step < starts[s + 1]))
        def _(s=s):
            x = x_refs[s][...]
            if has_mix:
                n_attn = attn_refs[s].shape[1]
                x = x + _dot(attn_refs[s][...], wo_ref[:n_attn, :]) + _dot(conv_refs[s][...], wo_ref[n_attn:, :])
            h = _rms_rows(x, g_ref[...]).astype(BF16)
            d_ff = wd_ref.shape[0]
            a = _dot(h, wgu_ref[:, :d_ff])
            b = _dot(h, wgu_ref[:, d_ff:])
            act = (a * jax.nn.sigmoid(a) * b).astype(BF16)
            y = x + 0.5 * _dot(act, wd_ref[...])
            if has_final:
                y = _rms_rows(y, gf_ref[...])
            out_refs[s][...] = y


def _ffn(xs, g, w_gu, wd, mix=None, g_final=None, name="ffn"):
    D = xs[0].shape[1]
    tm = _row_tile(min(x.shape[0] for x in xs), FFN_ROW_TILE)
    starts = [0]
    for x in xs:
        assert x.shape[0] % tm == 0
        starts.append(starts[-1] + x.shape[0] // tm)

    def rows(s, width):
        lo, n = starts[s], starts[s + 1] - starts[s]
        return pl.BlockSpec((tm, width), lambda i: (jnp.clip(i - lo, 0, n - 1), 0))

    n_src = len(xs)
    args, specs = list(xs), [rows(s, D) for s in range(n_src)]
    if mix is not None:
        attns, convs, w_out = mix
        args += [*attns, *convs, w_out]
        specs += [rows(s, a.shape[1]) for s, a in enumerate(attns)]
        specs += [rows(s, c.shape[1]) for s, c in enumerate(convs)]
        specs.append(_const_spec(w_out.shape))
    args += [g, w_gu, wd]
    specs += [_const_spec((1, D)), _const_spec(w_gu.shape), _const_spec(wd.shape)]
    if g_final is not None:
        args.append(g_final)
        specs.append(_const_spec((1, D)))
    return pl.pallas_call(
        functools.partial(_ffn_kernel, starts=tuple(starts), has_mix=mix is not None,
                          has_final=g_final is not None),
        grid=(starts[-1],),
        in_specs=specs,
        out_specs=[rows(s, D) for s in range(n_src)],
        out_shape=[jax.ShapeDtypeStruct(x.shape, F32) for x in xs],
        compiler_params=_params(1),
        name=name,
    )(*args)


def _group_rms(y, g, ones_bd):
    y2 = y * y
    hi = y2.astype(BF16)
    lo = (y2 - hi.astype(F32)).astype(BF16)
    span = ones_bd.shape[0]
    ss = jnp.concatenate(
        [_dot(hi[:, c:c + span], ones_bd) + _dot(lo[:, c:c + span], ones_bd) for c in range(0, y.shape[1], span)],
        axis=1)
    return y * lax.rsqrt(ss * (1.0 / HEAD_DIM) + EPS) * g


def _alibi_query_lanes():
    c = jnp.asarray([s * LOG2E for s in ALIBI_SLOPES], F32)
    pieces, rest = [], c
    for _ in range(ALIBI_PIECES):
        pieces.append(rest.astype(BF16).astype(F32))
        rest = rest - pieces[-1]
    pieces = jnp.stack(pieces + pieces, axis=1)
    return jnp.zeros((N_HEADS, V_DIM), F32).at[:, HEAD_DIM:HEAD_DIM + 2 * ALIBI_PIECES].set(pieces)


def _proj_kernel(*refs, key_tile):
    if key_tile is None:
        x_ref, g_ref, w_ref, gq_ref, gk_ref, ones_ref, qs_ref, k32_ref, v32_ref, u_ref = refs
    else:
        (x_ref, g_ref, w_ref, gq_ref, gk_ref, ones_ref, qext_ref,
         qs_ref, ks_ref, vt_ref, k32_ref, v32_ref, u_ref) = refs
    tm = x_ref.shape[0]
    h = _rms_rows(x_ref[...], g_ref[...]).astype(BF16)
    z = _dot(h, w_ref[...])
    ones_bd = ones_ref[...]
    qn = _group_rms(z[:, :Q_WIDTH], gq_ref[...], ones_bd) * (HEAD_DIM ** -0.5 * LOG2E)
    kn = _group_rms(z[:, Q_WIDTH:2 * Q_WIDTH], gk_ref[...], ones_bd)
    v = z[:, 2 * Q_WIDTH:2 * Q_WIDTH + ATTN_WIDTH]
    conv_ch = (z.shape[1] - 2 * Q_WIDTH - ATTN_WIDTH) // 2
    a = z[:, 2 * Q_WIDTH + ATTN_WIDTH:2 * Q_WIDTH + ATTN_WIDTH + conv_ch]
    gate = z[:, 2 * Q_WIDTH + ATTN_WIDTH + conv_ch:]
    k32_ref[...] = kn
    for hh in range(N_HEADS):
        v32_ref[pl.ds(hh, tm, stride=N_HEADS), :] = v[:, V_DIM * hh:V_DIM * (hh + 1)]
    u_ref[...] = a * jax.nn.sigmoid(gate)

    lane = lax.broadcasted_iota(jnp.int32, (tm, V_DIM), 1)
    low = lane < HEAD_DIM
    if key_tile is None:
        for hh in range(N_HEADS):
            qh = qn[:, V_DIM * hh:V_DIM * (hh + 1)]
            qs_ref[hh, 0] = jnp.where(low, qh, 0.0).astype(BF16)
            qs_ref[hh, 1] = jnp.where(low, 0.0, qh).astype(BF16)
        return

    row = pl.program_id(0) * tm + lax.broadcasted_iota(jnp.int32, (tm, V_DIM), 0)
    off = row % key_tile
    lo = off % KEY_OFFSET_LO
    in_hi = (lane >= HEAD_DIM) & (lane < HEAD_DIM + ALIBI_PIECES)
    in_lo = (lane >= HEAD_DIM + ALIBI_PIECES) & (lane < HEAD_DIM + 2 * ALIBI_PIECES)
    k_ext = jnp.where(in_hi, off - lo, jnp.where(in_lo, lo, 0)).astype(F32)
    for hh in range(N_HEADS):
        sl = slice(V_DIM * hh, V_DIM * (hh + 1))
        q_ext = qext_ref[hh:hh + 1, :]
        for x, ext, dst in ((qn[:, sl], q_ext, qs_ref), (kn[:, sl], k_ext, ks_ref)):
            dst[hh, 0] = jnp.where(low, x, ext).astype(BF16)
            dst[hh, 1] = jnp.where(low, pltpu.roll(x, HEAD_DIM, axis=1), ext).astype(BF16)
        vt_ref[hh] = v[:, sl].T.astype(BF16)


def _project(x1, g_mix, w_in, gq_t, gk_t, ones_bd, key_tile=None):
    T, D = x1.shape
    C = w_in.shape[1]
    conv_ch = (C - 2 * Q_WIDTH - ATTN_WIDTH) // 2
    tm = _row_tile(T, FFN_ROW_TILE)
    row = lambda w: pl.BlockSpec((tm, w), lambda i: (i, 0))
    per_map = pl.BlockSpec((N_HEADS, 2, tm, V_DIM), lambda i: (0, 0, i, 0))
    per_map_shape = jax.ShapeDtypeStruct((N_HEADS, 2, T, V_DIM), BF16)
    args = [x1, g_mix, w_in, gq_t, gk_t, ones_bd]
    in_specs = [row(D), _const_spec((1, D)), _const_spec((D, C)), _const_spec((1, Q_WIDTH)),
                _const_spec((1, Q_WIDTH)), _const_spec(ones_bd.shape)]
    out_specs, out_shape = [per_map], [per_map_shape]
    if key_tile is not None:
        args.append(_alibi_query_lanes())
        in_specs.append(_const_spec((N_HEADS, V_DIM)))
        out_specs += [per_map, pl.BlockSpec((N_HEADS, V_DIM, tm), lambda i: (0, 0, i))]
        out_shape += [per_map_shape, jax.ShapeDtypeStruct((N_HEADS, V_DIM, T), BF16)]
    out_specs += [row(Q_WIDTH), pl.BlockSpec((tm * N_HEADS, V_DIM), lambda i: (i, 0)), row(conv_ch)]
    out_shape += [jax.ShapeDtypeStruct(s, F32) for s in ((T, Q_WIDTH), (T * N_HEADS, V_DIM), (T, conv_ch))]
    return pl.pallas_call(
        functools.partial(_proj_kernel, key_tile=key_tile),
        grid=(T // tm,),
        in_specs=in_specs,
        out_specs=out_specs,
        out_shape=out_shape,
        compiler_params=_params(1),
        name="proj",
    )(*args)


def _diff_lambda(lam_ref, lam_init):
    lp = lam_ref[...]
    s1 = jnp.sum(lp[0:1] * lp[1:2], axis=-1, keepdims=True)
    s2 = jnp.sum(lp[2:3] * lp[3:4], axis=-1, keepdims=True)
    return jnp.exp(s1) - jnp.exp(s2) + lam_init


def _head_out(o1, o2, lam, g_sub, lam_init):
    d = o1 - lam * o2
    return _rms_rows(d, g_sub) * (1.0 - lam_init)


def _prompt_attn_kernel(qs_ref, qnext_ref, ks_ref, vt_ref, lam_ref, gsub_ref, out_ref,
                        acc_ref, m_ref, l_ref, sa_ref, sb_ref, maxa_ref, maxb_ref, p_ref, a_ref, base_ref,
                        *, tq, tk, lam_init):
    hh = pl.program_id(0)
    qi = pl.program_id(1)
    cols = 2 * tq
    slope2 = jnp.float32(0.0)
    for h_static, sl in enumerate(ALIBI_SLOPES):
        slope2 = jnp.where(hh == h_static, jnp.float32(sl * LOG2E), slope2)

    acc_ref[...] = jnp.zeros(acc_ref.shape, F32)
    m_ref[...] = jnp.full(m_ref.shape, NEG_INF, F32)
    l_ref[...] = jnp.zeros(l_ref.shape, F32)

    q_start = qi * tq
    n_full = q_start // tk

    def tile_start(j):
        return pl.multiple_of(j * tk, tk)

    n_blocks = cols // QUERY_BLOCK

    def scores(j, buf, c, q_ref=qs_ref):
        s_ref, max_ref = buf
        which, first = divmod(c * QUERY_BLOCK, tq)
        q = q_ref[which, first:first + QUERY_BLOCK, :]
        s = _dot_nt(ks_ref[which, pl.ds(tile_start(j), tk), :], q)
        s_ref[c] = s
        max_ref[c] = jnp.max(s, axis=0, keepdims=True)

    def softmax_pv(j, buf, c, n_keys=None):
        s_ref, max_ref = buf
        if n_keys is None:
            keys, tile_max = slice(0, tk), max_ref[c]
        else:
            keys = slice(0, n_keys)
            tile_max = jnp.max(s_ref[c, keys, :], axis=0, keepdims=True)
        t_off = slope2 * (j * tk - q_start).astype(F32)
        m = m_ref[c]
        m_new = jnp.maximum(m, tile_max + t_off)
        alpha = jnp.exp2(m - m_new)
        p = jnp.exp2(s_ref[c, keys, :] - (m_new - t_off))
        l_ref[c] = alpha * l_ref[c] + jnp.sum(p, axis=0, keepdims=True)
        m_ref[c] = m_new
        p_ref[c, keys, :] = p.astype(BF16)
        a_ref[c] = alpha
        vt = vt_ref[:, pl.ds(tile_start(j), keys.stop)]
        acc_ref[c] = a_ref[c] * acc_ref[c] + _dot(vt, p_ref[c, keys, :])

    def mask_own(buf, c, first_key):
        s_ref = buf[0]
        keys = slice(first_key, first_key + QUERY_BLOCK)
        ko = lax.broadcasted_iota(jnp.int32, (QUERY_BLOCK, QUERY_BLOCK), 0)
        qo = lax.broadcasted_iota(jnp.int32, (QUERY_BLOCK, QUERY_BLOCK), 1)
        after = (-2.0 * slope2) * jnp.maximum(ko - qo, 0).astype(F32)
        visible = (ko // CHUNK) <= (qo // CHUNK)
        s_ref[c, keys, :] = jnp.where(visible, s_ref[c, keys, :] + after, NEG_INF)

    sets = ((sa_ref, maxa_ref), (sb_ref, maxb_ref))

    @pl.when(qi == 0)
    def _():
        base_ref[0] = 0
        for c in range(n_blocks):
            scores(0, sets[0], c)

    base = base_ref[0]

    def step(j, cur, other):
        for c in range(n_blocks):
            scores(j + 1, other, c)
            softmax_pv(j, cur, c)

    def body(j, carry):
        for parity in range(2):
            @pl.when((j + base) % 2 == parity)
            def _(parity=parity):
                step(j, sets[parity], sets[1 - parity])

        return carry

    lax.fori_loop(0, n_full, body, 0)

    def last(cur, other):
        for sub in range(tk // tq):
            @pl.when(q_start - n_full * tk == sub * tq)
            def _(sub=sub):
                for c in range(n_blocks):
                    if other is not None:
                        scores(0, other, c, qnext_ref)
                    own = sub * tq + (c * QUERY_BLOCK) % tq
                    mask_own(cur, c, own)
                    softmax_pv(n_full, cur, c, own + QUERY_BLOCK)

    has_next = qi + 1 < pl.num_programs(1)
    for parity in range(2):
        @pl.when(jnp.logical_and((n_full + base) % 2 == parity, has_next))
        def _(parity=parity):
            last(sets[parity], sets[1 - parity])
            base_ref[0] = 1 - parity

        @pl.when(jnp.logical_and((n_full + base) % 2 == parity, jnp.logical_not(has_next)))
        def _(parity=parity):
            last(sets[parity], None)

    lam = _diff_lambda(lam_ref, lam_init)
    per_map = tq // QUERY_BLOCK
    ot = [acc_ref[c] * (1.0 / l_ref[c]) for c in range(n_blocks)]
    dt = jnp.concatenate([ot[c] - lam * ot[per_map + c] for c in range(per_map)], axis=1)
    yt = dt * lax.rsqrt(jnp.mean(dt * dt, axis=0, keepdims=True) + EPS)
    out_ref[...] = (yt.T * gsub_ref[...] * (1.0 - lam_init)).astype(out_ref.dtype)


def _prompt_tiles(T):
    tk = _row_tile(T, ATTN_KEY_TILE)
    tq = _row_tile(tk, ATTN_QUERY_TILE)
    assert tq % QUERY_BLOCK == 0 and QUERY_BLOCK % CHUNK == 0
    assert tk // KEY_OFFSET_LO <= 256
    return tq, tk


def _prompt_attention(qs, ks, vt, lam_rows, g_sub, lam_init, tq, tk):
    T = ks.shape[2]
    n_q = T // tq
    n_blocks = 2 * tq // QUERY_BLOCK
    return pl.pallas_call(
        functools.partial(_prompt_attn_kernel, tq=tq, tk=tk, lam_init=lam_init),
        grid=(N_HEADS, n_q),
        in_specs=[
            pl.BlockSpec((None, 2, tq, V_DIM), lambda h, i: (h, 0, i, 0)),
            pl.BlockSpec((None, 2, tq, V_DIM), lambda h, i: (h, 0, jnp.minimum(i + 1, n_q - 1), 0)),
            pl.BlockSpec((None, 2, T, V_DIM), lambda h, i: (h, 0, 0, 0)),
            pl.BlockSpec((None, V_DIM, T), lambda h, i: (h, 0, 0)),
            pl.BlockSpec((4, HEAD_DIM), lambda h, i: (0, 0)),
            pl.BlockSpec((1, V_DIM), lambda h, i: (0, 0)),
        ],
        out_specs=pl.BlockSpec((tq, V_DIM), lambda h, i: (i, h)),
        out_shape=jax.ShapeDtypeStruct((T, ATTN_WIDTH), BF16),
        scratch_shapes=[
            pltpu.VMEM((n_blocks, V_DIM, QUERY_BLOCK), F32),
            pltpu.VMEM((n_blocks, 1, QUERY_BLOCK), F32),
            pltpu.VMEM((n_blocks, 1, QUERY_BLOCK), F32),
            pltpu.VMEM((n_blocks, tk, QUERY_BLOCK), F32),
            pltpu.VMEM((n_blocks, tk, QUERY_BLOCK), F32),
            pltpu.VMEM((n_blocks, 1, QUERY_BLOCK), F32),
            pltpu.VMEM((n_blocks, 1, QUERY_BLOCK), F32),
            pltpu.VMEM((n_blocks, tk, QUERY_BLOCK), BF16),
            pltpu.VMEM((n_blocks, 1, QUERY_BLOCK), F32),
            pltpu.SMEM((1,), jnp.int32),
        ],
        compiler_params=_params(2),
        name="prompt_attn",
    )(qs, qs, ks, vt, lam_rows, g_sub)


def _sample_attn_kernel(qs_ref, kn_ref, vn_ref, ckt_ref, cv_ref, lam_ref, gsub_ref, out_ref,
                        m_ref, l_ref, acc_ref, *, t_new, past, tk, lam_init):
    grp = 2 * t_new
    rows = N_HEADS * grp
    blocks = []
    for hh in range(N_HEADS):
        qh = qs_ref[hh].reshape(grp, V_DIM)
        z = jnp.zeros((grp, V_DIM), BF16)
        blocks.append(jnp.concatenate([qh if c == hh else z for c in range(N_HEADS)], axis=1))
    q = jnp.concatenate(blocks, axis=0)

    r = lax.broadcasted_iota(jnp.int32, (rows, 1), 0)
    q_pos = past + (r % t_new)
    head = r // grp
    slope2 = jnp.zeros((rows, 1), F32)
    for h_static, sl in enumerate(ALIBI_SLOPES):
        slope2 = jnp.where(head == h_static, jnp.float32(sl * LOG2E), slope2)

    m_ref[...] = jnp.full(m_ref.shape, NEG_INF, F32)
    l_ref[...] = jnp.zeros(l_ref.shape, F32)
    acc_ref[...] = jnp.zeros(acc_ref.shape, F32)

    def step(qk, head_values, k_start, n_keys):
        k_pos = k_start + lax.broadcasted_iota(jnp.int32, (1, n_keys), 1)
        bias = -slope2 * jnp.abs(q_pos - k_pos).astype(F32)
        visible = (k_pos // CHUNK) <= (q_pos // CHUNK)
        s = jnp.where(visible, qk + bias, NEG_INF)
        m_prev = m_ref[...]
        m_new = jnp.maximum(m_prev, jnp.max(s, axis=1, keepdims=True))
        alpha = jnp.exp2(m_prev - m_new)
        p = jnp.exp2(s - m_new)
        l_ref[...] = alpha * l_ref[...] + jnp.sum(p, axis=1, keepdims=True)
        m_ref[...] = m_new
        p = p.astype(BF16)
        for hh in range(N_HEADS):
            mine = slice(hh * grp, (hh + 1) * grp)
            acc_ref[mine, :] = alpha[mine] * acc_ref[mine, :] + _dot(p[mine], head_values(hh).astype(BF16))

    for c in range(past // tk):
        step(_dot(q, ckt_ref[:, c * tk:(c + 1) * tk].astype(BF16)),
             lambda hh, c=c: cv_ref[pl.ds(c * tk * N_HEADS + hh, tk, stride=N_HEADS), :],
             c * tk, tk)
    step(_dot_nt(q, kn_ref[...].astype(BF16)), lambda hh: vn_ref[pl.ds(hh, t_new, stride=N_HEADS), :],
         past, t_new)

    o = acc_ref[...] / l_ref[...]
    lam = _diff_lambda(lam_ref, lam_init)
    outs = []
    for hh in range(N_HEADS):
        o1 = o[hh * grp:hh * grp + t_new]
        o2 = o[hh * grp + t_new:(hh + 1) * grp]
        outs.append(_head_out(o1, o2, lam, gsub_ref[...], lam_init))
    out_ref[...] = jnp.concatenate(outs, axis=1).astype(out_ref.dtype)


def _sample_attention(qs, k_new, v_new, cache_kt, cache_v, lam_rows, g_sub, lam_init, n_streams):
    t_new = k_new.shape[0] // n_streams
    past = cache_kt.shape[2]
    tk = _row_tile(past, SAMPLE_KEY_CHUNK)
    rows = N_HEADS * 2 * t_new
    return pl.pallas_call(
        functools.partial(_sample_attn_kernel, t_new=t_new, past=past, tk=tk, lam_init=lam_init),
        grid=(n_streams,),
        in_specs=[
            pl.BlockSpec((N_HEADS, 2, t_new, V_DIM), lambda b: (0, 0, b, 0)),
            pl.BlockSpec((t_new, Q_WIDTH), lambda b: (b, 0)),
            pl.BlockSpec((t_new * N_HEADS, V_DIM), lambda b: (b, 0)),
            pl.BlockSpec((None, Q_WIDTH, past), lambda b: (b, 0, 0)),
            pl.BlockSpec((None, past * N_HEADS, V_DIM), lambda b: (b, 0, 0)),
            pl.BlockSpec((4, HEAD_DIM), lambda b: (0, 0)),
            pl.BlockSpec((1, V_DIM), lambda b: (0, 0)),
        ],
        out_specs=pl.BlockSpec((t_new, ATTN_WIDTH), lambda b: (b, 0)),
        out_shape=jax.ShapeDtypeStruct((n_streams * t_new, ATTN_WIDTH), BF16),
        scratch_shapes=[
            pltpu.VMEM((rows, 1), F32),
            pltpu.VMEM((rows, 1), F32),
            pltpu.VMEM((rows, V_DIM), F32),
        ],
        compiler_params=_params(1),
        name="sample_attn",
    )(qs, k_new, v_new, cache_kt, cache_v, lam_rows, g_sub)


def _conv_kernel(u_ref, halo_ref, init_ref, w_ref, b_ref, g_ref, beta_ref, out_ref,
                 xp_ref, shift_ref, y_ref, *, tm):
    i = pl.program_id(1)
    hist = jnp.where(i == 0, init_ref[...], halo_ref[...])
    xp_ref[0:HALO_ROWS, :] = hist
    xp_ref[HALO_ROWS:HALO_ROWS + tm, :] = u_ref[...]
    first = HALO_ROWS - (CONV_WIDTH - 1)
    n_lane_blocks = u_ref.shape[-1] // V7X_LANES
    taps_of = lambda phase: (CONV_WIDTH - 1 - phase) // V7X_SUBLANES + 1

    for phase in range(V7X_SUBLANES):
        n_win = tm + V7X_SUBLANES * (taps_of(phase) - 1)
        for c in range(n_lane_blocks):
            shift_ref[phase, c, 0:n_win, :] = xp_ref[first + phase:first + phase + n_win,
                                                     c * V7X_LANES:(c + 1) * V7X_LANES]

    rows = min(tm, CONV_ROW_BLOCK)
    n_row_blocks = tm // rows

    def block(t, carry):
        c = t // n_row_blocks
        r0 = pl.multiple_of((t % n_row_blocks) * rows, rows)
        acc = jnp.zeros((rows, V7X_LANES), F32) + b_ref[c]
        for phase in range(V7X_SUBLANES):
            for a in range(taps_of(phase)):
                w = V7X_SUBLANES * a + phase
                acc = acc + shift_ref[phase, c, pl.ds(r0 + V7X_SUBLANES * a, rows), :] * w_ref[c, w:w + 1, :]
        y_ref[c, pl.ds(r0, rows), :] = acc
        return carry

    lax.fori_loop(0, n_lane_blocks * n_row_blocks, block, 0)
    y = jnp.concatenate([y_ref[c] for c in range(n_lane_blocks)], axis=1)
    mu = jnp.mean(y, axis=-1, keepdims=True)
    d = y - mu
    var = jnp.mean(d * d, axis=-1, keepdims=True)
    yn = d * lax.rsqrt(var + EPS) * g_ref[...] + beta_ref[...]
    out_ref[...] = (yn * jax.nn.sigmoid(yn)).astype(out_ref.dtype)


def _conv_branch(u, init_hist, w_dw, b_dw, g_ln, b_ln):
    B, T, C = u.shape
    tm = _row_tile(T, CONV_ROW_TILE)
    assert tm % HALO_ROWS == 0
    per = tm // HALO_ROWS
    n_lane_blocks = C // V7X_LANES
    vec = lambda: pl.BlockSpec((1, C), lambda b, i: (0, 0))
    w_blocks = w_dw.reshape(CONV_WIDTH, n_lane_blocks, V7X_LANES).swapaxes(0, 1)
    b_blocks = b_dw.reshape(n_lane_blocks, 1, V7X_LANES)
    return pl.pallas_call(
        functools.partial(_conv_kernel, tm=tm),
        grid=(B, T // tm),
        in_specs=[
            pl.BlockSpec((None, tm, C), lambda b, i: (b, i, 0)),
            pl.BlockSpec((None, HALO_ROWS, C), lambda b, i: (b, jnp.maximum(i * per - 1, 0), 0)),
            pl.BlockSpec((None, HALO_ROWS, C), lambda b, i: (b, 0, 0)),
            pl.BlockSpec((n_lane_blocks, CONV_WIDTH, V7X_LANES), lambda b, i: (0, 0, 0)),
            pl.BlockSpec((n_lane_blocks, 1, V7X_LANES), lambda b, i: (0, 0, 0)),
            vec(), vec(),
        ],
        out_specs=pl.BlockSpec((None, tm, C), lambda b, i: (b, i, 0)),
        out_shape=jax.ShapeDtypeStruct((B, T, C), BF16),
        scratch_shapes=[
            pltpu.VMEM((HALO_ROWS + tm, C), F32),
            pltpu.VMEM((V7X_SUBLANES, n_lane_blocks, tm + HALO_ROWS, V7X_LANES), F32),
            pltpu.VMEM((n_lane_blocks, tm, V7X_LANES), F32),
        ],
        compiler_params=_params(2),
        name="conv",
    )(u, u, init_hist, w_blocks, b_blocks, g_ln, b_ln)


def _mixers(x1, n_streams, attend, key_tile, conv_hist, p):
    T = x1.shape[0] // n_streams
    proj = _project(x1, p["g_mix"], p["w_in"], p["gq_t"], p["gk_t"], p["ones_bd"], key_tile)
    k32, v32, u = proj[-3:]
    attn = attend(*proj)
    conv_ch = u.shape[1]
    hist = jnp.pad(conv_hist, ((0, 0), (HALO_ROWS - conv_hist.shape[1], 0), (0, 0)))
    conv = _conv_branch(u.reshape(n_streams, T, conv_ch), hist, p["w_dw"], p["b_dw"], p["g_ln"], p["b_ln"])
    return attn, conv.reshape(n_streams * T, conv_ch), k32, v32, u


def kernel(x_prompt, x_sample, cache_k, cache_v, cache_conv, g_ffn1, w_ffn1_gu, w_ffn1_down, g_mix, w_in, g_q, g_k, lambda_q1, lambda_k1, lambda_q2, lambda_k2, g_sub, w_dw, b_dw, g_conv_ln, b_conv_ln, w_out, g_ffn2, w_ffn2_gu, w_ffn2_down, g_final):
    depth = cache_k.shape[0]
    n_p, t_p, d_model = x_prompt.shape
    n_s, t_s, _ = x_sample.shape
    past = cache_k.shape[2]
    conv_ch = cache_conv.shape[-1]
    hist_rows = CONV_WIDTH - 1
    assert n_p == 1, "prompt attention kernel handles one prompt stream"
    assert t_p >= hist_rows and t_s >= hist_rows

    group = jnp.arange(V7X_MXU_WIDTH) // HEAD_DIM
    ones_bd = (group[:, None] == group[None, :]).astype(BF16)
    row = lambda a: a.reshape(1, -1)

    yp = x_prompt.reshape(n_p * t_p, d_model)
    ys = x_sample.reshape(n_s * t_s, d_model)
    outs = [[] for _ in range(6)]
    for l in range(depth):
        lam_init = 0.8 - 0.6 * math.exp(-0.3 * l)
        lam_rows = jnp.stack([lambda_q1[l], lambda_k1[l], lambda_q2[l], lambda_k2[l]])
        p = dict(
            g_ffn1=row(g_ffn1[l]), w1gu=w_ffn1_gu[l].astype(BF16),
            w1d=w_ffn1_down[l].astype(BF16), g_mix=row(g_mix[l]), w_in=w_in[l].astype(BF16),
            gq_t=row(jnp.tile(g_q[l], 2 * N_HEADS)), gk_t=row(jnp.tile(g_k[l], 2 * N_HEADS)), ones_bd=ones_bd,
            w_dw=w_dw[l], b_dw=row(b_dw[l]), g_ln=row(g_conv_ln[l]), b_ln=row(b_conv_ln[l]),
            w_out=w_out[l].astype(BF16), g_ffn2=row(g_ffn2[l]), w2gu=w_ffn2_gu[l].astype(BF16),
            w2d=w_ffn2_down[l].astype(BF16), g_final=row(g_final[l]),
        )
        g_sub_row = row(g_sub[l])

        q_tile, key_tile = _prompt_tiles(t_p)

        def attend_prompt(qs, ks, vt, k32, v32, u):
            return _prompt_attention(qs, ks, vt, lam_rows, g_sub_row, lam_init, q_tile, key_tile)

        def attend_sample(qs, k32, v32, u, l=l):
            ck = jnp.swapaxes(cache_k[l].reshape(n_s, past, Q_WIDTH), 1, 2)
            cv = cache_v[l].reshape(n_s, past * N_HEADS, V_DIM)
            return _sample_attention(qs, k32, v32, ck, cv, lam_rows, g_sub_row, lam_init, n_s)

        zero_hist = jnp.zeros((n_p, hist_rows, conv_ch), F32)
        x1p, x1s = _ffn((yp, ys), p["g_ffn1"], p["w1gu"], p["w1d"], name="ffn1")
        attn_p, conv_p, kp, vp, up = _mixers(x1p, n_p, attend_prompt, key_tile, zero_hist, p)
        attn_s, conv_s, ks, vs, us = _mixers(x1s, n_s, attend_sample, None, cache_conv[l], p)
        yp, ys = _ffn((x1p, x1s), p["g_ffn2"], p["w2gu"], p["w2d"],
                      mix=((attn_p, attn_s), (conv_p, conv_s), p["w_out"]), g_final=p["g_final"], name="ffn2")
        outs[0].append(kp.reshape(n_p, t_p, N_HEADS, 2, HEAD_DIM))
        outs[1].append(vp.reshape(n_p, t_p, N_HEADS, V_DIM))
        outs[2].append(up.reshape(n_p, t_p, conv_ch)[:, t_p - hist_rows:])
        outs[3].append(ks.reshape(n_s, t_s, N_HEADS, 2, HEAD_DIM))
        outs[4].append(vs.reshape(n_s, t_s, N_HEADS, V_DIM))
        outs[5].append(us.reshape(n_s, t_s, conv_ch)[:, t_s - hist_rows:])

    return (yp.reshape(n_p, t_p, d_model), ys.reshape(n_s, t_s, d_model),
            *[jnp.stack(o, axis=0) for o in outs])
```

```python
import functools
import math

import jax
import jax.numpy as jnp
from jax import lax
from jax.experimental import pallas as pl
from jax.experimental.pallas import tpu as pltpu

N_HEADS = 4
HEAD_DIM = 64
V_DIM = 2 * HEAD_DIM
Q_WIDTH = N_HEADS * 2 * HEAD_DIM
ATTN_WIDTH = N_HEADS * V_DIM
CHUNK = 64
CONV_WIDTH = 31
EPS = 1e-6
NEG_INF = -1e30
ALIBI_SLOPES = tuple(2.0 ** (-8.0 * (h + 1) / N_HEADS) for h in range(N_HEADS))
LOG2E = math.log2(math.e)

V7X_LANES = 128
V7X_SUBLANES = 8
V7X_MXU_WIDTH = 256
V7X_VMEM_BYTES = 64 * 1024 * 1024
VMEM_LIMIT_BYTES = V7X_VMEM_BYTES - 8 * 1024 * 1024

FFN_ROW_TILE = 512
CONV_ROW_TILE = 256
ATTN_KEY_TILE = 1024
ATTN_QUERY_TILE = 1024
SAMPLE_KEY_CHUNK = 1024
QUERY_BLOCK = 256
ALIBI_PIECES = 3
KEY_OFFSET_LO = 32
CONV_ROW_BLOCK = 128
HALO_ROWS = 32

F32 = jnp.float32
BF16 = jnp.bfloat16


def _dot(a, b):
    return jnp.dot(a, b, preferred_element_type=F32)


def _dot_nt(a, b):
    return lax.dot_general(a, b, (((1,), (1,)), ((), ())), preferred_element_type=F32)


def _rms_rows(x, g):
    return x * lax.rsqrt(jnp.mean(x * x, axis=-1, keepdims=True) + EPS) * g


def _params(n_grid_axes):
    return pltpu.CompilerParams(
        dimension_semantics=("arbitrary",) * n_grid_axes,
        vmem_limit_bytes=VMEM_LIMIT_BYTES,
    )


def _const_spec(shape):
    nd = len(shape)
    return pl.BlockSpec(shape, lambda *_: (0,) * nd, pipeline_mode=pl.Buffered(1))


def _row_tile(n_rows, want):
    t = min(want, n_rows)
    assert n_rows % t == 0, (n_rows, t)
    return t


def _ffn_kernel(*refs, starts, has_mix, has_final):
    n_src = len(starts) - 1
    it = iter(refs)
    x_refs = [next(it) for _ in range(n_src)]
    if has_mix:
        attn_refs = [next(it) for _ in range(n_src)]
        conv_refs = [next(it) for _ in range(n_src)]
        wo_ref = next(it)
    g_ref, wgu_ref, wd_ref = next(it), next(it), next(it)
    gf_ref = next(it) if has_final else None
    out_refs = [next(it) for _ in range(n_src)]
    step = pl.program_id(0)

    for s in range(n_src):
        @pl.when(jnp.logical_and(step >= starts[s], step < starts[s + 1]))
        def _(s=s):
            x = x_refs[s][...]
            if has_mix:
                n_attn = attn_refs[s].shape[1]
                x = x + _dot(attn_refs[s][...], wo_ref[:n_attn, :]) + _dot(conv_refs[s][...], wo_ref[n_attn:, :])
            h = _rms_rows(x, g_ref[...]).astype(BF16)
            d_ff = wd_ref.shape[0]
            a = _dot(h, wgu_ref[:, :d_ff])
            b = _dot(h, wgu_ref[:, d_ff:])
            act = (a * jax.nn.sigmoid(a) * b).astype(BF16)
            y = x + 0.5 * _dot(act, wd_ref[...])
            if has_final:
                y = _rms_rows(y, gf_ref[...])
            out_refs[s][...] = y


def _ffn(xs, g, w_gu, wd, mix=None, g_final=None, name="ffn"):
    D = xs[0].shape[1]
    tm = _row_tile(min(x.shape[0] for x in xs), FFN_ROW_TILE)
    starts = [0]
    for x in xs:
        assert x.shape[0] % tm == 0
        starts.append(starts[-1] + x.shape[0] // tm)

    def rows(s, width):
        lo, n = starts[s], starts[s + 1] - starts[s]
        return pl.BlockSpec((tm, width), lambda i: (jnp.clip(i - lo, 0, n - 1), 0))

    n_src = len(xs)
    args, specs = list(xs), [rows(s, D) for s in range(n_src)]
    if mix is not None:
        attns, convs, w_out = mix
        args += [*attns, *convs, w_out]
        specs += [rows(s, a.shape[1]) for s, a in enumerate(attns)]
        specs += [rows(s, c.shape[1]) for s, c in enumerate(convs)]
        specs.append(_const_spec(w_out.shape))
    args += [g, w_gu, wd]
    specs += [_const_spec((1, D)), _const_spec(w_gu.shape), _const_spec(wd.shape)]
    if g_final is not None:
        args.append(g_final)
        specs.append(_const_spec((1, D)))
    return pl.pallas_call(
        functools.partial(_ffn_kernel, starts=tuple(starts), has_mix=mix is not None,
                          has_final=g_final is not None),
        grid=(starts[-1],),
        in_specs=specs,
        out_specs=[rows(s, D) for s in range(n_src)],
        out_shape=[jax.ShapeDtypeStruct(x.shape, F32) for x in xs],
        compiler_params=_params(1),
        name=name,
    )(*args)


def _group_rms(y, g, ones_bd):
    y2 = y * y
    hi = y2.astype(BF16)
    lo = (y2 - hi.astype(F32)).astype(BF16)
    span = ones_bd.shape[0]
    ss = jnp.concatenate(
        [_dot(hi[:, c:c + span], ones_bd) + _dot(lo[:, c:c + span], ones_bd) for c in range(0, y.shape[1], span)],
        axis=1)
    return y * lax.rsqrt(ss * (1.0 / HEAD_DIM) + EPS) * g


def _alibi_query_lanes():
    c = jnp.asarray([s * LOG2E for s in ALIBI_SLOPES], F32)
    pieces, rest = [], c
    for _ in range(ALIBI_PIECES):
        pieces.append(rest.astype(BF16).astype(F32))
        rest = rest - pieces[-1]
    pieces = jnp.stack(pieces + pieces, axis=1)
    return jnp.zeros((N_HEADS, V_DIM), F32).at[:, HEAD_DIM:HEAD_DIM + 2 * ALIBI_PIECES].set(pieces)


def _proj_kernel(*refs, starts, key_tiles):
    n_src = len(key_tiles)
    it = iter(refs)
    x_refs = [next(it) for _ in range(n_src)]
    shared = [next(it) for _ in range(6)]
    out_refs = [[next(it) for _ in range(4 if kt is None else 6)] for kt in key_tiles]
    step = pl.program_id(0)
    for s in range(n_src):
        @pl.when(jnp.logical_and(step >= starts[s], step < starts[s + 1]))
        def _(s=s):
            _proj_rows(x_refs[s], *shared, *out_refs[s], first_row=(step - starts[s]) * x_refs[s].shape[0],
                       key_tile=key_tiles[s])


def _proj_rows(x_ref, g_ref, w_ref, gq_ref, gk_ref, ones_ref, qext_ref, *out_refs, first_row, key_tile):
    if key_tile is None:
        qs_ref, k32_ref, v32_ref, u_ref = out_refs
    else:
        qs_ref, ks_ref, vt_ref, k32_ref, v32_ref, u_ref = out_refs
    tm = x_ref.shape[0]
    h = _rms_rows(x_ref[...], g_ref[...]).astype(BF16)
    z = _dot(h, w_ref[...])
    ones_bd = ones_ref[...]
    qn = _group_rms(z[:, :Q_WIDTH], gq_ref[...], ones_bd) * (HEAD_DIM ** -0.5 * LOG2E)
    kn = _group_rms(z[:, Q_WIDTH:2 * Q_WIDTH], gk_ref[...], ones_bd)
    v = z[:, 2 * Q_WIDTH:2 * Q_WIDTH + ATTN_WIDTH]
    conv_ch = (z.shape[1] - 2 * Q_WIDTH - ATTN_WIDTH) // 2
    a = z[:, 2 * Q_WIDTH + ATTN_WIDTH:2 * Q_WIDTH + ATTN_WIDTH + conv_ch]
    gate = z[:, 2 * Q_WIDTH + ATTN_WIDTH + conv_ch:]
    k32_ref[...] = kn
    for hh in range(N_HEADS):
        v32_ref[pl.ds(hh, tm, stride=N_HEADS), :] = v[:, V_DIM * hh:V_DIM * (hh + 1)]
    u_ref[...] = a * jax.nn.sigmoid(gate)

    lane = lax.broadcasted_iota(jnp.int32, (tm, V_DIM), 1)
    low = lane < HEAD_DIM
    if key_tile is None:
        for hh in range(N_HEADS):
            qh = qn[:, V_DIM * hh:V_DIM * (hh + 1)]
            qs_ref[hh, 0] = jnp.where(low, qh, 0.0).astype(BF16)
            qs_ref[hh, 1] = jnp.where(low, 0.0, qh).astype(BF16)
        return

    row = first_row + lax.broadcasted_iota(jnp.int32, (tm, V_DIM), 0)
    off = row % key_tile
    lo = off % KEY_OFFSET_LO
    in_hi = (lane >= HEAD_DIM) & (lane < HEAD_DIM + ALIBI_PIECES)
    in_lo = (lane >= HEAD_DIM + ALIBI_PIECES) & (lane < HEAD_DIM + 2 * ALIBI_PIECES)
    k_ext = jnp.where(in_hi, off - lo, jnp.where(in_lo, lo, 0)).astype(F32)
    for hh in range(N_HEADS):
        sl = slice(V_DIM * hh, V_DIM * (hh + 1))
        q_ext = qext_ref[hh:hh + 1, :]
        for x, ext, dst in ((qn[:, sl], q_ext, qs_ref), (kn[:, sl], k_ext, ks_ref)):
            dst[hh, 0] = jnp.where(low, x, ext).astype(BF16)
            dst[hh, 1] = jnp.where(low, pltpu.roll(x, HEAD_DIM, axis=1), ext).astype(BF16)
        vt_ref[hh] = v[:, sl].T.astype(BF16)


def _project(xs, key_tiles, g_mix, w_in, gq_t, gk_t, ones_bd):
    D = xs[0].shape[1]
    C = w_in.shape[1]
    conv_ch = (C - 2 * Q_WIDTH - ATTN_WIDTH) // 2
    tm = _row_tile(min(x.shape[0] for x in xs), FFN_ROW_TILE)
    starts = [0]
    for x in xs:
        assert x.shape[0] % tm == 0
        starts.append(starts[-1] + x.shape[0] // tm)

    def block_of(s):
        lo, n = starts[s], starts[s + 1] - starts[s]
        return lambda i: jnp.clip(i - lo, 0, n - 1)

    in_specs = [pl.BlockSpec((tm, D), lambda i, b=block_of(s): (b(i), 0)) for s in range(len(xs))]
    in_specs += [_const_spec((1, D)), _const_spec((D, C)), _const_spec((1, Q_WIDTH)),
                 _const_spec((1, Q_WIDTH)), _const_spec(ones_bd.shape), _const_spec((N_HEADS, V_DIM))]
    out_specs, out_shape, n_outs = [], [], []
    for s, (x, key_tile) in enumerate(zip(xs, key_tiles)):
        T, b = x.shape[0], block_of(s)
        rows = lambda w, b=b: pl.BlockSpec((tm, w), lambda i: (b(i), 0))
        per_map = pl.BlockSpec((N_HEADS, 2, tm, V_DIM), lambda i, b=b: (0, 0, b(i), 0))
        per_map_shape = jax.ShapeDtypeStruct((N_HEADS, 2, T, V_DIM), BF16)
        specs, shapes = [per_map], [per_map_shape]
        if key_tile is not None:
            specs += [per_map, pl.BlockSpec((N_HEADS, V_DIM, tm), lambda i, b=b: (0, 0, b(i)))]
            shapes += [per_map_shape, jax.ShapeDtypeStruct((N_HEADS, V_DIM, T), BF16)]
        specs += [rows(Q_WIDTH), pl.BlockSpec((tm * N_HEADS, V_DIM), lambda i, b=b: (b(i), 0)), rows(conv_ch)]
        shapes += [jax.ShapeDtypeStruct(sh, F32) for sh in ((T, Q_WIDTH), (T * N_HEADS, V_DIM), (T, conv_ch))]
        out_specs += specs
        out_shape += shapes
        n_outs.append(len(specs))
    outs = pl.pallas_call(
        functools.partial(_proj_kernel, starts=tuple(starts), key_tiles=tuple(key_tiles)),
        grid=(starts[-1],),
        in_specs=in_specs,
        out_specs=out_specs,
        out_shape=out_shape,
        compiler_params=_params(1),
        name="proj",
    )(*xs, g_mix, w_in, gq_t, gk_t, ones_bd, _alibi_query_lanes())
    split, at = [], 0
    for n in n_outs:
        split.append(outs[at:at + n])
        at += n
    return split


def _diff_lambda(lam_ref, lam_init):
    lp = lam_ref[...]
    s1 = jnp.sum(lp[0:1] * lp[1:2], axis=-1, keepdims=True)
    s2 = jnp.sum(lp[2:3] * lp[3:4], axis=-1, keepdims=True)
    return jnp.exp(s1) - jnp.exp(s2) + lam_init


def _head_out(o1, o2, lam, g_sub, lam_init):
    d = o1 - lam * o2
    return _rms_rows(d, g_sub) * (1.0 - lam_init)


def _prompt_attn_kernel(qs_ref, qnext_ref, ks_ref, vt_ref, lam_ref, gsub_ref, out_ref,
                        acc_ref, m_ref, l_ref, sa_ref, sb_ref, maxa_ref, maxb_ref, p_ref, a_ref, base_ref,
                        *, tq, tk, lam_init):
    hh = pl.program_id(0)
    qi = pl.program_id(1)
    cols = 2 * tq
    slope2 = jnp.float32(0.0)
    for h_static, sl in enumerate(ALIBI_SLOPES):
        slope2 = jnp.where(hh == h_static, jnp.float32(sl * LOG2E), slope2)

    acc_ref[...] = jnp.zeros(acc_ref.shape, F32)
    m_ref[...] = jnp.full(m_ref.shape, NEG_INF, F32)
    l_ref[...] = jnp.zeros(l_ref.shape, F32)

    q_start = qi * tq
    n_full = q_start // tk

    def tile_start(j):
        return pl.multiple_of(j * tk, tk)

    n_blocks = cols // QUERY_BLOCK

    def scores(j, buf, c, q_ref=qs_ref):
        s_ref, max_ref = buf
        which, first = divmod(c * QUERY_BLOCK, tq)
        q = q_ref[which, first:first + QUERY_BLOCK, :]
        s = _dot_nt(ks_ref[which, pl.ds(tile_start(j), tk), :], q)
        s_ref[c] = s
        max_ref[c] = jnp.max(s, axis=0, keepdims=True)

    def softmax_pv(j, buf, c, n_keys=None):
        s_ref, max_ref = buf
        if n_keys is None:
            keys, tile_max = slice(0, tk), max_ref[c]
        else:
            keys = slice(0, n_keys)
            tile_max = jnp.max(s_ref[c, keys, :], axis=0, keepdims=True)
        t_off = slope2 * (j * tk - q_start).astype(F32)
        m = m_ref[c]
        m_new = jnp.maximum(m, tile_max + t_off)
        alpha = jnp.exp2(m - m_new)
        p = jnp.exp2(s_ref[c, keys, :] - (m_new - t_off))
        l_ref[c] = alpha * l_ref[c] + jnp.sum(p, axis=0, keepdims=True)
        m_ref[c] = m_new
        p_ref[c, keys, :] = p.astype(BF16)
        a_ref[c] = alpha
        vt = vt_ref[:, pl.ds(tile_start(j), keys.stop)]
        acc_ref[c] = a_ref[c] * acc_ref[c] + _dot(vt, p_ref[c, keys, :])

    def mask_own(buf, c, first_key):
        s_ref = buf[0]
        keys = slice(first_key, first_key + QUERY_BLOCK)
        ko = lax.broadcasted_iota(jnp.int32, (QUERY_BLOCK, QUERY_BLOCK), 0)
        qo = lax.broadcasted_iota(jnp.int32, (QUERY_BLOCK, QUERY_BLOCK), 1)
        after = (-2.0 * slope2) * jnp.maximum(ko - qo, 0).astype(F32)
        visible = (ko // CHUNK) <= (qo // CHUNK)
        s_ref[c, keys, :] = jnp.where(visible, s_ref[c, keys, :] + after, NEG_INF)

    sets = ((sa_ref, maxa_ref), (sb_ref, maxb_ref))

    @pl.when(qi == 0)
    def _():
        base_ref[0] = 0
        for c in range(n_blocks):
            scores(0, sets[0], c)

    base = base_ref[0]

    def step(j, cur, other):
        for c in range(n_blocks):
            scores(j + 1, other, c)
            softmax_pv(j, cur, c)

    def body(j, carry):
        for parity in range(2):
            @pl.when((j + base) % 2 == parity)
            def _(parity=parity):
                step(j, sets[parity], sets[1 - parity])

        return carry

    lax.fori_loop(0, n_full, body, 0)

    def last(cur, other):
        for sub in range(tk // tq):
            @pl.when(q_start - n_full * tk == sub * tq)
            def _(sub=sub):
                for c in range(n_blocks):
                    if other is not None:
                        scores(0, other, c, qnext_ref)
                    own = sub * tq + (c * QUERY_BLOCK) % tq
                    mask_own(cur, c, own)
                    softmax_pv(n_full, cur, c, own + QUERY_BLOCK)

    has_next = qi + 1 < pl.num_programs(1)
    for parity in range(2):
        @pl.when(jnp.logical_and((n_full + base) % 2 == parity, has_next))
        def _(parity=parity):
            last(sets[parity], sets[1 - parity])
            base_ref[0] = 1 - parity

        @pl.when(jnp.logical_and((n_full + base) % 2 == parity, jnp.logical_not(has_next)))
        def _(parity=parity):
            last(sets[parity], None)

    lam = _diff_lambda(lam_ref, lam_init)
    per_map = tq // QUERY_BLOCK
    ot = [acc_ref[c] * (1.0 / l_ref[c]) for c in range(n_blocks)]
    dt = jnp.concatenate([ot[c] - lam * ot[per_map + c] for c in range(per_map)], axis=1)
    yt = dt * lax.rsqrt(jnp.mean(dt * dt, axis=0, keepdims=True) + EPS)
    out_ref[...] = (yt.T * gsub_ref[...] * (1.0 - lam_init)).astype(out_ref.dtype)


def _prompt_tiles(T):
    tk = _row_tile(T, ATTN_KEY_TILE)
    tq = _row_tile(tk, ATTN_QUERY_TILE)
    assert tq % QUERY_BLOCK == 0 and QUERY_BLOCK % CHUNK == 0
    assert tk // KEY_OFFSET_LO <= 256
    return tq, tk


def _prompt_attention(qs, ks, vt, lam_rows, g_sub, lam_init, tq, tk):
    T = ks.shape[2]
    n_q = T // tq
    n_blocks = 2 * tq // QUERY_BLOCK
    return pl.pallas_call(
        functools.partial(_prompt_attn_kernel, tq=tq, tk=tk, lam_init=lam_init),
        grid=(N_HEADS, n_q),
        in_specs=[
            pl.BlockSpec((None, 2, tq, V_DIM), lambda h, i: (h, 0, i, 0)),
            pl.BlockSpec((None, 2, tq, V_DIM), lambda h, i: (h, 0, jnp.minimum(i + 1, n_q - 1), 0)),
            pl.BlockSpec((None, 2, T, V_DIM), lambda h, i: (h, 0, 0, 0)),
            pl.BlockSpec((None, V_DIM, T), lambda h, i: (h, 0, 0)),
            pl.BlockSpec((4, HEAD_DIM), lambda h, i: (0, 0)),
            pl.BlockSpec((1, V_DIM), lambda h, i: (0, 0)),
        ],
        out_specs=pl.BlockSpec((tq, V_DIM), lambda h, i: (i, h)),
        out_shape=jax.ShapeDtypeStruct((T, ATTN_WIDTH), BF16),
        scratch_shapes=[
            pltpu.VMEM((n_blocks, V_DIM, QUERY_BLOCK), F32),
            pltpu.VMEM((n_blocks, 1, QUERY_BLOCK), F32),
            pltpu.VMEM((n_blocks, 1, QUERY_BLOCK), F32),
            pltpu.VMEM((n_blocks, tk, QUERY_BLOCK), F32),
            pltpu.VMEM((n_blocks, tk, QUERY_BLOCK), F32),
            pltpu.VMEM((n_blocks, 1, QUERY_BLOCK), F32),
            pltpu.VMEM((n_blocks, 1, QUERY_BLOCK), F32),
            pltpu.VMEM((n_blocks, tk, QUERY_BLOCK), BF16),
            pltpu.VMEM((n_blocks, 1, QUERY_BLOCK), F32),
            pltpu.SMEM((1,), jnp.int32),
        ],
        compiler_params=_params(2),
        name="prompt_attn",
    )(qs, qs, ks, vt, lam_rows, g_sub)


def _sample_attn_kernel(qs_ref, kn_ref, vn_ref, ckt_ref, cv_ref, lam_ref, gsub_ref, out_ref,
                        m_ref, l_ref, acc_ref, *, t_new, past, tk, lam_init):
    grp = 2 * t_new
    rows = N_HEADS * grp
    blocks = []
    for hh in range(N_HEADS):
        qh = qs_ref[hh].reshape(grp, V_DIM)
        z = jnp.zeros((grp, V_DIM), BF16)
        blocks.append(jnp.concatenate([qh if c == hh else z for c in range(N_HEADS)], axis=1))
    q = jnp.concatenate(blocks, axis=0)

    r = lax.broadcasted_iota(jnp.int32, (rows, 1), 0)
    q_pos = past + (r % t_new)
    head = r // grp
    slope2 = jnp.zeros((rows, 1), F32)
    for h_static, sl in enumerate(ALIBI_SLOPES):
        slope2 = jnp.where(head == h_static, jnp.float32(sl * LOG2E), slope2)

    m_ref[...] = jnp.full(m_ref.shape, NEG_INF, F32)
    l_ref[...] = jnp.zeros(l_ref.shape, F32)
    acc_ref[...] = jnp.zeros(acc_ref.shape, F32)

    def step(qk, head_values, k_start, n_keys):
        k_pos = k_start + lax.broadcasted_iota(jnp.int32, (1, n_keys), 1)
        bias = -slope2 * jnp.abs(q_pos - k_pos).astype(F32)
        visible = (k_pos // CHUNK) <= (q_pos // CHUNK)
        s = jnp.where(visible, qk + bias, NEG_INF)
        m_prev = m_ref[...]
        m_new = jnp.maximum(m_prev, jnp.max(s, axis=1, keepdims=True))
        alpha = jnp.exp2(m_prev - m_new)
        p = jnp.exp2(s - m_new)
        l_ref[...] = alpha * l_ref[...] + jnp.sum(p, axis=1, keepdims=True)
        m_ref[...] = m_new
        p = p.astype(BF16)
        for hh in range(N_HEADS):
            mine = slice(hh * grp, (hh + 1) * grp)
            acc_ref[mine, :] = alpha[mine] * acc_ref[mine, :] + _dot(p[mine], head_values(hh).astype(BF16))

    for c in range(past // tk):
        step(_dot(q, ckt_ref[:, c * tk:(c + 1) * tk].astype(BF16)),
             lambda hh, c=c: cv_ref[pl.ds(c * tk * N_HEADS + hh, tk, stride=N_HEADS), :],
             c * tk, tk)
    step(_dot_nt(q, kn_ref[...].astype(BF16)), lambda hh: vn_ref[pl.ds(hh, t_new, stride=N_HEADS), :],
         past, t_new)

    o = acc_ref[...] / l_ref[...]
    lam = _diff_lambda(lam_ref, lam_init)
    outs = []
    for hh in range(N_HEADS):
        o1 = o[hh * grp:hh * grp + t_new]
        o2 = o[hh * grp + t_new:(hh + 1) * grp]
        outs.append(_head_out(o1, o2, lam, gsub_ref[...], lam_init))
    out_ref[...] = jnp.concatenate(outs, axis=1).astype(out_ref.dtype)


def _sample_attention(qs, k_new, v_new, cache_kt, cache_v, lam_rows, g_sub, lam_init, n_streams):
    t_new = k_new.shape[0] // n_streams
    past = cache_kt.shape[2]
    tk = _row_tile(past, SAMPLE_KEY_CHUNK)
    rows = N_HEADS * 2 * t_new
    return pl.pallas_call(
        functools.partial(_sample_attn_kernel, t_new=t_new, past=past, tk=tk, lam_init=lam_init),
        grid=(n_streams,),
        in_specs=[
            pl.BlockSpec((N_HEADS, 2, t_new, V_DIM), lambda b: (0, 0, b, 0)),
            pl.BlockSpec((t_new, Q_WIDTH), lambda b: (b, 0)),
            pl.BlockSpec((t_new * N_HEADS, V_DIM), lambda b: (b, 0)),
            pl.BlockSpec((None, Q_WIDTH, past), lambda b: (b, 0, 0)),
            pl.BlockSpec((None, past * N_HEADS, V_DIM), lambda b: (b, 0, 0)),
            pl.BlockSpec((4, HEAD_DIM), lambda b: (0, 0)),
            pl.BlockSpec((1, V_DIM), lambda b: (0, 0)),
        ],
        out_specs=pl.BlockSpec((t_new, ATTN_WIDTH), lambda b: (b, 0)),
        out_shape=jax.ShapeDtypeStruct((n_streams * t_new, ATTN_WIDTH), BF16),
        scratch_shapes=[
            pltpu.VMEM((rows, 1), F32),
            pltpu.VMEM((rows, 1), F32),
            pltpu.VMEM((rows, V_DIM), F32),
        ],
        compiler_params=_params(1),
        name="sample_attn",
    )(qs, k_new, v_new, cache_kt, cache_v, lam_rows, g_sub)


def _conv_kernel(u_ref, halo_ref, init_ref, w_ref, b_ref, g_ref, beta_ref, out_ref,
                 xp_ref, shift_ref, y_ref, *, tm):
    i = pl.program_id(1)
    hist = jnp.where(i == 0, init_ref[...], halo_ref[...])
    xp_ref[0:HALO_ROWS, :] = hist
    xp_ref[HALO_ROWS:HALO_ROWS + tm, :] = u_ref[...]
    first = HALO_ROWS - (CONV_WIDTH - 1)
    n_lane_blocks = u_ref.shape[-1] // V7X_LANES
    taps_of = lambda phase: (CONV_WIDTH - 1 - phase) // V7X_SUBLANES + 1

    for phase in range(V7X_SUBLANES):
        n_win = tm + V7X_SUBLANES * (taps_of(phase) - 1)
        for c in range(n_lane_blocks):
            shift_ref[phase, c, 0:n_win, :] = xp_ref[first + phase:first + phase + n_win,
                                                     c * V7X_LANES:(c + 1) * V7X_LANES]

    rows = min(tm, CONV_ROW_BLOCK)
    n_row_blocks = tm // rows

    def block(t, carry):
        c = t // n_row_blocks
        r0 = pl.multiple_of((t % n_row_blocks) * rows, rows)
        acc = jnp.zeros((rows, V7X_LANES), F32) + b_ref[c]
        for phase in range(V7X_SUBLANES):
            for a in range(taps_of(phase)):
                w = V7X_SUBLANES * a + phase
                acc = acc + shift_ref[phase, c, pl.ds(r0 + V7X_SUBLANES * a, rows), :] * w_ref[c, w:w + 1, :]
        y_ref[c, pl.ds(r0, rows), :] = acc
        return carry

    lax.fori_loop(0, n_lane_blocks * n_row_blocks, block, 0)
    y = jnp.concatenate([y_ref[c] for c in range(n_lane_blocks)], axis=1)
    mu = jnp.mean(y, axis=-1, keepdims=True)
    d = y - mu
    var = jnp.mean(d * d, axis=-1, keepdims=True)
    yn = d * lax.rsqrt(var + EPS) * g_ref[...] + beta_ref[...]
    out_ref[...] = (yn * jax.nn.sigmoid(yn)).astype(out_ref.dtype)


def _conv_branch(u, init_hist, w_dw, b_dw, g_ln, b_ln):
    B, T, C = u.shape
    tm = _row_tile(T, CONV_ROW_TILE)
    assert tm % HALO_ROWS == 0
    per = tm // HALO_ROWS
    n_lane_blocks = C // V7X_LANES
    vec = lambda: pl.BlockSpec((1, C), lambda b, i: (0, 0))
    w_blocks = w_dw.reshape(CONV_WIDTH, n_lane_blocks, V7X_LANES).swapaxes(0, 1)
    b_blocks = b_dw.reshape(n_lane_blocks, 1, V7X_LANES)
    return pl.pallas_call(
        functools.partial(_conv_kernel, tm=tm),
        grid=(B, T // tm),
        in_specs=[
            pl.BlockSpec((None, tm, C), lambda b, i: (b, i, 0)),
            pl.BlockSpec((None, HALO_ROWS, C), lambda b, i: (b, jnp.maximum(i * per - 1, 0), 0)),
            pl.BlockSpec((None, HALO_ROWS, C), lambda b, i: (b, 0, 0)),
            pl.BlockSpec((n_lane_blocks, CONV_WIDTH, V7X_LANES), lambda b, i: (0, 0, 0)),
            pl.BlockSpec((n_lane_blocks, 1, V7X_LANES), lambda b, i: (0, 0, 0)),
            vec(), vec(),
        ],
        out_specs=pl.BlockSpec((None, tm, C), lambda b, i: (b, i, 0)),
        out_shape=jax.ShapeDtypeStruct((B, T, C), BF16),
        scratch_shapes=[
            pltpu.VMEM((HALO_ROWS + tm, C), F32),
            pltpu.VMEM((V7X_SUBLANES, n_lane_blocks, tm + HALO_ROWS, V7X_LANES), F32),
            pltpu.VMEM((n_lane_blocks, tm, V7X_LANES), F32),
        ],
        compiler_params=_params(2),
        name="conv",
    )(u, u, init_hist, w_blocks, b_blocks, g_ln, b_ln)


def _mixers(proj, n_streams, attend, conv_hist, p):
    k32, v32, u = proj[-3:]
    T = u.shape[0] // n_streams
    attn = attend(*proj)
    conv_ch = u.shape[1]
    hist = jnp.pad(conv_hist, ((0, 0), (HALO_ROWS - conv_hist.shape[1], 0), (0, 0)))
    conv = _conv_branch(u.reshape(n_streams, T, conv_ch), hist, p["w_dw"], p["b_dw"], p["g_ln"], p["b_ln"])
    return attn, conv.reshape(n_streams * T, conv_ch), k32, v32, u


def kernel(x_prompt, x_sample, cache_k, cache_v, cache_conv, g_ffn1, w_ffn1_gu, w_ffn1_down, g_mix, w_in, g_q, g_k, lambda_q1, lambda_k1, lambda_q2, lambda_k2, g_sub, w_dw, b_dw, g_conv_ln, b_conv_ln, w_out, g_ffn2, w_ffn2_gu, w_ffn2_down, g_final):
    depth = cache_k.shape[0]
    n_p, t_p, d_model = x_prompt.shape
    n_s, t_s, _ = x_sample.shape
    past = cache_k.shape[2]
    conv_ch = cache_conv.shape[-1]
    hist_rows = CONV_WIDTH - 1
    assert n_p == 1, "prompt attention kernel handles one prompt stream"
    assert t_p >= hist_rows and t_s >= hist_rows

    group = jnp.arange(V7X_MXU_WIDTH) // HEAD_DIM
    ones_bd = (group[:, None] == group[None, :]).astype(BF16)
    row = lambda a: a.reshape(1, -1)

    yp = x_prompt.reshape(n_p * t_p, d_model)
    ys = x_sample.reshape(n_s * t_s, d_model)
    outs = [[] for _ in range(6)]
    for l in range(depth):
        lam_init = 0.8 - 0.6 * math.exp(-0.3 * l)
        lam_rows = jnp.stack([lambda_q1[l], lambda_k1[l], lambda_q2[l], lambda_k2[l]])
        p = dict(
            g_ffn1=row(g_ffn1[l]), w1gu=w_ffn1_gu[l].astype(BF16),
            w1d=w_ffn1_down[l].astype(BF16), g_mix=row(g_mix[l]), w_in=w_in[l].astype(BF16),
            gq_t=row(jnp.tile(g_q[l], 2 * N_HEADS)), gk_t=row(jnp.tile(g_k[l], 2 * N_HEADS)), ones_bd=ones_bd,
            w_dw=w_dw[l], b_dw=row(b_dw[l]), g_ln=row(g_conv_ln[l]), b_ln=row(b_conv_ln[l]),
            w_out=w_out[l].astype(BF16), g_ffn2=row(g_ffn2[l]), w2gu=w_ffn2_gu[l].astype(BF16),
            w2d=w_ffn2_down[l].astype(BF16), g_final=row(g_final[l]),
        )
        g_sub_row = row(g_sub[l])

        q_tile, key_tile = _prompt_tiles(t_p)

        def attend_prompt(qs, ks, vt, k32, v32, u):
            return _prompt_attention(qs, ks, vt, lam_rows, g_sub_row, lam_init, q_tile, key_tile)

        def attend_sample(qs, k32, v32, u, l=l):
            ck = jnp.swapaxes(cache_k[l].reshape(n_s, past, Q_WIDTH), 1, 2)
            cv = cache_v[l].reshape(n_s, past * N_HEADS, V_DIM)
            return _sample_attention(qs, k32, v32, ck, cv, lam_rows, g_sub_row, lam_init, n_s)

        zero_hist = jnp.zeros((n_p, hist_rows, conv_ch), F32)
        x1p, x1s = _ffn((yp, ys), p["g_ffn1"], p["w1gu"], p["w1d"], name="ffn1")
        proj_p, proj_s = _project((x1p, x1s), (key_tile, None), p["g_mix"], p["w_in"], p["gq_t"], p["gk_t"],
                                  p["ones_bd"])
        attn_p, conv_p, kp, vp, up = _mixers(proj_p, n_p, attend_prompt, zero_hist, p)
        attn_s, conv_s, ks, vs, us = _mixers(proj_s, n_s, attend_sample, cache_conv[l], p)
        yp, ys = _ffn((x1p, x1s), p["g_ffn2"], p["w2gu"], p["w2d"],
                      mix=((attn_p, attn_s), (conv_p, conv_s), p["w_out"]), g_final=p["g_final"], name="ffn2")
        outs[0].append(kp.reshape(n_p, t_p, N_HEADS, 2, HEAD_DIM))
        outs[1].append(vp.reshape(n_p, t_p, N_HEADS, V_DIM))
        outs[2].append(up.reshape(n_p, t_p, conv_ch)[:, t_p - hist_rows:])
        outs[3].append(ks.reshape(n_s, t_s, N_HEADS, 2, HEAD_DIM))
        outs[4].append(vs.reshape(n_s, t_s, N_HEADS, V_DIM))
        outs[5].append(us.reshape(n_s, t_s, conv_ch)[:, t_s - hist_rows:])

    return (yp.reshape(n_p, t_p, d_model), ys.reshape(n_s, t_s, d_model),
            *[jnp.stack(o, axis=0) for o in outs])
```

```python
import functools
import math

import jax
import jax.numpy as jnp
from jax import lax
from jax.experimental import pallas as pl
from jax.experimental.pallas import tpu as pltpu

N_HEADS = 4
HEAD_DIM = 64
V_DIM = 2 * HEAD_DIM
Q_WIDTH = N_HEADS * 2 * HEAD_DIM
ATTN_WIDTH = N_HEADS * V_DIM
CHUNK = 64
CONV_WIDTH = 31
EPS = 1e-6
NEG_INF = -1e30
ALIBI_SLOPES = tuple(2.0 ** (-8.0 * (h + 1) / N_HEADS) for h in range(N_HEADS))
LOG2E = math.log2(math.e)

V7X_LANES = 128
V7X_SUBLANES = 8
V7X_MXU_WIDTH = 256
V7X_VMEM_BYTES = 64 * 1024 * 1024
VMEM_LIMIT_BYTES = V7X_VMEM_BYTES - 8 * 1024 * 1024

FFN_ROW_TILE = 512
CONV_ROW_TILE = 512
ATTN_KEY_TILE = 1024
ATTN_QUERY_TILE = 1024
SAMPLE_KEY_CHUNK = 1024
QUERY_BLOCK = 256
ALIBI_PIECES = 3
KEY_OFFSET_LO = 32
CONV_ROW_BLOCK = 128
HALO_ROWS = 32

F32 = jnp.float32
BF16 = jnp.bfloat16


def _dot(a, b):
    return jnp.dot(a, b, preferred_element_type=F32)


def _dot_nt(a, b):
    return lax.dot_general(a, b, (((1,), (1,)), ((), ())), preferred_element_type=F32)


def _rms_rows(x, g):
    return x * lax.rsqrt(jnp.mean(x * x, axis=-1, keepdims=True) + EPS) * g


def _params(n_grid_axes):
    return pltpu.CompilerParams(
        dimension_semantics=("arbitrary",) * n_grid_axes,
        vmem_limit_bytes=VMEM_LIMIT_BYTES,
    )


def _const_spec(shape):
    nd = len(shape)
    return pl.BlockSpec(shape, lambda *_: (0,) * nd, pipeline_mode=pl.Buffered(1))


def _row_tile(n_rows, want):
    t = min(want, n_rows)
    assert n_rows % t == 0, (n_rows, t)
    return t


def _ffn_kernel(*refs, starts, has_mix, has_final):
    n_src = len(starts) - 1
    it = iter(refs)
    x_refs = [next(it) for _ in range(n_src)]
    if has_mix:
        attn_refs = [next(it) for _ in range(n_src)]
        conv_refs = [next(it) for _ in range(n_src)]
        wo_ref = next(it)
    g_ref, wgu_ref, wd_ref = next(it), next(it), next(it)
    gf_ref = next(it) if has_final else None
    out_refs = [next(it) for _ in range(n_src)]
    step = pl.program_id(0)

    for s in range(n_src):
        @pl.when(jnp.logical_and(step >= starts[s], step < starts[s + 1]))
        def _(s=s):
            x = x_refs[s][...]
            if has_mix:
                n_attn = attn_refs[s].shape[1]
                x = x + _dot(attn_refs[s][...], wo_ref[:n_attn, :]) + _dot(conv_refs[s][...], wo_ref[n_attn:, :])
            h = _rms_rows(x, g_ref[...]).astype(BF16)
            d_ff = wd_ref.shape[0]
            a = _dot(h, wgu_ref[:, :d_ff])
            b = _dot(h, wgu_ref[:, d_ff:])
            act = (a * jax.nn.sigmoid(a) * b).astype(BF16)
            y = x + 0.5 * _dot(act, wd_ref[...])
            if has_final:
                y = _rms_rows(y, gf_ref[...])
            out_refs[s][...] = y


def _ffn(xs, g, w_gu, wd, mix=None, g_final=None, name="ffn"):
    D = xs[0].shape[1]
    tm = _row_tile(min(x.shape[0] for x in xs), FFN_ROW_TILE)
    starts = [0]
    for x in xs:
        assert x.shape[0] % tm == 0
        starts.append(starts[-1] + x.shape[0] // tm)

    def rows(s, width):
        lo, n = starts[s], starts[s + 1] - starts[s]
        return pl.BlockSpec((tm, width), lambda i: (jnp.clip(i - lo, 0, n - 1), 0))

    n_src = len(xs)
    args, specs = list(xs), [rows(s, D) for s in range(n_src)]
    if mix is not None:
        attns, convs, w_out = mix
        args += [*attns, *convs, w_out]
        specs += [rows(s, a.shape[1]) for s, a in enumerate(attns)]
        specs += [rows(s, c.shape[1]) for s, c in enumerate(convs)]
        specs.append(_const_spec(w_out.shape))
    args += [g, w_gu, wd]
    specs += [_const_spec((1, D)), _const_spec(w_gu.shape), _const_spec(wd.shape)]
    if g_final is not None:
        args.append(g_final)
        specs.append(_const_spec((1, D)))
    return pl.pallas_call(
        functools.partial(_ffn_kernel, starts=tuple(starts), has_mix=mix is not None,
                          has_final=g_final is not None),
        grid=(starts[-1],),
        in_specs=specs,
        out_specs=[rows(s, D) for s in range(n_src)],
        out_shape=[jax.ShapeDtypeStruct(x.shape, F32) for x in xs],
        compiler_params=_params(1),
        name=name,
    )(*args)


def _group_rms(y, g, ones_bd):
    y2 = y * y
    hi = y2.astype(BF16)
    lo = (y2 - hi.astype(F32)).astype(BF16)
    span = ones_bd.shape[0]
    ss = jnp.concatenate(
        [_dot(hi[:, c:c + span], ones_bd) + _dot(lo[:, c:c + span], ones_bd) for c in range(0, y.shape[1], span)],
        axis=1)
    return y * lax.rsqrt(ss * (1.0 / HEAD_DIM) + EPS) * g


def _alibi_query_lanes():
    c = jnp.asarray([s * LOG2E for s in ALIBI_SLOPES], F32)
    pieces, rest = [], c
    for _ in range(ALIBI_PIECES):
        pieces.append(rest.astype(BF16).astype(F32))
        rest = rest - pieces[-1]
    pieces = jnp.stack(pieces + pieces, axis=1)
    return jnp.zeros((N_HEADS, V_DIM), F32).at[:, HEAD_DIM:HEAD_DIM + 2 * ALIBI_PIECES].set(pieces)


def _proj_kernel(*refs, starts, key_tiles):
    n_src = len(key_tiles)
    it = iter(refs)
    x_refs = [next(it) for _ in range(n_src)]
    shared = [next(it) for _ in range(6)]
    out_refs = [[next(it) for _ in range(4 if kt is None else 6)] for kt in key_tiles]
    step = pl.program_id(0)
    for s in range(n_src):
        @pl.when(jnp.logical_and(step >= starts[s], step < starts[s + 1]))
        def _(s=s):
            _proj_rows(x_refs[s], *shared, *out_refs[s], first_row=(step - starts[s]) * x_refs[s].shape[0],
                       key_tile=key_tiles[s])


def _proj_rows(x_ref, g_ref, w_ref, gq_ref, gk_ref, ones_ref, qext_ref, *out_refs, first_row, key_tile):
    if key_tile is None:
        qs_ref, k32_ref, v32_ref, u_ref = out_refs
    else:
        qs_ref, ks_ref, vt_ref, k32_ref, v32_ref, u_ref = out_refs
    tm = x_ref.shape[0]
    h = _rms_rows(x_ref[...], g_ref[...]).astype(BF16)
    z = _dot(h, w_ref[...])
    ones_bd = ones_ref[...]
    qn = _group_rms(z[:, :Q_WIDTH], gq_ref[...], ones_bd) * (HEAD_DIM ** -0.5 * LOG2E)
    kn = _group_rms(z[:, Q_WIDTH:2 * Q_WIDTH], gk_ref[...], ones_bd)
    v = z[:, 2 * Q_WIDTH:2 * Q_WIDTH + ATTN_WIDTH]
    conv_ch = (z.shape[1] - 2 * Q_WIDTH - ATTN_WIDTH) // 2
    a = z[:, 2 * Q_WIDTH + ATTN_WIDTH:2 * Q_WIDTH + ATTN_WIDTH + conv_ch]
    gate = z[:, 2 * Q_WIDTH + ATTN_WIDTH + conv_ch:]
    k32_ref[...] = kn
    for hh in range(N_HEADS):
        v32_ref[pl.ds(hh, tm, stride=N_HEADS), :] = v[:, V_DIM * hh:V_DIM * (hh + 1)]
    u_ref[...] = a * jax.nn.sigmoid(gate)

    lane = lax.broadcasted_iota(jnp.int32, (tm, V_DIM), 1)
    low = lane < HEAD_DIM
    if key_tile is None:
        for hh in range(N_HEADS):
            qh = qn[:, V_DIM * hh:V_DIM * (hh + 1)]
            qs_ref[hh, 0] = jnp.where(low, qh, 0.0).astype(BF16)
            qs_ref[hh, 1] = jnp.where(low, 0.0, qh).astype(BF16)
        return

    row = first_row + lax.broadcasted_iota(jnp.int32, (tm, V_DIM), 0)
    off = row % key_tile
    lo = off % KEY_OFFSET_LO
    in_hi = (lane >= HEAD_DIM) & (lane < HEAD_DIM + ALIBI_PIECES)
    in_lo = (lane >= HEAD_DIM + ALIBI_PIECES) & (lane < HEAD_DIM + 2 * ALIBI_PIECES)
    k_ext = jnp.where(in_hi, off - lo, jnp.where(in_lo, lo, 0)).astype(F32)
    for hh in range(N_HEADS):
        sl = slice(V_DIM * hh, V_DIM * (hh + 1))
        q_ext = qext_ref[hh:hh + 1, :]
        for x, ext, dst in ((qn[:, sl], q_ext, qs_ref), (kn[:, sl], k_ext, ks_ref)):
            dst[hh, 0] = jnp.where(low, x, ext).astype(BF16)
            dst[hh, 1] = jnp.where(low, pltpu.roll(x, HEAD_DIM, axis=1), ext).astype(BF16)
        vt_ref[hh] = v[:, sl].T.astype(BF16)


def _project(xs, key_tiles, g_mix, w_in, gq_t, gk_t, ones_bd):
    D = xs[0].shape[1]
    C = w_in.shape[1]
    conv_ch = (C - 2 * Q_WIDTH - ATTN_WIDTH) // 2
    tm = _row_tile(min(x.shape[0] for x in xs), FFN_ROW_TILE)
    starts = [0]
    for x in xs:
        assert x.shape[0] % tm == 0
        starts.append(starts[-1] + x.shape[0] // tm)

    def block_of(s):
        lo, n = starts[s], starts[s + 1] - starts[s]
        return lambda i: jnp.clip(i - lo, 0, n - 1)

    in_specs = [pl.BlockSpec((tm, D), lambda i, b=block_of(s): (b(i), 0)) for s in range(len(xs))]
    in_specs += [_const_spec((1, D)), _const_spec((D, C)), _const_spec((1, Q_WIDTH)),
                 _const_spec((1, Q_WIDTH)), _const_spec(ones_bd.shape), _const_spec((N_HEADS, V_DIM))]
    out_specs, out_shape, n_outs = [], [], []
    for s, (x, key_tile) in enumerate(zip(xs, key_tiles)):
        T, b = x.shape[0], block_of(s)
        rows = lambda w, b=b: pl.BlockSpec((tm, w), lambda i: (b(i), 0))
        per_map = pl.BlockSpec((N_HEADS, 2, tm, V_DIM), lambda i, b=b: (0, 0, b(i), 0))
        per_map_shape = jax.ShapeDtypeStruct((N_HEADS, 2, T, V_DIM), BF16)
        specs, shapes = [per_map], [per_map_shape]
        if key_tile is not None:
            specs += [per_map, pl.BlockSpec((N_HEADS, V_DIM, tm), lambda i, b=b: (0, 0, b(i)))]
            shapes += [per_map_shape, jax.ShapeDtypeStruct((N_HEADS, V_DIM, T), BF16)]
        specs += [rows(Q_WIDTH), pl.BlockSpec((tm * N_HEADS, V_DIM), lambda i, b=b: (b(i), 0)), rows(conv_ch)]
        shapes += [jax.ShapeDtypeStruct(sh, F32) for sh in ((T, Q_WIDTH), (T * N_HEADS, V_DIM), (T, conv_ch))]
        out_specs += specs
        out_shape += shapes
        n_outs.append(len(specs))
    outs = pl.pallas_call(
        functools.partial(_proj_kernel, starts=tuple(starts), key_tiles=tuple(key_tiles)),
        grid=(starts[-1],),
        in_specs=in_specs,
        out_specs=out_specs,
        out_shape=out_shape,
        compiler_params=_params(1),
        name="proj",
    )(*xs, g_mix, w_in, gq_t, gk_t, ones_bd, _alibi_query_lanes())
    split, at = [], 0
    for n in n_outs:
        split.append(outs[at:at + n])
        at += n
    return split


def _diff_lambda(lam_ref, lam_init):
    lp = lam_ref[...]
    s1 = jnp.sum(lp[0:1] * lp[1:2], axis=-1, keepdims=True)
    s2 = jnp.sum(lp[2:3] * lp[3:4], axis=-1, keepdims=True)
    return jnp.exp(s1) - jnp.exp(s2) + lam_init


def _head_out(o1, o2, lam, g_sub, lam_init):
    d = o1 - lam * o2
    return _rms_rows(d, g_sub) * (1.0 - lam_init)


def _prompt_attn_kernel(qs_ref, qnext_ref, ks_ref, vt_ref, lam_ref, gsub_ref, out_ref,
                        acc_ref, m_ref, l_ref, sa_ref, sb_ref, maxa_ref, maxb_ref, p_ref, a_ref, base_ref,
                        *, tq, tk, lam_init):
    hh = pl.program_id(0)
    qi = pl.program_id(1)
    cols = 2 * tq
    slope2 = jnp.float32(0.0)
    for h_static, sl in enumerate(ALIBI_SLOPES):
        slope2 = jnp.where(hh == h_static, jnp.float32(sl * LOG2E), slope2)

    acc_ref[...] = jnp.zeros(acc_ref.shape, F32)
    m_ref[...] = jnp.full(m_ref.shape, NEG_INF, F32)
    l_ref[...] = jnp.zeros(l_ref.shape, F32)

    q_start = qi * tq
    n_full = q_start // tk

    def tile_start(j):
        return pl.multiple_of(j * tk, tk)

    n_blocks = cols // QUERY_BLOCK

    def scores(j, buf, c, q_ref=qs_ref):
        s_ref, max_ref = buf
        which, first = divmod(c * QUERY_BLOCK, tq)
        q = q_ref[which, first:first + QUERY_BLOCK, :]
        s = _dot_nt(ks_ref[which, pl.ds(tile_start(j), tk), :], q)
        s_ref[c] = s
        max_ref[c] = jnp.max(s, axis=0, keepdims=True)

    def softmax_pv(j, buf, c, n_keys=None):
        s_ref, max_ref = buf
        if n_keys is None:
            keys, tile_max = slice(0, tk), max_ref[c]
        else:
            keys = slice(0, n_keys)
            tile_max = jnp.max(s_ref[c, keys, :], axis=0, keepdims=True)
        t_off = slope2 * (j * tk - q_start).astype(F32)
        m = m_ref[c]
        m_new = jnp.maximum(m, tile_max + t_off)
        alpha = jnp.exp2(m - m_new)
        p = jnp.exp2(s_ref[c, keys, :] - (m_new - t_off))
        l_ref[c] = alpha * l_ref[c] + jnp.sum(p, axis=0, keepdims=True)
        m_ref[c] = m_new
        p_ref[c, keys, :] = p.astype(BF16)
        a_ref[c] = alpha
        vt = vt_ref[:, pl.ds(tile_start(j), keys.stop)]
        acc_ref[c] = a_ref[c] * acc_ref[c] + _dot(vt, p_ref[c, keys, :])

    def mask_own(buf, c, first_key):
        s_ref = buf[0]
        keys = slice(first_key, first_key + QUERY_BLOCK)
        ko = lax.broadcasted_iota(jnp.int32, (QUERY_BLOCK, QUERY_BLOCK), 0)
        qo = lax.broadcasted_iota(jnp.int32, (QUERY_BLOCK, QUERY_BLOCK), 1)
        after = (-2.0 * slope2) * jnp.maximum(ko - qo, 0).astype(F32)
        visible = (ko // CHUNK) <= (qo // CHUNK)
        s_ref[c, keys, :] = jnp.where(visible, s_ref[c, keys, :] + after, NEG_INF)

    sets = ((sa_ref, maxa_ref), (sb_ref, maxb_ref))

    @pl.when(qi == 0)
    def _():
        base_ref[0] = 0
        for c in range(n_blocks):
            scores(0, sets[0], c)

    base = base_ref[0]

    def step(j, cur, other):
        for c in range(n_blocks):
            scores(j + 1, other, c)
            softmax_pv(j, cur, c)

    def body(j, carry):
        for parity in range(2):
            @pl.when((j + base) % 2 == parity)
            def _(parity=parity):
                step(j, sets[parity], sets[1 - parity])

        return carry

    lax.fori_loop(0, n_full, body, 0)

    def last(cur, other):
        for sub in range(tk // tq):
            @pl.when(q_start - n_full * tk == sub * tq)
            def _(sub=sub):
                for c in range(n_blocks):
                    if other is not None:
                        scores(0, other, c, qnext_ref)
                    own = sub * tq + (c * QUERY_BLOCK) % tq
                    mask_own(cur, c, own)
                    softmax_pv(n_full, cur, c, own + QUERY_BLOCK)

    has_next = qi + 1 < pl.num_programs(1)
    for parity in range(2):
        @pl.when(jnp.logical_and((n_full + base) % 2 == parity, has_next))
        def _(parity=parity):
            last(sets[parity], sets[1 - parity])
            base_ref[0] = 1 - parity

        @pl.when(jnp.logical_and((n_full + base) % 2 == parity, jnp.logical_not(has_next)))
        def _(parity=parity):
            last(sets[parity], None)

    lam = _diff_lambda(lam_ref, lam_init)
    per_map = tq // QUERY_BLOCK
    ot = [acc_ref[c] * (1.0 / l_ref[c]) for c in range(n_blocks)]
    dt = jnp.concatenate([ot[c] - lam * ot[per_map + c] for c in range(per_map)], axis=1)
    yt = dt * lax.rsqrt(jnp.mean(dt * dt, axis=0, keepdims=True) + EPS)
    out_ref[...] = (yt.T * gsub_ref[...] * (1.0 - lam_init)).astype(out_ref.dtype)


def _prompt_tiles(T):
    tk = _row_tile(T, ATTN_KEY_TILE)
    tq = _row_tile(tk, ATTN_QUERY_TILE)
    assert tq % QUERY_BLOCK == 0 and QUERY_BLOCK % CHUNK == 0
    assert tk // KEY_OFFSET_LO <= 256
    return tq, tk


def _prompt_attention(qs, ks, vt, lam_rows, g_sub, lam_init, tq, tk):
    T = ks.shape[2]
    n_q = T // tq
    n_blocks = 2 * tq // QUERY_BLOCK
    return pl.pallas_call(
        functools.partial(_prompt_attn_kernel, tq=tq, tk=tk, lam_init=lam_init),
        grid=(N_HEADS, n_q),
        in_specs=[
            pl.BlockSpec((None, 2, tq, V_DIM), lambda h, i: (h, 0, i, 0)),
            pl.BlockSpec((None, 2, tq, V_DIM), lambda h, i: (h, 0, jnp.minimum(i + 1, n_q - 1), 0)),
            pl.BlockSpec((None, 2, T, V_DIM), lambda h, i: (h, 0, 0, 0)),
            pl.BlockSpec((None, V_DIM, T), lambda h, i: (h, 0, 0)),
            pl.BlockSpec((4, HEAD_DIM), lambda h, i: (0, 0)),
            pl.BlockSpec((1, V_DIM), lambda h, i: (0, 0)),
        ],
        out_specs=pl.BlockSpec((tq, V_DIM), lambda h, i: (i, h)),
        out_shape=jax.ShapeDtypeStruct((T, ATTN_WIDTH), BF16),
        scratch_shapes=[
            pltpu.VMEM((n_blocks, V_DIM, QUERY_BLOCK), F32),
            pltpu.VMEM((n_blocks, 1, QUERY_BLOCK), F32),
            pltpu.VMEM((n_blocks, 1, QUERY_BLOCK), F32),
            pltpu.VMEM((n_blocks, tk, QUERY_BLOCK), F32),
            pltpu.VMEM((n_blocks, tk, QUERY_BLOCK), F32),
            pltpu.VMEM((n_blocks, 1, QUERY_BLOCK), F32),
            pltpu.VMEM((n_blocks, 1, QUERY_BLOCK), F32),
            pltpu.VMEM((n_blocks, tk, QUERY_BLOCK), BF16),
            pltpu.VMEM((n_blocks, 1, QUERY_BLOCK), F32),
            pltpu.SMEM((1,), jnp.int32),
        ],
        compiler_params=_params(2),
        name="prompt_attn",
    )(qs, qs, ks, vt, lam_rows, g_sub)


def _sample_attn_kernel(qs_ref, kn_ref, vn_ref, ckt_ref, cv_ref, lam_ref, gsub_ref, out_ref,
                        m_ref, l_ref, acc_ref, *, t_new, past, tk, lam_init):
    grp = 2 * t_new
    rows = N_HEADS * grp
    blocks = []
    for hh in range(N_HEADS):
        qh = qs_ref[hh].reshape(grp, V_DIM)
        z = jnp.zeros((grp, V_DIM), BF16)
        blocks.append(jnp.concatenate([qh if c == hh else z for c in range(N_HEADS)], axis=1))
    q = jnp.concatenate(blocks, axis=0)

    r = lax.broadcasted_iota(jnp.int32, (rows, 1), 0)
    q_pos = past + (r % t_new)
    head = r // grp
    slope2 = jnp.zeros((rows, 1), F32)
    for h_static, sl in enumerate(ALIBI_SLOPES):
        slope2 = jnp.where(head == h_static, jnp.float32(sl * LOG2E), slope2)

    m_ref[...] = jnp.full(m_ref.shape, NEG_INF, F32)
    l_ref[...] = jnp.zeros(l_ref.shape, F32)
    acc_ref[...] = jnp.zeros(acc_ref.shape, F32)

    def step(qk, head_values, k_start, n_keys):
        k_pos = k_start + lax.broadcasted_iota(jnp.int32, (1, n_keys), 1)
        bias = -slope2 * jnp.abs(q_pos - k_pos).astype(F32)
        visible = (k_pos // CHUNK) <= (q_pos // CHUNK)
        s = jnp.where(visible, qk + bias, NEG_INF)
        m_prev = m_ref[...]
        m_new = jnp.maximum(m_prev, jnp.max(s, axis=1, keepdims=True))
        alpha = jnp.exp2(m_prev - m_new)
        p = jnp.exp2(s - m_new)
        l_ref[...] = alpha * l_ref[...] + jnp.sum(p, axis=1, keepdims=True)
        m_ref[...] = m_new
        p = p.astype(BF16)
        for hh in range(N_HEADS):
            mine = slice(hh * grp, (hh + 1) * grp)
            acc_ref[mine, :] = alpha[mine] * acc_ref[mine, :] + _dot(p[mine], head_values(hh).astype(BF16))

    for c in range(past // tk):
        step(_dot(q, ckt_ref[:, c * tk:(c + 1) * tk].astype(BF16)),
             lambda hh, c=c: cv_ref[pl.ds(c * tk * N_HEADS + hh, tk, stride=N_HEADS), :],
             c * tk, tk)
    step(_dot_nt(q, kn_ref[...].astype(BF16)), lambda hh: vn_ref[pl.ds(hh, t_new, stride=N_HEADS), :],
         past, t_new)

    o = acc_ref[...] / l_ref[...]
    lam = _diff_lambda(lam_ref, lam_init)
    outs = []
    for hh in range(N_HEADS):
        o1 = o[hh * grp:hh * grp + t_new]
        o2 = o[hh * grp + t_new:(hh + 1) * grp]
        outs.append(_head_out(o1, o2, lam, gsub_ref[...], lam_init))
    out_ref[...] = jnp.concatenate(outs, axis=1).astype(out_ref.dtype)


def _sample_attention(qs, k_new, v_new, cache_kt, cache_v, lam_rows, g_sub, lam_init, n_streams):
    t_new = k_new.shape[0] // n_streams
    past = cache_kt.shape[2]
    tk = _row_tile(past, SAMPLE_KEY_CHUNK)
    rows = N_HEADS * 2 * t_new
    return pl.pallas_call(
        functools.partial(_sample_attn_kernel, t_new=t_new, past=past, tk=tk, lam_init=lam_init),
        grid=(n_streams,),
        in_specs=[
            pl.BlockSpec((N_HEADS, 2, t_new, V_DIM), lambda b: (0, 0, b, 0)),
            pl.BlockSpec((t_new, Q_WIDTH), lambda b: (b, 0)),
            pl.BlockSpec((t_new * N_HEADS, V_DIM), lambda b: (b, 0)),
            pl.BlockSpec((None, Q_WIDTH, past), lambda b: (b, 0, 0)),
            pl.BlockSpec((None, past * N_HEADS, V_DIM), lambda b: (b, 0, 0)),
            pl.BlockSpec((4, HEAD_DIM), lambda b: (0, 0)),
            pl.BlockSpec((1, V_DIM), lambda b: (0, 0)),
        ],
        out_specs=pl.BlockSpec((t_new, ATTN_WIDTH), lambda b: (b, 0)),
        out_shape=jax.ShapeDtypeStruct((n_streams * t_new, ATTN_WIDTH), BF16),
        scratch_shapes=[
            pltpu.VMEM((rows, 1), F32),
            pltpu.VMEM((rows, 1), F32),
            pltpu.VMEM((rows, V_DIM), F32),
        ],
        compiler_params=_params(1),
        name="sample_attn",
    )(qs, k_new, v_new, cache_kt, cache_v, lam_rows, g_sub)


def _conv_kernel(u_ref, halo_ref, init_ref, w_ref, b_ref, g_ref, beta_ref, out_ref,
                 xp_ref, shift_ref, y_ref, *, tm):
    i = pl.program_id(1)
    hist = jnp.where(i == 0, init_ref[...], halo_ref[...])
    xp_ref[0:HALO_ROWS, :] = hist
    xp_ref[HALO_ROWS:HALO_ROWS + tm, :] = u_ref[...]
    first = HALO_ROWS - (CONV_WIDTH - 1)
    n_lane_blocks = u_ref.shape[-1] // V7X_LANES
    taps_of = lambda phase: (CONV_WIDTH - 1 - phase) // V7X_SUBLANES + 1

    for phase in range(V7X_SUBLANES):
        n_win = tm + V7X_SUBLANES * (taps_of(phase) - 1)
        for c in range(n_lane_blocks):
            shift_ref[phase, c, 0:n_win, :] = xp_ref[first + phase:first + phase + n_win,
                                                     c * V7X_LANES:(c + 1) * V7X_LANES]

    rows = min(tm, CONV_ROW_BLOCK)
    n_row_blocks = tm // rows

    def block(t, carry):
        c = t // n_row_blocks
        r0 = pl.multiple_of((t % n_row_blocks) * rows, rows)
        acc = jnp.zeros((rows, V7X_LANES), F32) + b_ref[c]
        for phase in range(V7X_SUBLANES):
            for a in range(taps_of(phase)):
                w = V7X_SUBLANES * a + phase
                acc = acc + shift_ref[phase, c, pl.ds(r0 + V7X_SUBLANES * a, rows), :] * w_ref[c, w:w + 1, :]
        y_ref[c, pl.ds(r0, rows), :] = acc
        return carry

    lax.fori_loop(0, n_lane_blocks * n_row_blocks, block, 0)
    y = jnp.concatenate([y_ref[c] for c in range(n_lane_blocks)], axis=1)
    mu = jnp.mean(y, axis=-1, keepdims=True)
    d = y - mu
    var = jnp.mean(d * d, axis=-1, keepdims=True)
    yn = d * lax.rsqrt(var + EPS) * g_ref[...] + beta_ref[...]
    out_ref[...] = (yn * jax.nn.sigmoid(yn)).astype(out_ref.dtype)


def _conv_branch(u, init_hist, w_dw, b_dw, g_ln, b_ln):
    B, T, C = u.shape
    tm = _row_tile(T, CONV_ROW_TILE)
    assert tm % HALO_ROWS == 0
    per = tm // HALO_ROWS
    n_lane_blocks = C // V7X_LANES
    vec = lambda: pl.BlockSpec((1, C), lambda b, i: (0, 0))
    w_blocks = w_dw.reshape(CONV_WIDTH, n_lane_blocks, V7X_LANES).swapaxes(0, 1)
    b_blocks = b_dw.reshape(n_lane_blocks, 1, V7X_LANES)
    return pl.pallas_call(
        functools.partial(_conv_kernel, tm=tm),
        grid=(B, T // tm),
        in_specs=[
            pl.BlockSpec((None, tm, C), lambda b, i: (b, i, 0)),
            pl.BlockSpec((None, HALO_ROWS, C), lambda b, i: (b, jnp.maximum(i * per - 1, 0), 0)),
            pl.BlockSpec((None, HALO_ROWS, C), lambda b, i: (b, 0, 0)),
            pl.BlockSpec((n_lane_blocks, CONV_WIDTH, V7X_LANES), lambda b, i: (0, 0, 0)),
            pl.BlockSpec((n_lane_blocks, 1, V7X_LANES), lambda b, i: (0, 0, 0)),
            vec(), vec(),
        ],
        out_specs=pl.BlockSpec((None, tm, C), lambda b, i: (b, i, 0)),
        out_shape=jax.ShapeDtypeStruct((B, T, C), BF16),
        scratch_shapes=[
            pltpu.VMEM((HALO_ROWS + tm, C), F32),
            pltpu.VMEM((V7X_SUBLANES, n_lane_blocks, tm + HALO_ROWS, V7X_LANES), F32),
            pltpu.VMEM((n_lane_blocks, tm, V7X_LANES), F32),
        ],
        compiler_params=_params(2),
        name="conv",
    )(u, u, init_hist, w_blocks, b_blocks, g_ln, b_ln)


def _mixers(proj, n_streams, attend, conv_hist, p):
    k32, v32, u = proj[-3:]
    T = u.shape[0] // n_streams
    attn = attend(*proj)
    conv_ch = u.shape[1]
    hist = jnp.pad(conv_hist, ((0, 0), (HALO_ROWS - conv_hist.shape[1], 0), (0, 0)))
    conv = _conv_branch(u.reshape(n_streams, T, conv_ch), hist, p["w_dw"], p["b_dw"], p["g_ln"], p["b_ln"])
    return attn, conv.reshape(n_streams * T, conv_ch), k32, v32, u


def kernel(x_prompt, x_sample, cache_k, cache_v, cache_conv, g_ffn1, w_ffn1_gu, w_ffn1_down, g_mix, w_in, g_q, g_k, lambda_q1, lambda_k1, lambda_q2, lambda_k2, g_sub, w_dw, b_dw, g_conv_ln, b_conv_ln, w_out, g_ffn2, w_ffn2_gu, w_ffn2_down, g_final):
    depth = cache_k.shape[0]
    n_p, t_p, d_model = x_prompt.shape
    n_s, t_s, _ = x_sample.shape
    past = cache_k.shape[2]
    conv_ch = cache_conv.shape[-1]
    hist_rows = CONV_WIDTH - 1
    assert n_p == 1, "prompt attention kernel handles one prompt stream"
    assert t_p >= hist_rows and t_s >= hist_rows

    group = jnp.arange(V7X_MXU_WIDTH) // HEAD_DIM
    ones_bd = (group[:, None] == group[None, :]).astype(BF16)
    row = lambda a: a.reshape(1, -1)

    yp = x_prompt.reshape(n_p * t_p, d_model)
    ys = x_sample.reshape(n_s * t_s, d_model)
    outs = [[] for _ in range(6)]
    for l in range(depth):
        lam_init = 0.8 - 0.6 * math.exp(-0.3 * l)
        lam_rows = jnp.stack([lambda_q1[l], lambda_k1[l], lambda_q2[l], lambda_k2[l]])
        p = dict(
            g_ffn1=row(g_ffn1[l]), w1gu=w_ffn1_gu[l].astype(BF16),
            w1d=w_ffn1_down[l].astype(BF16), g_mix=row(g_mix[l]), w_in=w_in[l].astype(BF16),
            gq_t=row(jnp.tile(g_q[l], 2 * N_HEADS)), gk_t=row(jnp.tile(g_k[l], 2 * N_HEADS)), ones_bd=ones_bd,
            w_dw=w_dw[l], b_dw=row(b_dw[l]), g_ln=row(g_conv_ln[l]), b_ln=row(b_conv_ln[l]),
            w_out=w_out[l].astype(BF16), g_ffn2=row(g_ffn2[l]), w2gu=w_ffn2_gu[l].astype(BF16),
            w2d=w_ffn2_down[l].astype(BF16), g_final=row(g_final[l]),
        )
        g_sub_row = row(g_sub[l])

        q_tile, key_tile = _prompt_tiles(t_p)

        def attend_prompt(qs, ks, vt, k32, v32, u):
            return _prompt_attention(qs, ks, vt, lam_rows, g_sub_row, lam_init, q_tile, key_tile)

        def attend_sample(qs, k32, v32, u, l=l):
            ck = jnp.swapaxes(cache_k[l].reshape(n_s, past, Q_WIDTH), 1, 2)
            cv = cache_v[l].reshape(n_s, past * N_HEADS, V_DIM)
            return _sample_attention(qs, k32, v32, ck, cv, lam_rows, g_sub_row, lam_init, n_s)

        zero_hist = jnp.zeros((n_p, hist_rows, conv_ch), F32)
        x1p, x1s = _ffn((yp, ys), p["g_ffn1"], p["w1gu"], p["w1d"], name="ffn1")
        proj_p, proj_s = _project((x1p, x1s), (key_tile, None), p["g_mix"], p["w_in"], p["gq_t"], p["gk_t"],
                                  p["ones_bd"])
        attn_p, conv_p, kp, vp, up = _mixers(proj_p, n_p, attend_prompt, zero_hist, p)
        attn_s, conv_s, ks, vs, us = _mixers(proj_s, n_s, attend_sample, cache_conv[l], p)
        yp, ys = _ffn((x1p, x1s), p["g_ffn2"], p["w2gu"], p["w2d"],
                      mix=((attn_p, attn_s), (conv_p, conv_s), p["w_out"]), g_final=p["g_final"], name="ffn2")
        outs[0].append(kp.reshape(n_p, t_p, N_HEADS, 2, HEAD_DIM))
        outs[1].append(vp.reshape(n_p, t_p, N_HEADS, V_DIM))
        outs[2].append(up.reshape(n_p, t_p, conv_ch)[:, t_p - hist_rows:])
        outs[3].append(ks.reshape(n_s, t_s, N_HEADS, 2, HEAD_DIM))
        outs[4].append(vs.reshape(n_s, t_s, N_HEADS, V_DIM))
        outs[5].append(us.reshape(n_s, t_s, conv_ch)[:, t_s - hist_rows:])

    return (yp.reshape(n_p, t_p, d_model), ys.reshape(n_s, t_s, d_model),
            *[jnp.stack(o, axis=0) for o in outs])
```

```python
import functools
import math

import jax
import jax.numpy as jnp
from jax import lax
from jax.experimental import pallas as pl
from jax.experimental.pallas import tpu as pltpu

N_HEADS = 4
HEAD_DIM = 64
V_DIM = 2 * HEAD_DIM
Q_WIDTH = N_HEADS * 2 * HEAD_DIM
ATTN_WIDTH = N_HEADS * V_DIM
CHUNK = 64
CONV_WIDTH = 31
EPS = 1e-6
NEG_INF = -1e30
ALIBI_SLOPES = tuple(2.0 ** (-8.0 * (h + 1) / N_HEADS) for h in range(N_HEADS))
LOG2E = math.log2(math.e)

V7X_LANES = 128
V7X_SUBLANES = 8
V7X_MXU_WIDTH = 256
V7X_VMEM_BYTES = 64 * 1024 * 1024
VMEM_LIMIT_BYTES = V7X_VMEM_BYTES - 8 * 1024 * 1024

FFN_ROW_TILE = 512
CONV_ROW_TILE = 512
ATTN_KEY_TILE = 1024
ATTN_QUERY_TILE = 1024
SAMPLE_KEY_CHUNK = 2048
QUERY_BLOCK = 256
ALIBI_PIECES = 3
KEY_OFFSET_LO = 32
CONV_ROW_BLOCK = 256
HALO_ROWS = 32

F32 = jnp.float32
BF16 = jnp.bfloat16


def _dot(a, b):
    return jnp.dot(a, b, preferred_element_type=F32)


def _dot_nt(a, b):
    return lax.dot_general(a, b, (((1,), (1,)), ((), ())), preferred_element_type=F32)


def _rms_rows(x, g):
    return x * lax.rsqrt(jnp.mean(x * x, axis=-1, keepdims=True) + EPS) * g


def _params(n_grid_axes):
    return pltpu.CompilerParams(
        dimension_semantics=("arbitrary",) * n_grid_axes,
        vmem_limit_bytes=VMEM_LIMIT_BYTES,
    )


def _const_spec(shape):
    nd = len(shape)
    return pl.BlockSpec(shape, lambda *_: (0,) * nd, pipeline_mode=pl.Buffered(1))


def _row_tile(n_rows, want):
    t = min(want, n_rows)
    assert n_rows % t == 0, (n_rows, t)
    return t


def _ffn_kernel(*refs, starts, has_mix, has_final):
    n_src = len(starts) - 1
    it = iter(refs)
    x_refs = [next(it) for _ in range(n_src)]
    if has_mix:
        attn_refs = [next(it) for _ in range(n_src)]
        conv_refs = [next(it) for _ in range(n_src)]
        wo_ref = next(it)
    g_ref, wgu_ref, wd_ref = next(it), next(it), next(it)
    gf_ref = next(it) if has_final else None
    out_refs = [next(it) for _ in range(n_src)]
    step = pl.program_id(0)

    for s in range(n_src):
        @pl.when(jnp.logical_and(step >= starts[s], step < starts[s + 1]))
        def _(s=s):
            x = x_refs[s][...]
            if has_mix:
                n_attn = attn_refs[s].shape[1]
                x = x + _dot(attn_refs[s][...], wo_ref[:n_attn, :]) + _dot(conv_refs[s][...], wo_ref[n_attn:, :])
            h = _rms_rows(x, g_ref[...]).astype(BF16)
            d_ff = wd_ref.shape[0]
            a = _dot(h, wgu_ref[:, :d_ff])
            b = _dot(h, wgu_ref[:, d_ff:])
            act = (a * jax.nn.sigmoid(a) * b).astype(BF16)
            y = x + 0.5 * _dot(act, wd_ref[...])
            if has_final:
                y = _rms_rows(y, gf_ref[...])
            out_refs[s][...] = y


def _ffn(xs, g, w_gu, wd, mix=None, g_final=None, name="ffn"):
    D = xs[0].shape[1]
    tm = _row_tile(min(x.shape[0] for x in xs), FFN_ROW_TILE)
    starts = [0]
    for x in xs:
        assert x.shape[0] % tm == 0
        starts.append(starts[-1] + x.shape[0] // tm)

    def rows(s, width):
        lo, n = starts[s], starts[s + 1] - starts[s]
        return pl.BlockSpec((tm, width), lambda i: (jnp.clip(i - lo, 0, n - 1), 0))

    n_src = len(xs)
    args, specs = list(xs), [rows(s, D) for s in range(n_src)]
    if mix is not None:
        attns, convs, w_out = mix
        args += [*attns, *convs, w_out]
        specs += [rows(s, a.shape[1]) for s, a in enumerate(attns)]
        specs += [rows(s, c.shape[1]) for s, c in enumerate(convs)]
        specs.append(_const_spec(w_out.shape))
    args += [g, w_gu, wd]
    specs += [_const_spec((1, D)), _const_spec(w_gu.shape), _const_spec(wd.shape)]
    if g_final is not None:
        args.append(g_final)
        specs.append(_const_spec((1, D)))
    return pl.pallas_call(
        functools.partial(_ffn_kernel, starts=tuple(starts), has_mix=mix is not None,
                          has_final=g_final is not None),
        grid=(starts[-1],),
        in_specs=specs,
        out_specs=[rows(s, D) for s in range(n_src)],
        out_shape=[jax.ShapeDtypeStruct(x.shape, F32) for x in xs],
        compiler_params=_params(1),
        name=name,
    )(*args)


def _group_rms(y, g, ones_bd):
    y2 = y * y
    hi = y2.astype(BF16)
    lo = (y2 - hi.astype(F32)).astype(BF16)
    span = ones_bd.shape[0]
    ss = jnp.concatenate(
        [_dot(hi[:, c:c + span], ones_bd) + _dot(lo[:, c:c + span], ones_bd) for c in range(0, y.shape[1], span)],
        axis=1)
    return y * lax.rsqrt(ss * (1.0 / HEAD_DIM) + EPS) * g


def _alibi_query_lanes():
    c = jnp.asarray([s * LOG2E for s in ALIBI_SLOPES], F32)
    pieces, rest = [], c
    for _ in range(ALIBI_PIECES):
        pieces.append(rest.astype(BF16).astype(F32))
        rest = rest - pieces[-1]
    pieces = jnp.stack(pieces + pieces, axis=1)
    return jnp.zeros((N_HEADS, V_DIM), F32).at[:, HEAD_DIM:HEAD_DIM + 2 * ALIBI_PIECES].set(pieces)


def _proj_kernel(*refs, starts, key_tiles):
    n_src = len(key_tiles)
    it = iter(refs)
    x_refs = [next(it) for _ in range(n_src)]
    shared = [next(it) for _ in range(6)]
    out_refs = [[next(it) for _ in range(4 if kt is None else 6)] for kt in key_tiles]
    step = pl.program_id(0)
    for s in range(n_src):
        @pl.when(jnp.logical_and(step >= starts[s], step < starts[s + 1]))
        def _(s=s):
            _proj_rows(x_refs[s], *shared, *out_refs[s], first_row=(step - starts[s]) * x_refs[s].shape[0],
                       key_tile=key_tiles[s])


def _proj_rows(x_ref, g_ref, w_ref, gq_ref, gk_ref, ones_ref, qext_ref, *out_refs, first_row, key_tile):
    if key_tile is None:
        qs_ref, k32_ref, v32_ref, u_ref = out_refs
    else:
        qs_ref, ks_ref, vt_ref, k32_ref, v32_ref, u_ref = out_refs
    tm = x_ref.shape[0]
    h = _rms_rows(x_ref[...], g_ref[...]).astype(BF16)
    z = _dot(h, w_ref[...])
    ones_bd = ones_ref[...]
    qn = _group_rms(z[:, :Q_WIDTH], gq_ref[...], ones_bd) * (HEAD_DIM ** -0.5 * LOG2E)
    kn = _group_rms(z[:, Q_WIDTH:2 * Q_WIDTH], gk_ref[...], ones_bd)
    v = z[:, 2 * Q_WIDTH:2 * Q_WIDTH + ATTN_WIDTH]
    conv_ch = (z.shape[1] - 2 * Q_WIDTH - ATTN_WIDTH) // 2
    a = z[:, 2 * Q_WIDTH + ATTN_WIDTH:2 * Q_WIDTH + ATTN_WIDTH + conv_ch]
    gate = z[:, 2 * Q_WIDTH + ATTN_WIDTH + conv_ch:]
    k32_ref[...] = kn
    for hh in range(N_HEADS):
        v32_ref[pl.ds(hh, tm, stride=N_HEADS), :] = v[:, V_DIM * hh:V_DIM * (hh + 1)]
    u_ref[...] = a * jax.nn.sigmoid(gate)

    lane = lax.broadcasted_iota(jnp.int32, (tm, V_DIM), 1)
    low = lane < HEAD_DIM
    if key_tile is None:
        for hh in range(N_HEADS):
            qh = qn[:, V_DIM * hh:V_DIM * (hh + 1)]
            qs_ref[hh, 0] = jnp.where(low, qh, 0.0).astype(BF16)
            qs_ref[hh, 1] = jnp.where(low, 0.0, qh).astype(BF16)
        return

    row = first_row + lax.broadcasted_iota(jnp.int32, (tm, V_DIM), 0)
    off = row % key_tile
    lo = off % KEY_OFFSET_LO
    in_hi = (lane >= HEAD_DIM) & (lane < HEAD_DIM + ALIBI_PIECES)
    in_lo = (lane >= HEAD_DIM + ALIBI_PIECES) & (lane < HEAD_DIM + 2 * ALIBI_PIECES)
    k_ext = jnp.where(in_hi, off - lo, jnp.where(in_lo, lo, 0)).astype(F32)
    for hh in range(N_HEADS):
        sl = slice(V_DIM * hh, V_DIM * (hh + 1))
        q_ext = qext_ref[hh:hh + 1, :]
        for x, ext, dst in ((qn[:, sl], q_ext, qs_ref), (kn[:, sl], k_ext, ks_ref)):
            dst[hh, 0] = jnp.where(low, x, ext).astype(BF16)
            dst[hh, 1] = jnp.where(low, pltpu.roll(x, HEAD_DIM, axis=1), ext).astype(BF16)
        vt_ref[hh] = v[:, sl].T.astype(BF16)


def _project(xs, key_tiles, g_mix, w_in, gq_t, gk_t, ones_bd):
    D = xs[0].shape[1]
    C = w_in.shape[1]
    conv_ch = (C - 2 * Q_WIDTH - ATTN_WIDTH) // 2
    tm = _row_tile(min(x.shape[0] for x in xs), FFN_ROW_TILE)
    starts = [0]
    for x in xs:
        assert x.shape[0] % tm == 0
        starts.append(starts[-1] + x.shape[0] // tm)

    def block_of(s):
        lo, n = starts[s], starts[s + 1] - starts[s]
        return lambda i: jnp.clip(i - lo, 0, n - 1)

    in_specs = [pl.BlockSpec((tm, D), lambda i, b=block_of(s): (b(i), 0)) for s in range(len(xs))]
    in_specs += [_const_spec((1, D)), _const_spec((D, C)), _const_spec((1, Q_WIDTH)),
                 _const_spec((1, Q_WIDTH)), _const_spec(ones_bd.shape), _const_spec((N_HEADS, V_DIM))]
    out_specs, out_shape, n_outs = [], [], []
    for s, (x, key_tile) in enumerate(zip(xs, key_tiles)):
        T, b = x.shape[0], block_of(s)
        rows = lambda w, b=b: pl.BlockSpec((tm, w), lambda i: (b(i), 0))
        per_map = pl.BlockSpec((N_HEADS, 2, tm, V_DIM), lambda i, b=b: (0, 0, b(i), 0))
        per_map_shape = jax.ShapeDtypeStruct((N_HEADS, 2, T, V_DIM), BF16)
        specs, shapes = [per_map], [per_map_shape]
        if key_tile is not None:
            specs += [per_map, pl.BlockSpec((N_HEADS, V_DIM, tm), lambda i, b=b: (0, 0, b(i)))]
            shapes += [per_map_shape, jax.ShapeDtypeStruct((N_HEADS, V_DIM, T), BF16)]
        specs += [rows(Q_WIDTH), pl.BlockSpec((tm * N_HEADS, V_DIM), lambda i, b=b: (b(i), 0)), rows(conv_ch)]
        shapes += [jax.ShapeDtypeStruct(sh, F32) for sh in ((T, Q_WIDTH), (T * N_HEADS, V_DIM), (T, conv_ch))]
        out_specs += specs
        out_shape += shapes
        n_outs.append(len(specs))
    outs = pl.pallas_call(
        functools.partial(_proj_kernel, starts=tuple(starts), key_tiles=tuple(key_tiles)),
        grid=(starts[-1],),
        in_specs=in_specs,
        out_specs=out_specs,
        out_shape=out_shape,
        compiler_params=_params(1),
        name="proj",
    )(*xs, g_mix, w_in, gq_t, gk_t, ones_bd, _alibi_query_lanes())
    split, at = [], 0
    for n in n_outs:
        split.append(outs[at:at + n])
        at += n
    return split


def _diff_lambda(lam_ref, lam_init):
    lp = lam_ref[...]
    s1 = jnp.sum(lp[0:1] * lp[1:2], axis=-1, keepdims=True)
    s2 = jnp.sum(lp[2:3] * lp[3:4], axis=-1, keepdims=True)
    return jnp.exp(s1) - jnp.exp(s2) + lam_init


def _head_out(o1, o2, lam, g_sub, lam_init):
    d = o1 - lam * o2
    return _rms_rows(d, g_sub) * (1.0 - lam_init)


def _prompt_attn_kernel(qs_ref, qnext_ref, ks_ref, vt_ref, lam_ref, gsub_ref, out_ref,
                        acc_ref, m_ref, l_ref, sa_ref, sb_ref, maxa_ref, maxb_ref, p_ref, a_ref, base_ref,
                        *, tq, tk, lam_init):
    hh = pl.program_id(0)
    qi = pl.program_id(1)
    cols = 2 * tq
    slope2 = jnp.float32(0.0)
    for h_static, sl in enumerate(ALIBI_SLOPES):
        slope2 = jnp.where(hh == h_static, jnp.float32(sl * LOG2E), slope2)

    acc_ref[...] = jnp.zeros(acc_ref.shape, F32)
    m_ref[...] = jnp.full(m_ref.shape, NEG_INF, F32)
    l_ref[...] = jnp.zeros(l_ref.shape, F32)

    q_start = qi * tq
    n_full = q_start // tk

    def tile_start(j):
        return pl.multiple_of(j * tk, tk)

    n_blocks = cols // QUERY_BLOCK

    def scores(j, buf, c, q_ref=qs_ref):
        s_ref, max_ref = buf
        which, first = divmod(c * QUERY_BLOCK, tq)
        q = q_ref[which, first:first + QUERY_BLOCK, :]
        s = _dot_nt(ks_ref[which, pl.ds(tile_start(j), tk), :], q)
        s_ref[c] = s
        max_ref[c] = jnp.max(s, axis=0, keepdims=True)

    def softmax_pv(j, buf, c, n_keys=None):
        s_ref, max_ref = buf
        if n_keys is None:
            keys, tile_max = slice(0, tk), max_ref[c]
        else:
            keys = slice(0, n_keys)
            tile_max = jnp.max(s_ref[c, keys, :], axis=0, keepdims=True)
        t_off = slope2 * (j * tk - q_start).astype(F32)
        m = m_ref[c]
        m_new = jnp.maximum(m, tile_max + t_off)
        alpha = jnp.exp2(m - m_new)
        p = jnp.exp2(s_ref[c, keys, :] - (m_new - t_off))
        l_ref[c] = alpha * l_ref[c] + jnp.sum(p, axis=0, keepdims=True)
        m_ref[c] = m_new
        p_ref[c, keys, :] = p.astype(BF16)
        a_ref[c] = alpha
        vt = vt_ref[:, pl.ds(tile_start(j), keys.stop)]
        acc_ref[c] = a_ref[c] * acc_ref[c] + _dot(vt, p_ref[c, keys, :])

    def mask_own(buf, c, first_key):
        s_ref = buf[0]
        keys = slice(first_key, first_key + QUERY_BLOCK)
        ko = lax.broadcasted_iota(jnp.int32, (QUERY_BLOCK, QUERY_BLOCK), 0)
        qo = lax.broadcasted_iota(jnp.int32, (QUERY_BLOCK, QUERY_BLOCK), 1)
        after = (-2.0 * slope2) * jnp.maximum(ko - qo, 0).astype(F32)
        visible = (ko // CHUNK) <= (qo // CHUNK)
        s_ref[c, keys, :] = jnp.where(visible, s_ref[c, keys, :] + after, NEG_INF)

    sets = ((sa_ref, maxa_ref), (sb_ref, maxb_ref))

    @pl.when(qi == 0)
    def _():
        base_ref[0] = 0
        for c in range(n_blocks):
            scores(0, sets[0], c)

    base = base_ref[0]

    def step(j, cur, other):
        for c in range(n_blocks):
            scores(j + 1, other, c)
            softmax_pv(j, cur, c)

    def body(j, carry):
        for parity in range(2):
            @pl.when((j + base) % 2 == parity)
            def _(parity=parity):
                step(j, sets[parity], sets[1 - parity])

        return carry

    lax.fori_loop(0, n_full, body, 0)

    def last(cur, other):
        for sub in range(tk // tq):
            @pl.when(q_start - n_full * tk == sub * tq)
            def _(sub=sub):
                for c in range(n_blocks):
                    if other is not None:
                        scores(0, other, c, qnext_ref)
                    own = sub * tq + (c * QUERY_BLOCK) % tq
                    mask_own(cur, c, own)
                    softmax_pv(n_full, cur, c, own + QUERY_BLOCK)

    has_next = qi + 1 < pl.num_programs(1)
    for parity in range(2):
        @pl.when(jnp.logical_and((n_full + base) % 2 == parity, has_next))
        def _(parity=parity):
            last(sets[parity], sets[1 - parity])
            base_ref[0] = 1 - parity

        @pl.when(jnp.logical_and((n_full + base) % 2 == parity, jnp.logical_not(has_next)))
        def _(parity=parity):
            last(sets[parity], None)

    lam = _diff_lambda(lam_ref, lam_init)
    per_map = tq // QUERY_BLOCK
    ot = [acc_ref[c] * (1.0 / l_ref[c]) for c in range(n_blocks)]
    dt = jnp.concatenate([ot[c] - lam * ot[per_map + c] for c in range(per_map)], axis=1)
    yt = dt * lax.rsqrt(jnp.mean(dt * dt, axis=0, keepdims=True) + EPS)
    out_ref[...] = (yt.T * gsub_ref[...] * (1.0 - lam_init)).astype(out_ref.dtype)


def _prompt_tiles(T):
    tk = _row_tile(T, ATTN_KEY_TILE)
    tq = _row_tile(tk, ATTN_QUERY_TILE)
    assert tq % QUERY_BLOCK == 0 and QUERY_BLOCK % CHUNK == 0
    assert tk // KEY_OFFSET_LO <= 256
    return tq, tk


def _prompt_attention(qs, ks, vt, lam_rows, g_sub, lam_init, tq, tk):
    T = ks.shape[2]
    n_q = T // tq
    n_blocks = 2 * tq // QUERY_BLOCK
    return pl.pallas_call(
        functools.partial(_prompt_attn_kernel, tq=tq, tk=tk, lam_init=lam_init),
        grid=(N_HEADS, n_q),
        in_specs=[
            pl.BlockSpec((None, 2, tq, V_DIM), lambda h, i: (h, 0, i, 0)),
            pl.BlockSpec((None, 2, tq, V_DIM), lambda h, i: (h, 0, jnp.minimum(i + 1, n_q - 1), 0)),
            pl.BlockSpec((None, 2, T, V_DIM), lambda h, i: (h, 0, 0, 0)),
            pl.BlockSpec((None, V_DIM, T), lambda h, i: (h, 0, 0)),
            pl.BlockSpec((4, HEAD_DIM), lambda h, i: (0, 0)),
            pl.BlockSpec((1, V_DIM), lambda h, i: (0, 0)),
        ],
        out_specs=pl.BlockSpec((tq, V_DIM), lambda h, i: (i, h)),
        out_shape=jax.ShapeDtypeStruct((T, ATTN_WIDTH), BF16),
        scratch_shapes=[
            pltpu.VMEM((n_blocks, V_DIM, QUERY_BLOCK), F32),
            pltpu.VMEM((n_blocks, 1, QUERY_BLOCK), F32),
            pltpu.VMEM((n_blocks, 1, QUERY_BLOCK), F32),
            pltpu.VMEM((n_blocks, tk, QUERY_BLOCK), F32),
            pltpu.VMEM((n_blocks, tk, QUERY_BLOCK), F32),
            pltpu.VMEM((n_blocks, 1, QUERY_BLOCK), F32),
            pltpu.VMEM((n_blocks, 1, QUERY_BLOCK), F32),
            pltpu.VMEM((n_blocks, tk, QUERY_BLOCK), BF16),
            pltpu.VMEM((n_blocks, 1, QUERY_BLOCK), F32),
            pltpu.SMEM((1,), jnp.int32),
        ],
        compiler_params=_params(2),
        name="prompt_attn",
    )(qs, qs, ks, vt, lam_rows, g_sub)


def _sample_attn_kernel(qs_ref, kn_ref, vn_ref, ckt_ref, cv_ref, lam_ref, gsub_ref, out_ref,
                        m_ref, l_ref, acc_ref, *, t_new, past, tk, lam_init):
    grp = 2 * t_new
    rows = N_HEADS * grp
    blocks = []
    for hh in range(N_HEADS):
        qh = qs_ref[hh].reshape(grp, V_DIM)
        z = jnp.zeros((grp, V_DIM), BF16)
        blocks.append(jnp.concatenate([qh if c == hh else z for c in range(N_HEADS)], axis=1))
    q = jnp.concatenate(blocks, axis=0)

    r = lax.broadcasted_iota(jnp.int32, (rows, 1), 0)
    q_pos = past + (r % t_new)
    head = r // grp
    slope2 = jnp.zeros((rows, 1), F32)
    for h_static, sl in enumerate(ALIBI_SLOPES):
        slope2 = jnp.where(head == h_static, jnp.float32(sl * LOG2E), slope2)

    m_ref[...] = jnp.full(m_ref.shape, NEG_INF, F32)
    l_ref[...] = jnp.zeros(l_ref.shape, F32)
    acc_ref[...] = jnp.zeros(acc_ref.shape, F32)

    def step(qk, head_values, k_start, n_keys):
        k_pos = k_start + lax.broadcasted_iota(jnp.int32, (1, n_keys), 1)
        bias = -slope2 * jnp.abs(q_pos - k_pos).astype(F32)
        visible = (k_pos // CHUNK) <= (q_pos // CHUNK)
        s = jnp.where(visible, qk + bias, NEG_INF)
        m_prev = m_ref[...]
        m_new = jnp.maximum(m_prev, jnp.max(s, axis=1, keepdims=True))
        alpha = jnp.exp2(m_prev - m_new)
        p = jnp.exp2(s - m_new)
        l_ref[...] = alpha * l_ref[...] + jnp.sum(p, axis=1, keepdims=True)
        m_ref[...] = m_new
        p = p.astype(BF16)
        for hh in range(N_HEADS):
            mine = slice(hh * grp, (hh + 1) * grp)
            acc_ref[mine, :] = alpha[mine] * acc_ref[mine, :] + _dot(p[mine], head_values(hh).astype(BF16))

    for c in range(past // tk):
        step(_dot(q, ckt_ref[:, c * tk:(c + 1) * tk].astype(BF16)),
             lambda hh, c=c: cv_ref[pl.ds(c * tk * N_HEADS + hh, tk, stride=N_HEADS), :],
             c * tk, tk)
    step(_dot_nt(q, kn_ref[...].astype(BF16)), lambda hh: vn_ref[pl.ds(hh, t_new, stride=N_HEADS), :],
         past, t_new)

    o = acc_ref[...] / l_ref[...]
    lam = _diff_lambda(lam_ref, lam_init)
    outs = []
    for hh in range(N_HEADS):
        o1 = o[hh * grp:hh * grp + t_new]
        o2 = o[hh * grp + t_new:(hh + 1) * grp]
        outs.append(_head_out(o1, o2, lam, gsub_ref[...], lam_init))
    out_ref[...] = jnp.concatenate(outs, axis=1).astype(out_ref.dtype)


def _sample_attention(qs, k_new, v_new, cache_kt, cache_v, lam_rows, g_sub, lam_init, n_streams):
    t_new = k_new.shape[0] // n_streams
    past = cache_kt.shape[2]
    tk = _row_tile(past, SAMPLE_KEY_CHUNK)
    rows = N_HEADS * 2 * t_new
    return pl.pallas_call(
        functools.partial(_sample_attn_kernel, t_new=t_new, past=past, tk=tk, lam_init=lam_init),
        grid=(n_streams,),
        in_specs=[
            pl.BlockSpec((N_HEADS, 2, t_new, V_DIM), lambda b: (0, 0, b, 0)),
            pl.BlockSpec((t_new, Q_WIDTH), lambda b: (b, 0)),
            pl.BlockSpec((t_new * N_HEADS, V_DIM), lambda b: (b, 0)),
            pl.BlockSpec((None, Q_WIDTH, past), lambda b: (b, 0, 0)),
            pl.BlockSpec((None, past * N_HEADS, V_DIM), lambda b: (b, 0, 0)),
            pl.BlockSpec((4, HEAD_DIM), lambda b: (0, 0)),
            pl.BlockSpec((1, V_DIM), lambda b: (0, 0)),
        ],
        out_specs=pl.BlockSpec((t_new, ATTN_WIDTH), lambda b: (b, 0)),
        out_shape=jax.ShapeDtypeStruct((n_streams * t_new, ATTN_WIDTH), BF16),
        scratch_shapes=[
            pltpu.VMEM((rows, 1), F32),
            pltpu.VMEM((rows, 1), F32),
            pltpu.VMEM((rows, V_DIM), F32),
        ],
        compiler_params=_params(1),
        name="sample_attn",
    )(qs, k_new, v_new, cache_kt, cache_v, lam_rows, g_sub)


def _conv_kernel(u_ref, halo_ref, init_ref, w_ref, b_ref, g_ref, beta_ref, out_ref,
                 xp_ref, shift_ref, y_ref, *, tm):
    i = pl.program_id(1)
    hist = jnp.where(i == 0, init_ref[...], halo_ref[...])
    xp_ref[0:HALO_ROWS, :] = hist
    xp_ref[HALO_ROWS:HALO_ROWS + tm, :] = u_ref[...]
    first = HALO_ROWS - (CONV_WIDTH - 1)
    n_lane_blocks = u_ref.shape[-1] // V7X_LANES
    taps_of = lambda phase: (CONV_WIDTH - 1 - phase) // V7X_SUBLANES + 1

    for phase in range(V7X_SUBLANES):
        n_win = tm + V7X_SUBLANES * (taps_of(phase) - 1)
        for c in range(n_lane_blocks):
            shift_ref[phase, c, 0:n_win, :] = xp_ref[first + phase:first + phase + n_win,
                                                     c * V7X_LANES:(c + 1) * V7X_LANES]

    rows = min(tm, CONV_ROW_BLOCK)
    n_row_blocks = tm // rows

    def block(t, carry):
        c = t // n_row_blocks
        r0 = pl.multiple_of((t % n_row_blocks) * rows, rows)
        acc = jnp.zeros((rows, V7X_LANES), F32) + b_ref[c]
        for phase in range(V7X_SUBLANES):
            for a in range(taps_of(phase)):
                w = V7X_SUBLANES * a + phase
                acc = acc + shift_ref[phase, c, pl.ds(r0 + V7X_SUBLANES * a, rows), :] * w_ref[c, w:w + 1, :]
        y_ref[c, pl.ds(r0, rows), :] = acc
        return carry

    lax.fori_loop(0, n_lane_blocks * n_row_blocks, block, 0)
    y = jnp.concatenate([y_ref[c] for c in range(n_lane_blocks)], axis=1)
    mu = jnp.mean(y, axis=-1, keepdims=True)
    d = y - mu
    var = jnp.mean(d * d, axis=-1, keepdims=True)
    yn = d * lax.rsqrt(var + EPS) * g_ref[...] + beta_ref[...]
    out_ref[...] = (yn * jax.nn.sigmoid(yn)).astype(out_ref.dtype)


def _conv_branch(u, init_hist, w_dw, b_dw, g_ln, b_ln):
    B, T, C = u.shape
    tm = _row_tile(T, CONV_ROW_TILE)
    assert tm % HALO_ROWS == 0
    per = tm // HALO_ROWS
    n_lane_blocks = C // V7X_LANES
    vec = lambda: pl.BlockSpec((1, C), lambda b, i: (0, 0))
    w_blocks = w_dw.reshape(CONV_WIDTH, n_lane_blocks, V7X_LANES).swapaxes(0, 1)
    b_blocks = b_dw.reshape(n_lane_blocks, 1, V7X_LANES)
    return pl.pallas_call(
        functools.partial(_conv_kernel, tm=tm),
        grid=(B, T // tm),
        in_specs=[
            pl.BlockSpec((None, tm, C), lambda b, i: (b, i, 0)),
            pl.BlockSpec((None, HALO_ROWS, C), lambda b, i: (b, jnp.maximum(i * per - 1, 0), 0)),
            pl.BlockSpec((None, HALO_ROWS, C), lambda b, i: (b, 0, 0)),
            pl.BlockSpec((n_lane_blocks, CONV_WIDTH, V7X_LANES), lambda b, i: (0, 0, 0)),
            pl.BlockSpec((n_lane_blocks, 1, V7X_LANES), lambda b, i: (0, 0, 0)),
            vec(), vec(),
        ],
        out_specs=pl.BlockSpec((None, tm, C), lambda b, i: (b, i, 0)),
        out_shape=jax.ShapeDtypeStruct((B, T, C), BF16),
        scratch_shapes=[
            pltpu.VMEM((HALO_ROWS + tm, C), F32),
            pltpu.VMEM((V7X_SUBLANES, n_lane_blocks, tm + HALO_ROWS, V7X_LANES), F32),
            pltpu.VMEM((n_lane_blocks, tm, V7X_LANES), F32),
        ],
        compiler_params=_params(2),
        name="conv",
    )(u, u, init_hist, w_blocks, b_blocks, g_ln, b_ln)


def _mixers(proj, n_streams, attend, conv_hist, p):
    k32, v32, u = proj[-3:]
    T = u.shape[0] // n_streams
    attn = attend(*proj)
    conv_ch = u.shape[1]
    hist = jnp.pad(conv_hist, ((0, 0), (HALO_ROWS - conv_hist.shape[1], 0), (0, 0)))
    conv = _conv_branch(u.reshape(n_streams, T, conv_ch), hist, p["w_dw"], p["b_dw"], p["g_ln"], p["b_ln"])
    return attn, conv.reshape(n_streams * T, conv_ch), k32, v32, u


def kernel(x_prompt, x_sample, cache_k, cache_v, cache_conv, g_ffn1, w_ffn1_gu, w_ffn1_down, g_mix, w_in, g_q, g_k, lambda_q1, lambda_k1, lambda_q2, lambda_k2, g_sub, w_dw, b_dw, g_conv_ln, b_conv_ln, w_out, g_ffn2, w_ffn2_gu, w_ffn2_down, g_final):
    depth = cache_k.shape[0]
    n_p, t_p, d_model = x_prompt.shape
    n_s, t_s, _ = x_sample.shape
    past = cache_k.shape[2]
    conv_ch = cache_conv.shape[-1]
    hist_rows = CONV_WIDTH - 1
    assert n_p == 1, "prompt attention kernel handles one prompt stream"
    assert t_p >= hist_rows and t_s >= hist_rows

    group = jnp.arange(V7X_MXU_WIDTH) // HEAD_DIM
    ones_bd = (group[:, None] == group[None, :]).astype(BF16)
    row = lambda a: a.reshape(1, -1)

    yp = x_prompt.reshape(n_p * t_p, d_model)
    ys = x_sample.reshape(n_s * t_s, d_model)
    outs = [[] for _ in range(6)]
    for l in range(depth):
        lam_init = 0.8 - 0.6 * math.exp(-0.3 * l)
        lam_rows = jnp.stack([lambda_q1[l], lambda_k1[l], lambda_q2[l], lambda_k2[l]])
        p = dict(
            g_ffn1=row(g_ffn1[l]), w1gu=w_ffn1_gu[l].astype(BF16),
            w1d=w_ffn1_down[l].astype(BF16), g_mix=row(g_mix[l]), w_in=w_in[l].astype(BF16),
            gq_t=row(jnp.tile(g_q[l], 2 * N_HEADS)), gk_t=row(jnp.tile(g_k[l], 2 * N_HEADS)), ones_bd=ones_bd,
            w_dw=w_dw[l], b_dw=row(b_dw[l]), g_ln=row(g_conv_ln[l]), b_ln=row(b_conv_ln[l]),
            w_out=w_out[l].astype(BF16), g_ffn2=row(g_ffn2[l]), w2gu=w_ffn2_gu[l].astype(BF16),
            w2d=w_ffn2_down[l].astype(BF16), g_final=row(g_final[l]),
        )
        g_sub_row = row(g_sub[l])

        q_tile, key_tile = _prompt_tiles(t_p)

        def attend_prompt(qs, ks, vt, k32, v32, u):
            return _prompt_attention(qs, ks, vt, lam_rows, g_sub_row, lam_init, q_tile, key_tile)

        def attend_sample(qs, k32, v32, u, l=l):
            ck = jnp.swapaxes(cache_k[l].reshape(n_s, past, Q_WIDTH), 1, 2)
            cv = cache_v[l].reshape(n_s, past * N_HEADS, V_DIM)
            return _sample_attention(qs, k32, v32, ck, cv, lam_rows, g_sub_row, lam_init, n_s)

        zero_hist = jnp.zeros((n_p, hist_rows, conv_ch), F32)
        x1p, x1s = _ffn((yp, ys), p["g_ffn1"], p["w1gu"], p["w1d"], name="ffn1")
        proj_p, proj_s = _project((x1p, x1s), (key_tile, None), p["g_mix"], p["w_in"], p["gq_t"], p["gk_t"],
                                  p["ones_bd"])
        attn_p, conv_p, kp, vp, up = _mixers(proj_p, n_p, attend_prompt, zero_hist, p)
        attn_s, conv_s, ks, vs, us = _mixers(proj_s, n_s, attend_sample, cache_conv[l], p)
        yp, ys = _ffn((x1p, x1s), p["g_ffn2"], p["w2gu"], p["w2d"],
                      mix=((attn_p, attn_s), (conv_p, conv_s), p["w_out"]), g_final=p["g_final"], name="ffn2")
        outs[0].append(kp.reshape(n_p, t_p, N_HEADS, 2, HEAD_DIM))
        outs[1].append(vp.reshape(n_p, t_p, N_HEADS, V_DIM))
        outs[2].append(up.reshape(n_p, t_p, conv_ch)[:, t_p - hist_rows:])
        outs[3].append(ks.reshape(n_s, t_s, N_HEADS, 2, HEAD_DIM))
        outs[4].append(vs.reshape(n_s, t_s, N_HEADS, V_DIM))
        outs[5].append(us.reshape(n_s, t_s, conv_ch)[:, t_s - hist_rows:])

    return (yp.reshape(n_p, t_p, d_model), ys.reshape(n_s, t_s, d_model),
            *[jnp.stack(o, axis=0) for o in outs])
```
